```python
import jax, jax.numpy as jnp
from jax import lax
import numpy as np

D_MODEL = 1024
BATCH = 2
SEQ = 8192
DEPTH = 2

MIX_WIDTH = 512
RG_BLOCKS = 8
RG_BLOCK = MIX_WIDTH // RG_BLOCKS
RG_C = 8.0
CONV_WIDTH = 4
SB_HEADS = 8
SB_HEAD_DIM = MIX_WIDTH // SB_HEADS
Q_BLOCK = 128
HG_HEADS = 4
HG_KEY_DIM = MIX_WIDTH // HG_HEADS
HG_VAL_DIM = MIX_WIDTH // HG_HEADS
HG_CHUNK = 64
D_FF = 4 * D_MODEL
N_BRANCH = 3
N_NORMS = 4
EPS = 1e-6
LB_FLOOR = 1e-30
IN_WIDTHS = (MIX_WIDTH,) * 8 + (D_MODEL,) * 3
D_IN = 8 * MIX_WIDTH + 3 * D_MODEL

kernel_name = "hybrid_rglru_stickbreak_hgrn2_trunk"


def rms_norm(x, gain):
    xf = x.astype(jnp.float32)
    inv = lax.rsqrt(jnp.mean(xf * xf, axis=-1, keepdims=True) + EPS)
    return (xf * inv * gain.astype(jnp.float32)).astype(x.dtype)


def causal_depthwise_conv(x, w, b):
    S = x.shape[1]
    xp = jnp.pad(x, ((0, 0), (CONV_WIDTH - 1, 0), (0, 0)))
    out = sum(xp[:, k:k + S, :] * w[k] for k in range(CONV_WIDTH))
    return out + b


def rg_lru(x, w_a, b_a, w_x, b_x, lam):
    B, S, W = x.shape
    xf = x.astype(jnp.float32)
    xb = xf.reshape(B, S, RG_BLOCKS, RG_BLOCK)
    r = jax.nn.sigmoid(jnp.einsum('bsnc,ncd->bsnd', xb, w_a.astype(jnp.float32)).reshape(B, S, W) + b_a)
    i = jax.nn.sigmoid(jnp.einsum('bsnc,ncd->bsnd', xb, w_x.astype(jnp.float32)).reshape(B, S, W) + b_x)
    log_a = -RG_C * r * jax.nn.softplus(-lam.astype(jnp.float32))
    a = jnp.exp(log_a)
    mult = jnp.sqrt(jnp.maximum(-jnp.expm1(2.0 * log_a), 0.0))
    is_first = (jnp.arange(S) == 0)[None, :, None]
    mult = jnp.where(is_first, 1.0, mult)
    u = mult * (i * xf)

    def combine(e1, e2):
        a1, b1 = e1
        a2, b2 = e2
        return a1 * a2, a2 * b1 + b2

    _, h = lax.associative_scan(combine, (a, u), axis=1)
    return h.astype(x.dtype)


def stick_breaking_attention(q, k, v):
    B, S, H, Dh = q.shape
    n_blk = S // Q_BLOCK
    scale = Dh ** -0.5
    qb = q.astype(jnp.float32).reshape(B, n_blk, Q_BLOCK, H, Dh).transpose(1, 0, 3, 2, 4)
    kh = k.astype(jnp.float32).transpose(0, 2, 1, 3)
    vh = v.astype(jnp.float32).transpose(0, 2, 1, 3)
    s_pos = jnp.arange(S)
    starts = jnp.arange(n_blk) * Q_BLOCK

    def one_block(args):
        q_blk, start = args
        z = jnp.einsum('bhqd,bhsd->bhqs', q_blk, kh) * scale
        t_pos = start + jnp.arange(Q_BLOCK)
        causal = s_pos[None, :] < t_pos[:, None]
        log_keep = jnp.where(causal, jax.nn.log_sigmoid(-z), 0.0)
        suffix = lax.cumsum(log_keep, axis=3, reverse=True) - log_keep
        w = jnp.where(causal, jnp.exp(jax.nn.log_sigmoid(z) + suffix), 0.0)
        return jnp.einsum('bhqs,bhsd->bhqd', w, vh)

    out = lax.map(one_block, (qb, starts))
    out = out.transpose(1, 0, 3, 2, 4).reshape(B, S, H * Dh)
    return out.astype(q.dtype)


def hgrn2_recurrence(q, log_f, k, v):
    B, S, H, K = q.shape
    V = v.shape[-1]
    n = S // HG_CHUNK

    def to_chunks(t):
        return t.astype(jnp.float32).reshape(B, n, HG_CHUNK, H, t.shape[-1]).transpose(1, 0, 3, 2, 4)

    mask = jnp.tril(jnp.ones((HG_CHUNK, HG_CHUNK), dtype=bool))[:, :, None]

    def step(state, inp):
        qc, gc, kc, vc = inp
        b = jnp.cumsum(gc, axis=2)
        diff = b[:, :, :, None, :] - b[:, :, None, :, :]
        decay = jnp.where(mask, jnp.exp(jnp.where(mask, diff, 0.0)), 0.0)
        attn = jnp.einsum('bhtk,bhtsk,bhsk->bhts', qc, decay, kc)
        o = jnp.einsum('bhts,bhsv->bhtv', attn, vc) + jnp.einsum('bhtk,bhkv->bhtv', qc * jnp.exp(b), state)
        b_last = b[:, :, -1:, :]
        new_state = jnp.exp(b_last[:, :, 0, :])[..., None] * state + \
            jnp.einsum('bhsk,bhsv->bhkv', kc * jnp.exp(b_last - b), vc)
        return new_state, o

    init = jnp.zeros((B, H, K, V), jnp.float32)
    _, o = lax.scan(step, init, (to_chunks(q), to_chunks(log_f), to_chunks(k), to_chunks(v)))
    return o.transpose(1, 0, 3, 2, 4).reshape(B, S, H, V)


def hgrn2_branch(q_raw, f_raw, i_raw, g_raw, lb, norm_gain):
    B, S, _ = q_raw.shape
    shp = (B, S, HG_HEADS, HG_KEY_DIM)
    q = jax.nn.silu(q_raw.astype(jnp.float32)).reshape(shp)
    f_pre = f_raw.astype(jnp.float32).reshape(shp)
    lbh = lb.astype(jnp.float32).reshape(HG_HEADS, HG_KEY_DIM)
    log_f = jnp.logaddexp(jnp.log(jnp.maximum(lbh, LB_FLOOR)),
                          jnp.log1p(-lbh) + jax.nn.log_sigmoid(f_pre))
    k = (1.0 - lbh) * jax.nn.sigmoid(-f_pre)
    v = i_raw.astype(jnp.float32).reshape(B, S, HG_HEADS, HG_VAL_DIM)
    o = hgrn2_recurrence(q, log_f, k, v)
    o = o * lax.rsqrt(jnp.mean(o * o, axis=-1, keepdims=True) + EPS) * norm_gain.astype(jnp.float32)
    o = o.reshape(B, S, MIX_WIDTH) * jax.nn.silu(g_raw.astype(jnp.float32))
    return o.astype(q_raw.dtype)


def setup_inputs(seed: int = 0) -> dict:
    key = jax.random.key(seed)
    ks = jax.random.split(key, 20)
    f32 = jnp.float32
    nrm = lambda k, s: jax.random.normal(k, s, f32)
    x = nrm(ks[0], (BATCH, SEQ, D_MODEL))
    ln_gains = 1.0 + 0.02 * nrm(ks[1], (DEPTH, N_NORMS, D_MODEL))
    w_in = nrm(ks[2], (DEPTH, D_MODEL, D_IN)) * D_MODEL ** -0.5
    conv_w = nrm(ks[3], (DEPTH, CONV_WIDTH, MIX_WIDTH)) * CONV_WIDTH ** -0.5
    conv_b = 0.01 * nrm(ks[4], (DEPTH, MIX_WIDTH))
    rg_w_a = nrm(ks[5], (DEPTH, RG_BLOCKS, RG_BLOCK, RG_BLOCK)) * RG_BLOCK ** -0.5
    rg_b_a = 0.01 * nrm(ks[6], (DEPTH, MIX_WIDTH))
    rg_w_x = nrm(ks[7], (DEPTH, RG_BLOCKS, RG_BLOCK, RG_BLOCK)) * RG_BLOCK ** -0.5
    rg_b_x = 0.01 * nrm(ks[8], (DEPTH, MIX_WIDTH))
    u = jax.random.uniform(ks[9], (DEPTH, MIX_WIDTH), f32, minval=0.9, maxval=0.999)
    p = u ** (1.0 / RG_C)
    rg_lambda = jnp.log(p) - jnp.log1p(-p)
    lb_logits = 0.1 * nrm(ks[10], (DEPTH, MIX_WIDTH))
    hgrn_norm = 1.0 + 0.02 * nrm(ks[11], (DEPTH, HG_VAL_DIM))
    w_branch = nrm(ks[12], (DEPTH, N_BRANCH, MIX_WIDTH, D_MODEL)) * MIX_WIDTH ** -0.5
    w_out = nrm(ks[13], (DEPTH, D_MODEL, D_MODEL)) * D_MODEL ** -0.5
    w_up = nrm(ks[14], (DEPTH, D_MODEL, D_FF)) * D_MODEL ** -0.5
    w_down = nrm(ks[15], (DEPTH, D_FF, D_MODEL)) * D_FF ** -0.5
    return {"x": x, "ln_gains": ln_gains, "w_in": w_in, "conv_w": conv_w, "conv_b": conv_b,
            "rg_w_a": rg_w_a, "rg_b_a": rg_b_a, "rg_w_x": rg_w_x, "rg_b_x": rg_b_x,
            "rg_lambda": rg_lambda, "lb_logits": lb_logits, "hgrn_norm": hgrn_norm,
            "w_branch": w_branch, "w_out": w_out, "w_up": w_up, "w_down": w_down}


def reference(x, ln_gains, w_in, conv_w, conv_b, rg_w_a, rg_b_a, rg_w_x, rg_b_x,
              rg_lambda, lb_logits, hgrn_norm, w_branch, w_out, w_up, w_down):
    B, S, _ = x.shape
    split_points = [int(v) for v in np.cumsum(IN_WIDTHS)[:-1]]
    lb_p = jax.nn.softmax(lb_logits.astype(jnp.float32), axis=0)
    lower_bounds = jnp.cumsum(lb_p, axis=0) - lb_p[0:1]
    for l in range(DEPTH):
        h = rms_norm(x, ln_gains[l, 0])
        proj = h @ w_in[l]
        (rg_x, sb_q, sb_k, sb_v, hg_q, hg_f, hg_i, hg_g,
         gate_a, gate_b, gate_c) = jnp.split(proj, split_points, axis=-1)
        y_a = rg_lru(causal_depthwise_conv(rg_x, conv_w[l], conv_b[l]),
                     rg_w_a[l], rg_b_a[l], rg_w_x[l], rg_b_x[l], rg_lambda[l])
        hs = (B, S, SB_HEADS, SB_HEAD_DIM)
        y_b = stick_breaking_attention(sb_q.reshape(hs), sb_k.reshape(hs), sb_v.reshape(hs))
        y_c = hgrn2_branch(hg_q, hg_f, hg_i, hg_g, lower_bounds[l], hgrn_norm[l])
        merged = (jax.nn.sigmoid(gate_a) * (y_a @ w_branch[l, 0])
                  + jax.nn.sigmoid(gate_b) * (y_b @ w_branch[l, 1])
                  + jax.nn.sigmoid(gate_c) * (y_c @ w_branch[l, 2]))
        x = x + rms_norm(merged @ w_out[l], ln_gains[l, 1])
        h2 = rms_norm(x, ln_gains[l, 2])
        m = jnp.square(jax.nn.relu(h2 @ w_up[l])) @ w_down[l]
        x = x + rms_norm(m, ln_gains[l, 3])
    return x
```

```python
import functools

import jax
import jax.numpy as jnp
from jax import lax
from jax.experimental import pallas as pl
from jax.experimental.pallas import tpu as pltpu

F32 = jnp.float32
BF16 = jnp.bfloat16

EPS = 1e-6
RG_C = 8.0
CONV_WIDTH = 4
SB_HEAD_DIM = 64
HG_HEAD_DIM = 128
LB_FLOOR = 1e-30

LANES = 128
SUBLANES = 8
VMEM_LIMIT = 56 * 1024 * 1024

IN_TM, IN_TN = 512, 1024
RG_TB = 512
SB_BLK = 256
HG_TB = 512
HG_CHUNK = 64
HG_SUB = 16
MERGE_TM = 256
MLP_TM, MLP_TF = 512, 1024


def _params(*sem):
    return pltpu.CompilerParams(dimension_semantics=sem, vmem_limit_bytes=VMEM_LIMIT)


def _softplus(x):
    return jnp.maximum(x, 0.0) + jnp.log1p(jnp.exp(-jnp.abs(x)))


def _expm1_nonpos(y):
    u = jnp.exp(y)
    near = (y > -0.5) & (u != 1.0)
    us = jnp.where(near, u, 2.0)
    return jnp.where(near, (us - 1.0) * y / jnp.log(us), jnp.where(y > -0.5, y, u - 1.0))


def _rms_scale(x, gain):
    inv = lax.rsqrt(jnp.mean(x * x, axis=-1, keepdims=True) + EPS)
    return x * inv * gain


def _in_proj_kernel(x_ref, g_ref, w_ref, o_ref, h_ref):
    @pl.when(pl.program_id(1) == 0)
    def _():
        h_ref[...] = _rms_scale(x_ref[...], g_ref[...]).astype(BF16)

    o_ref[...] = jnp.dot(h_ref[...], w_ref[...], preferred_element_type=F32)


def _in_proj(x2, gain, w_bf):
    t, d = x2.shape
    n = w_bf.shape[1]
    tm, tn = min(IN_TM, t), min(IN_TN, n)
    return pl.pallas_call(
        _in_proj_kernel,
        grid=(t // tm, n // tn),
        in_specs=[
            pl.BlockSpec((tm, d), lambda i, j: (i, 0)),
            pl.BlockSpec((1, d), lambda i, j: (0, 0)),
            pl.BlockSpec((d, tn), lambda i, j: (0, j)),
        ],
        out_specs=pl.BlockSpec((tm, tn), lambda i, j: (i, j)),
        out_shape=jax.ShapeDtypeStruct((t, n), F32),
        scratch_shapes=[pltpu.VMEM((tm, d), BF16)],
        compiler_params=_params("parallel", "arbitrary"),
        name="in_proj",
    )(x2, gain, w_bf)


def _rglru_kernel(x_ref, cw_ref, cb_ref, wa_ref, ba_ref, wx_ref, bx_ref, lam_ref,
                  o_ref, xpad_ref, a_ref, u_ref, h_ref, *, tb):
    j = pl.program_id(1)
    w = x_ref.shape[-1]

    @pl.when(j == 0)
    def _():
        xpad_ref[0:SUBLANES, :] = jnp.zeros((SUBLANES, w), F32)
        h_ref[...] = jnp.zeros((1, w), F32)

    x = x_ref[0]
    xpad_ref[SUBLANES:SUBLANES + tb, :] = x
    xc = cb_ref[...]
    for k in range(CONV_WIDTH):
        off = SUBLANES - (CONV_WIDTH - 1) + k
        xc = xc + cw_ref[k:k + 1, :] * xpad_ref[off:off + tb, :]
    xpad_ref[0:SUBLANES, :] = x[tb - SUBLANES:tb, :]

    xcb = xc.astype(BF16)
    r = jax.nn.sigmoid(jnp.dot(xcb, wa_ref[...], preferred_element_type=F32) + ba_ref[...])
    gate_i = jax.nn.sigmoid(jnp.dot(xcb, wx_ref[...], preferred_element_type=F32) + bx_ref[...])
    log_a = -RG_C * r * _softplus(-lam_ref[...])
    a = jnp.exp(log_a)
    mult = jnp.sqrt(jnp.maximum(-_expm1_nonpos(2.0 * log_a), 0.0))
    row = lax.broadcasted_iota(jnp.int32, (tb, w), 0) + j * tb
    mult = jnp.where(row == 0, 1.0, mult)
    a_ref[...] = a
    u_ref[...] = mult * (gate_i * xc)

    def step(t, h):
        h = a_ref[pl.ds(t, 1), :] * h + u_ref[pl.ds(t, 1), :]
        o_ref[0, pl.ds(t, 1), :] = h
        return h

    h_ref[...] = lax.fori_loop(0, tb, step, h_ref[...], unroll=SUBLANES)


def _rglru(proj, cw, cb, wa_bd, ba, wx_bd, bx, lam, mix):
    b, s, _ = proj.shape
    tb = min(RG_TB, s)
    vec = lambda: pl.BlockSpec((1, mix), lambda i, j: (0, 0))
    return pl.pallas_call(
        functools.partial(_rglru_kernel, tb=tb),
        grid=(b, s // tb),
        in_specs=[
            pl.BlockSpec((1, tb, mix), lambda i, j: (i, j, 0)),
            pl.BlockSpec((CONV_WIDTH, mix), lambda i, j: (0, 0)),
            vec(),
            pl.BlockSpec((mix, mix), lambda i, j: (0, 0)),
            vec(),
            pl.BlockSpec((mix, mix), lambda i, j: (0, 0)),
            vec(),
            vec(),
        ],
        out_specs=pl.BlockSpec((1, tb, mix), lambda i, j: (i, j, 0)),
        out_shape=jax.ShapeDtypeStruct((b, s, mix), F32),
        scratch_shapes=[
            pltpu.VMEM((tb + SUBLANES, mix), F32),
            pltpu.VMEM((tb, mix), F32),
            pltpu.VMEM((tb, mix), F32),
            pltpu.VMEM((1, mix), F32),
        ],
        compiler_params=_params("parallel", "arbitrary"),
        name="rglru",
    )(proj, cw, cb, wa_bd, ba, wx_bd, bx, lam)


def _sb_kernel(q_ref, k_ref, v_ref, tri_ref, o_ref, *, blk, dh):
    qi = pl.program_id(2)
    scale = dh ** -0.5
    row = lax.broadcasted_iota(jnp.int32, (blk, blk), 0)
    col = lax.broadcasted_iota(jnp.int32, (blk, blk), 1)
    causal = col < row
    tri = tri_ref[...]

    for hh in range(LANES // dh):
        lanes = slice(hh * dh, (hh + 1) * dh)
        q = (q_ref[0, :, lanes] * scale).astype(BF16)

        def block(kj, carry, acc, diag):
            rows = pl.ds(pl.multiple_of(kj * blk, blk), blk)
            kb = k_ref[0, rows, lanes].astype(BF16)
            vb = v_ref[0, rows, lanes].astype(BF16)
            z = lax.dot_general(q, kb, (((1,), (1,)), ((), ())), preferred_element_type=F32)
            sp = _softplus(z)
            if diag:
                sp = jnp.where(causal, sp, 0.0)
            inner = jnp.dot(sp.astype(BF16), tri, preferred_element_type=F32)
            wgt = jnp.exp(z - sp - inner - carry)
            if diag:
                wgt = jnp.where(causal, wgt, 0.0)
            acc = acc + jnp.dot(wgt.astype(BF16), vb, preferred_element_type=F32)
            carry = carry + jnp.sum(sp, axis=1, keepdims=True)
            return carry, acc

        carry, acc = block(qi, jnp.zeros((blk, 1), F32), jnp.zeros((blk, dh), F32), True)

        def body(i, c):
            return block(qi - 1 - i, c[0], c[1], False)

        carry, acc = lax.fori_loop(0, qi, body, (carry, acc))
        o_ref[0, :, lanes] = acc


def _stick_breaking(proj, tri, mix, q_col0, k_col0, v_col0):
    b, s, _ = proj.shape
    blk = min(SB_BLK, s)
    dh = SB_HEAD_DIM
    n_grp = mix // LANES
    qb, kb, vb = q_col0 // LANES, k_col0 // LANES, v_col0 // LANES
    return pl.pallas_call(
        functools.partial(_sb_kernel, blk=blk, dh=dh),
        grid=(b, n_grp, s // blk),
        in_specs=[
            pl.BlockSpec((1, blk, LANES), lambda i, g, t: (i, t, qb + g)),
            pl.BlockSpec((1, s, LANES), lambda i, g, t: (i, 0, kb + g)),
            pl.BlockSpec((1, s, LANES), lambda i, g, t: (i, 0, vb + g)),
            pl.BlockSpec((blk, blk), lambda i, g, t: (0, 0)),
        ],
        out_specs=pl.BlockSpec((1, blk, LANES), lambda i, g, t: (i, t, g)),
        out_shape=jax.ShapeDtypeStruct((b, s, mix), F32),
        compiler_params=_params("parallel", "parallel", "arbitrary"),
        name="stick_breaking",
    )(proj, proj, proj, tri)


def _hgrn2_kernel(q_ref, f_ref, i_ref, g_ref, lbl_ref, ng_ref, sel_ref, o_ref,
                  st_ref, w_ref, *, layer, tb, chunk, sub):
    mix = q_ref.shape[-1]
    hk = HG_HEAD_DIM
    n_head = mix // hk
    n_sub = chunk // sub

    @pl.when(pl.program_id(1) == 0)
    def _():
        st_ref[...] = jnp.zeros(st_ref.shape, F32)

    lbl = lbl_ref[...]
    p = jnp.exp(lbl - jnp.max(lbl, axis=0, keepdims=True))
    p = p / jnp.sum(p, axis=0, keepdims=True)
    lb = jnp.zeros((1, mix), F32)
    for m in range(1, layer + 1):
        lb = lb + p[m:m + 1, :]
    log_lb = jnp.log(jnp.maximum(lb, LB_FLOOR))
    log_1m_lb = jnp.log1p(-lb)

    row = lax.broadcasted_iota(jnp.int32, (chunk, chunk), 0)
    col = lax.broadcasted_iota(jnp.int32, (chunk, chunk), 1)
    same_sub = (row // sub) == (col // sub)
    later_sub = (col // sub) > (row // sub)
    causal = row <= col
    trow = lax.broadcasted_iota(jnp.int32, (chunk, mix), 0)

    def chunk_body(c, _):
        rows = pl.ds(pl.multiple_of(c * chunk, chunk), chunk)
        q_raw = q_ref[0, rows, :]
        f_pre = f_ref[0, rows, :]
        v = i_ref[0, rows, :]
        g_raw = g_ref[0, rows, :]
        q = q_raw * jax.nn.sigmoid(q_raw)
        t2 = log_1m_lb - _softplus(-f_pre)
        log_f = jnp.maximum(log_lb, t2) + jnp.log1p(jnp.exp(-jnp.abs(log_lb - t2)))
        k = (1.0 - lb) * jax.nn.sigmoid(-f_pre)
        bcum = log_f
        d = 1
        while d < chunk:
            bcum = bcum + jnp.where(trow >= d, pltpu.roll(bcum, d, 0), 0.0)
            d *= 2
        b_last = bcum[chunk - 1:chunk, :]
        vb = v.astype(BF16)
        q_in = (q * jnp.exp(bcum)).astype(BF16)
        k_st = (k * jnp.exp(b_last - bcum)).astype(BF16)
        qk = q * k
        decay_last = jnp.exp(b_last)

        b3 = bcum.reshape(n_sub, sub, mix)
        q3 = q.reshape(n_sub, sub, mix)
        k3 = k.reshape(n_sub, sub, mix)
        for tl in range(sub):
            dec = jnp.exp(jnp.minimum(b3[:, tl:tl + 1, :] - b3, 0.0))
            tile = (dec * (q3[:, tl:tl + 1, :] * k3)).reshape(chunk, mix).astype(BF16)
            for hd in range(n_head):
                w_ref[hd * chunk:(hd + 1) * chunk, tl * hk:(tl + 1) * hk] = tile[:, hd * hk:(hd + 1) * hk]
        diag_all = jnp.dot(w_ref[...], sel_ref[...], preferred_element_type=F32)

        for hd in range(n_head):
            hl = slice(hd * hk, (hd + 1) * hk)
            at = jnp.where(same_sub & causal, diag_all[hd * chunk:(hd + 1) * chunk, :], 0.0)
            pieces = []
            for j in range(n_sub - 1):
                r_j = bcum[(j + 1) * sub - 1:(j + 1) * sub, hl]
                srows = slice(j * sub, (j + 1) * sub)
                k_hat = (k[srows, hl] * jnp.exp(r_j - bcum[srows, hl])).astype(BF16)
                q_j = (q[:, hl] * jnp.exp(jnp.minimum(bcum[:, hl] - r_j, 0.0))).astype(BF16)
                pieces.append(lax.dot_general(k_hat, q_j, (((1,), (1,)), ((), ())),
                                              preferred_element_type=F32))
            if pieces:
                pieces.append(jnp.zeros((sub, chunk), F32))
                at = jnp.where(later_sub, jnp.concatenate(pieces, axis=0), at)
            st = st_ref[hd]
            o = lax.dot_general(at.astype(BF16), vb[:, hl], (((0,), (0,)), ((), ())),
                                preferred_element_type=F32)
            o = o + lax.dot_general(q_in[:, hl], st.astype(BF16), (((1,), (1,)), ((), ())),
                                    preferred_element_type=F32)
            st_ref[hd] = decay_last[:, hl] * st + lax.dot_general(
                vb[:, hl], k_st[:, hl], (((0,), (0,)), ((), ())), preferred_element_type=F32)
            o = _rms_scale(o, ng_ref[...])
            g_h = g_raw[:, hl]
            o_ref[0, rows, hl] = o * (g_h * jax.nn.sigmoid(g_h))
        return 0

    lax.fori_loop(0, tb // chunk, chunk_body, 0)


def _hgrn2(proj, lb_logits, norm_gain, sel, layer, mix, col0):
    b, s, _ = proj.shape
    tb = min(HG_TB, s)
    chunk, sub = HG_CHUNK, HG_SUB
    n_head = mix // HG_HEAD_DIM
    depth = lb_logits.shape[0]
    cb = col0 // mix
    col = lambda off: pl.BlockSpec((1, tb, mix), lambda i, j: (i, j, cb + off))
    return pl.pallas_call(
        functools.partial(_hgrn2_kernel, layer=layer, tb=tb, chunk=chunk, sub=sub),
        grid=(b, s // tb),
        in_specs=[
            col(0), col(1), col(2), col(3),
            pl.BlockSpec((depth, mix), lambda i, j: (0, 0)),
            pl.BlockSpec((1, HG_HEAD_DIM), lambda i, j: (0, 0)),
            pl.BlockSpec((sub * HG_HEAD_DIM, chunk), lambda i, j: (0, 0)),
        ],
        out_specs=pl.BlockSpec((1, tb, mix), lambda i, j: (i, j, 0)),
        out_shape=jax.ShapeDtypeStruct((b, s, mix), F32),
        scratch_shapes=[
            pltpu.VMEM((n_head, HG_HEAD_DIM, HG_HEAD_DIM), F32),
            pltpu.VMEM((n_head * chunk, sub * HG_HEAD_DIM), BF16),
        ],
        compiler_params=_params("parallel", "arbitrary"),
        name="hgrn2",
    )(proj, proj, proj, proj, lb_logits, norm_gain, sel)


def _merge_kernel(ya_ref, yb_ref, yc_ref, ga_ref, gb_ref, gc_ref, x_ref,
                  wbr_ref, wout_ref, gain_ref, o_ref):
    merged = None
    for idx, (y_ref, gt_ref) in enumerate(((ya_ref, ga_ref), (yb_ref, gb_ref), (yc_ref, gc_ref))):
        part = jnp.dot(y_ref[...].astype(BF16), wbr_ref[idx], preferred_element_type=F32)
        part = jax.nn.sigmoid(gt_ref[...]) * part
        merged = part if merged is None else merged + part
    out = jnp.dot(merged.astype(BF16), wout_ref[...], preferred_element_type=F32)
    o_ref[...] = x_ref[...] + _rms_scale(out, gain_ref[...])


def _merge(ya, yb, yc, proj2, x2, wbr_bf, wout_bf, gain, gate_col0):
    t, d = x2.shape
    mix = ya.shape[1]
    tm = min(MERGE_TM, t)
    gcb = gate_col0 // d
    ybs = lambda: pl.BlockSpec((tm, mix), lambda i: (i, 0))
    gate = lambda off: pl.BlockSpec((tm, d), lambda i: (i, gcb + off))
    return pl.pallas_call(
        _merge_kernel,
        grid=(t // tm,),
        in_specs=[
            ybs(), ybs(), ybs(), gate(0), gate(1), gate(2),
            pl.BlockSpec((tm, d), lambda i: (i, 0)),
            pl.BlockSpec(wbr_bf.shape, lambda i: (0, 0, 0)),
            pl.BlockSpec((d, d), lambda i: (0, 0)),
            pl.BlockSpec((1, d), lambda i: (0, 0)),
        ],
        out_specs=pl.BlockSpec((tm, d), lambda i: (i, 0)),
        out_shape=jax.ShapeDtypeStruct((t, d), F32),
        compiler_params=_params("parallel"),
        name="merge_out",
    )(ya, yb, yc, proj2, proj2, proj2, x2, wbr_bf, wout_bf, gain)


def _mlp_kernel(x_ref, g_in_ref, wup_ref, wdn_ref, g_out_ref, o_ref, h_ref, acc_ref):
    f = pl.program_id(1)

    @pl.when(f == 0)
    def _():
        h_ref[...] = _rms_scale(x_ref[...], g_in_ref[...]).astype(BF16)
        acc_ref[...] = jnp.zeros(acc_ref.shape, F32)

    u = jnp.maximum(jnp.dot(h_ref[...], wup_ref[...], preferred_element_type=F32), 0.0)
    acc_ref[...] += jnp.dot((u * u).astype(BF16), wdn_ref[...], preferred_element_type=F32)

    @pl.when(f == pl.num_programs(1) - 1)
    def _():
        o_ref[...] = x_ref[...] + _rms_scale(acc_ref[...], g_out_ref[...])


def _mlp(x2, g_in, wup_bf, wdn_bf, g_out):
    t, d = x2.shape
    dff = wup_bf.shape[1]
    tm, tf = min(MLP_TM, t), min(MLP_TF, dff)
    return pl.pallas_call(
        _mlp_kernel,
        grid=(t // tm, dff // tf),
        in_specs=[
            pl.BlockSpec((tm, d), lambda i, f: (i, 0)),
            pl.BlockSpec((1, d), lambda i, f: (0, 0)),
            pl.BlockSpec((d, tf), lambda i, f: (0, f)),
            pl.BlockSpec((tf, d), lambda i, f: (f, 0)),
            pl.BlockSpec((1, d), lambda i, f: (0, 0)),
        ],
        out_specs=pl.BlockSpec((tm, d), lambda i, f: (i, 0)),
        out_shape=jax.ShapeDtypeStruct((t, d), F32),
        scratch_shapes=[pltpu.VMEM((tm, d), BF16), pltpu.VMEM((tm, d), F32)],
        compiler_params=_params("parallel", "arbitrary"),
        name="mlp",
    )(x2, g_in, wup_bf, wdn_bf, g_out)


def _block_diag(w):
    n, c, _ = w.shape
    eye = jnp.eye(n, dtype=w.dtype)
    return (eye[:, None, :, None] * w[:, :, None, :]).reshape(n * c, n * c)


def kernel(x, ln_gains, w_in, conv_w, conv_b, rg_w_a, rg_b_a, rg_w_x, rg_b_x, rg_lambda,
           lb_logits, hgrn_norm, w_branch, w_out, w_up, w_down):
    b, s, d = x.shape
    depth = w_in.shape[0]
    mix = conv_w.shape[-1]
    t = b * s
    sb_col0, hg_col0, gate_col0 = mix, 4 * mix, 8 * mix

    blk = min(SB_BLK, s)
    tri = (lax.broadcasted_iota(jnp.int32, (blk, blk), 0)
           > lax.broadcasted_iota(jnp.int32, (blk, blk), 1)).astype(BF16)
    sel_r = lax.broadcasted_iota(jnp.int32, (HG_SUB * HG_HEAD_DIM, HG_CHUNK), 0) // HG_HEAD_DIM
    sel_c = lax.broadcasted_iota(jnp.int32, (HG_SUB * HG_HEAD_DIM, HG_CHUNK), 1) % HG_SUB
    sel = (sel_r == sel_c).astype(BF16)

    x2 = x.reshape(t, d)
    for l in range(depth):
        g = ln_gains[l]
        proj2 = _in_proj(x2, g[0:1], w_in[l].astype(BF16))
        proj = proj2.reshape(b, s, -1)
        y_a = _rglru(proj, conv_w[l], conv_b[l:l + 1],
                     _block_diag(rg_w_a[l]).astype(BF16), rg_b_a[l:l + 1],
                     _block_diag(rg_w_x[l]).astype(BF16), rg_b_x[l:l + 1],
                     rg_lambda[l:l + 1], mix)
        y_b = _stick_breaking(proj, tri, mix, sb_col0, sb_col0 + mix, sb_col0 + 2 * mix)
        y_c = _hgrn2(proj, lb_logits, hgrn_norm[l:l + 1], sel, l, mix, hg_col0)
        x2 = _merge(y_a.reshape(t, mix), y_b.reshape(t, mix), y_c.reshape(t, mix), proj2, x2,
                    w_branch[l].astype(BF16), w_out[l].astype(BF16), g[1:2], gate_col0)
        x2 = _mlp(x2, g[2:3], w_up[l].astype(BF16), w_down[l].astype(BF16), g[3:4])
    return x2.reshape(b, s, d)
```

```python
import functools

import jax
import jax.numpy as jnp
from jax import lax
from jax.experimental import pallas as pl
from jax.experimental.pallas import tpu as pltpu

F32 = jnp.float32
BF16 = jnp.bfloat16

EPS = 1e-6
RG_C = 8.0
CONV_WIDTH = 4
SB_HEAD_DIM = 64
HG_HEAD_DIM = 128
LB_FLOOR = 1e-30
SB_ZERO_EXP = 105.0

LANES = 128
SUBLANES = 8
VMEM_LIMIT = 56 * 1024 * 1024

IN_TM = 512
RG_TB = 512
SB_BLK = 256
SB_HEADS_PER_STEP = 4
HG_TB = 512
HG_CHUNK = 64
HG_SUB = 16
MERGE_TM = 256
MLP_TM, MLP_TF = 512, 1024


def _params(*sem):
    return pltpu.CompilerParams(dimension_semantics=sem, vmem_limit_bytes=VMEM_LIMIT)


def _softplus(x):
    return jnp.maximum(x, 0.0) + jnp.log1p(jnp.exp(-jnp.abs(x)))


def _expm1_nonpos(y):
    u = jnp.exp(y)
    near = (y > -0.5) & (u != 1.0)
    us = jnp.where(near, u, 2.0)
    return jnp.where(near, (us - 1.0) * y / jnp.log(us), jnp.where(y > -0.5, y, u - 1.0))


def _rms_scale(x, gain):
    inv = lax.rsqrt(jnp.mean(x * x, axis=-1, keepdims=True) + EPS)
    return x * inv * gain


def _in_proj_kernel(x_ref, g_ref, w_ref, o_ref, h_ref, *, first_tile_scale):
    j = pl.program_id(1)

    @pl.when(j == 0)
    def _():
        h_ref[...] = _rms_scale(x_ref[...], g_ref[...]).astype(BF16)

    out = jnp.dot(h_ref[...], w_ref[...], preferred_element_type=F32)
    if first_tile_scale != 1.0:
        out = out * jnp.where(j == 0, first_tile_scale, 1.0)
    o_ref[...] = out.astype(o_ref.dtype)


def _in_proj(x2, gain, w_bf, tn, out_dtype, first_tile_scale=1.0):
    t, d = x2.shape
    n = w_bf.shape[1]
    tm = min(IN_TM, t)
    return pl.pallas_call(
        functools.partial(_in_proj_kernel, first_tile_scale=first_tile_scale),
        grid=(t // tm, n // tn),
        in_specs=[
            pl.BlockSpec((tm, d), lambda i, j: (i, 0)),
            pl.BlockSpec((1, d), lambda i, j: (0, 0)),
            pl.BlockSpec((d, tn), lambda i, j: (0, j)),
        ],
        out_specs=pl.BlockSpec((tm, tn), lambda i, j: (i, j)),
        out_shape=jax.ShapeDtypeStruct((t, n), out_dtype),
        scratch_shapes=[pltpu.VMEM((tm, d), BF16)],
        compiler_params=_params("parallel", "arbitrary"),
        name="in_proj",
    )(x2, gain, w_bf)


def _rglru_kernel(x_ref, cw_ref, cb_ref, wa_ref, ba_ref, wx_ref, bx_ref, lam_ref,
                  o_ref, xpad_ref, a_ref, u_ref, h_ref, *, tb):
    j = pl.program_id(1)
    w = x_ref.shape[-1]

    @pl.when(j == 0)
    def _():
        xpad_ref[0:SUBLANES, :] = jnp.zeros((SUBLANES, w), F32)
        h_ref[...] = jnp.zeros((1, w), F32)

    x = x_ref[0]
    xpad_ref[SUBLANES:SUBLANES + tb, :] = x
    xc = cb_ref[...]
    for k in range(CONV_WIDTH):
        off = SUBLANES - (CONV_WIDTH - 1) + k
        xc = xc + cw_ref[k:k + 1, :] * xpad_ref[off:off + tb, :]
    xpad_ref[0:SUBLANES, :] = x[tb - SUBLANES:tb, :]

    xcb = xc.astype(BF16)
    r = jax.nn.sigmoid(jnp.dot(xcb, wa_ref[...], preferred_element_type=F32) + ba_ref[...])
    gate_i = jax.nn.sigmoid(jnp.dot(xcb, wx_ref[...], preferred_element_type=F32) + bx_ref[...])
    log_a = -RG_C * r * _softplus(-lam_ref[...])
    a = jnp.exp(log_a)
    mult = jnp.sqrt(jnp.maximum(-_expm1_nonpos(2.0 * log_a), 0.0))
    row = lax.broadcasted_iota(jnp.int32, (tb, w), 0) + j * tb
    mult = jnp.where(row == 0, 1.0, mult)
    a_ref[...] = a
    u_ref[...] = mult * (gate_i * xc)

    def step(t, h):
        h = a_ref[pl.ds(t, 1), :] * h + u_ref[pl.ds(t, 1), :]
        o_ref[0, pl.ds(t, 1), :] = h
        return h

    h_ref[...] = lax.fori_loop(0, tb, step, h_ref[...], unroll=SUBLANES)


def _rglru(proj, cw, cb, wa_bd, ba, wx_bd, bx, lam, mix, col0):
    b, s, _ = proj.shape
    tb = min(RG_TB, s)
    cb0 = col0 // mix
    vec = lambda: pl.BlockSpec((1, mix), lambda i, j: (0, 0))
    return pl.pallas_call(
        functools.partial(_rglru_kernel, tb=tb),
        grid=(b, s // tb),
        in_specs=[
            pl.BlockSpec((1, tb, mix), lambda i, j: (i, j, cb0)),
            pl.BlockSpec((CONV_WIDTH, mix), lambda i, j: (0, 0)),
            vec(),
            pl.BlockSpec((mix, mix), lambda i, j: (0, 0)),
            vec(),
            pl.BlockSpec((mix, mix), lambda i, j: (0, 0)),
            vec(),
            vec(),
        ],
        out_specs=pl.BlockSpec((1, tb, mix), lambda i, j: (i, j, 0)),
        out_shape=jax.ShapeDtypeStruct((b, s, mix), F32),
        scratch_shapes=[
            pltpu.VMEM((tb + SUBLANES, mix), F32),
            pltpu.VMEM((tb, mix), F32),
            pltpu.VMEM((tb, mix), F32),
            pltpu.VMEM((1, mix), F32),
        ],
        compiler_params=_params("parallel", "arbitrary"),
        name="rglru",
    )(proj, cw, cb, wa_bd, ba, wx_bd, bx, lam)


def _sb_kernel(q_ref, k_ref, v_ref, tri_ref, o_ref, acc_ref, car_ref, *, blk, dh, n_h):
    qi = pl.program_id(2)
    row = lax.broadcasted_iota(jnp.int32, (blk, blk), 0)
    col = lax.broadcasted_iota(jnp.int32, (blk, blk), 1)
    causal = col < row
    tri = tri_ref[...]
    heads = [slice(h * dh, (h + 1) * dh) for h in range(n_h)]
    qs = [q_ref[0, :, lanes] for lanes in heads]

    acc_ref[...] = jnp.zeros(acc_ref.shape, F32)
    car_ref[...] = jnp.zeros(car_ref.shape, F32)

    def block(kj, diag):
        rows = pl.ds(pl.multiple_of(kj * blk, blk), blk)
        carry_min = None
        for h, lanes in enumerate(heads):
            kb = k_ref[0, rows, lanes]
            vb = v_ref[0, rows, lanes]
            z = lax.dot_general(qs[h], kb, (((1,), (1,)), ((), ())), preferred_element_type=F32)
            sp = jnp.maximum(z, 0.0) + jnp.log(1.0 + jnp.exp(-jnp.abs(z)))
            log_beta = z - sp
            if diag:
                sp = jnp.where(causal, sp, 0.0)
            inner = jnp.dot(sp.astype(BF16), tri, preferred_element_type=F32)
            carry = car_ref[h]
            wgt = jnp.exp(log_beta - inner - carry)
            if diag:
                wgt = jnp.where(causal, wgt, 0.0)
            acc_ref[h] += jnp.dot(wgt.astype(BF16), vb, preferred_element_type=F32)
            carry = carry + jnp.sum(sp, axis=1, keepdims=True)
            car_ref[h] = carry
            c_min = jnp.min(carry)
            carry_min = c_min if carry_min is None else jnp.minimum(carry_min, c_min)
        return carry_min

    def cond(c):
        return jnp.logical_and(c[0] >= 0, c[1] < SB_ZERO_EXP)

    def body(c):
        return c[0] - 1, block(c[0], False)

    lax.while_loop(cond, body, (qi - 1, block(qi, True)))
    for h, lanes in enumerate(heads):
        o_ref[0, :, lanes] = acc_ref[h]


def _stick_breaking(qkv, tri, mix):
    b, s, _ = qkv.shape
    blk = min(SB_BLK, s)
    dh = SB_HEAD_DIM
    width = SB_HEADS_PER_STEP * dh
    n_grp = mix // width
    return pl.pallas_call(
        functools.partial(_sb_kernel, blk=blk, dh=dh, n_h=SB_HEADS_PER_STEP),
        grid=(b, n_grp, s // blk),
        in_specs=[
            pl.BlockSpec((1, blk, width), lambda i, g, t: (i, t, g)),
            pl.BlockSpec((1, s, width), lambda i, g, t: (i, 0, n_grp + g)),
            pl.BlockSpec((1, s, width), lambda i, g, t: (i, 0, 2 * n_grp + g)),
            pl.BlockSpec((blk, blk), lambda i, g, t: (0, 0)),
        ],
        out_specs=pl.BlockSpec((1, blk, width), lambda i, g, t: (i, t, g)),
        out_shape=jax.ShapeDtypeStruct((b, s, mix), F32),
        scratch_shapes=[
            pltpu.VMEM((SB_HEADS_PER_STEP, blk, dh), F32),
            pltpu.VMEM((SB_HEADS_PER_STEP, blk, 1), F32),
        ],
        compiler_params=_params("parallel", "parallel", "arbitrary"),
        name="stick_breaking",
    )(qkv, qkv, qkv, tri)


def _hgrn2_kernel(q_ref, f_ref, i_ref, g_ref, lbl_ref, ng_ref, sel_ref, o_ref,
                  st_ref, w_ref, *, layer, tb, chunk, sub):
    mix = q_ref.shape[-1]
    hk = HG_HEAD_DIM
    n_head = mix // hk
    n_sub = chunk // sub

    @pl.when(pl.program_id(1) == 0)
    def _():
        st_ref[...] = jnp.zeros(st_ref.shape, F32)

    lbl = lbl_ref[...]
    p = jnp.exp(lbl - jnp.max(lbl, axis=0, keepdims=True))
    p = p / jnp.sum(p, axis=0, keepdims=True)
    lb = jnp.zeros((1, mix), F32)
    for m in range(1, layer + 1):
        lb = lb + p[m:m + 1, :]
    log_lb = jnp.log(jnp.maximum(lb, LB_FLOOR))
    log_1m_lb = jnp.log1p(-lb)

    row = lax.broadcasted_iota(jnp.int32, (chunk, chunk), 0)
    col = lax.broadcasted_iota(jnp.int32, (chunk, chunk), 1)
    same_sub = (row // sub) == (col // sub)
    later_sub = (col // sub) > (row // sub)
    causal = row <= col
    trow = lax.broadcasted_iota(jnp.int32, (chunk, mix), 0)

    def chunk_body(c, _):
        rows = pl.ds(pl.multiple_of(c * chunk, chunk), chunk)
        q_raw = q_ref[0, rows, :]
        f_pre = f_ref[0, rows, :]
        v = i_ref[0, rows, :]
        g_raw = g_ref[0, rows, :]
        q = q_raw * jax.nn.sigmoid(q_raw)
        t2 = log_1m_lb - _softplus(-f_pre)
        log_f = jnp.maximum(log_lb, t2) + jnp.log1p(jnp.exp(-jnp.abs(log_lb - t2)))
        k = (1.0 - lb) * jax.nn.sigmoid(-f_pre)
        bcum = log_f
        d = 1
        while d < chunk:
            bcum = bcum + jnp.where(trow >= d, pltpu.roll(bcum, d, 0), 0.0)
            d *= 2
        b_last = bcum[chunk - 1:chunk, :]
        vb = v.astype(BF16)
        q_in = (q * jnp.exp(bcum)).astype(BF16)
        k_st = (k * jnp.exp(b_last - bcum)).astype(BF16)
        decay_last = jnp.exp(b_last)

        b3 = bcum.reshape(n_sub, sub, mix)
        q3 = q.reshape(n_sub, sub, mix)
        k3 = k.reshape(n_sub, sub, mix)
        for tl in range(sub):
            dec = jnp.exp(jnp.minimum(b3[:, tl:tl + 1, :] - b3, 0.0))
            tile = (dec * (q3[:, tl:tl + 1, :] * k3)).reshape(chunk, mix).astype(BF16)
            for hd in range(n_head):
                w_ref[hd * chunk:(hd + 1) * chunk, tl * hk:(tl + 1) * hk] = tile[:, hd * hk:(hd + 1) * hk]
        diag_all = jnp.dot(w_ref[...], sel_ref[...], preferred_element_type=F32)

        for hd in range(n_head):
            hl = slice(hd * hk, (hd + 1) * hk)
            at = jnp.where(same_sub & causal, diag_all[hd * chunk:(hd + 1) * chunk, :], 0.0)
            pieces = []
            for j in range(n_sub - 1):
                r_j = bcum[(j + 1) * sub - 1:(j + 1) * sub, hl]
                srows = slice(j * sub, (j + 1) * sub)
                k_hat = (k[srows, hl] * jnp.exp(r_j - bcum[srows, hl])).astype(BF16)
                q_j = (q[:, hl] * jnp.exp(jnp.minimum(bcum[:, hl] - r_j, 0.0))).astype(BF16)
                pieces.append(lax.dot_general(k_hat, q_j, (((1,), (1,)), ((), ())),
                                              preferred_element_type=F32))
            if pieces:
                pieces.append(jnp.zeros((sub, chunk), F32))
                at = jnp.where(later_sub, jnp.concatenate(pieces, axis=0), at)
            st = st_ref[hd]
            o = lax.dot_general(at.astype(BF16), vb[:, hl], (((0,), (0,)), ((), ())),
                                preferred_element_type=F32)
            o = o + lax.dot_general(q_in[:, hl], st.astype(BF16), (((1,), (1,)), ((), ())),
                                    preferred_element_type=F32)
            st_ref[hd] = decay_last[:, hl] * st + lax.dot_general(
                vb[:, hl], k_st[:, hl], (((0,), (0,)), ((), ())), preferred_element_type=F32)
            o = _rms_scale(o, ng_ref[...])
            g_h = g_raw[:, hl]
            o_ref[0, rows, hl] = o * (g_h * jax.nn.sigmoid(g_h))
        return 0

    lax.fori_loop(0, tb // chunk, chunk_body, 0)


def _hgrn2(proj, lb_logits, norm_gain, sel, layer, mix, col0):
    b, s, _ = proj.shape
    tb = min(HG_TB, s)
    chunk, sub = HG_CHUNK, HG_SUB
    n_head = mix // HG_HEAD_DIM
    depth = lb_logits.shape[0]
    cb = col0 // mix
    col = lambda off: pl.BlockSpec((1, tb, mix), lambda i, j: (i, j, cb + off))
    return pl.pallas_call(
        functools.partial(_hgrn2_kernel, layer=layer, tb=tb, chunk=chunk, sub=sub),
        grid=(b, s // tb),
        in_specs=[
            col(0), col(1), col(2), col(3),
            pl.BlockSpec((depth, mix), lambda i, j: (0, 0)),
            pl.BlockSpec((1, HG_HEAD_DIM), lambda i, j: (0, 0)),
            pl.BlockSpec((sub * HG_HEAD_DIM, chunk), lambda i, j: (0, 0)),
        ],
        out_specs=pl.BlockSpec((1, tb, mix), lambda i, j: (i, j, 0)),
        out_shape=jax.ShapeDtypeStruct((b, s, mix), F32),
        scratch_shapes=[
            pltpu.VMEM((n_head, HG_HEAD_DIM, HG_HEAD_DIM), F32),
            pltpu.VMEM((n_head * chunk, sub * HG_HEAD_DIM), BF16),
        ],
        compiler_params=_params("parallel", "arbitrary"),
        name="hgrn2",
    )(proj, proj, proj, proj, lb_logits, norm_gain, sel)


def _merge_kernel(ya_ref, yb_ref, yc_ref, ga_ref, gb_ref, gc_ref, x_ref,
                  wbr_ref, wout_ref, gain_ref, o_ref):
    merged = None
    for idx, (y_ref, gt_ref) in enumerate(((ya_ref, ga_ref), (yb_ref, gb_ref), (yc_ref, gc_ref))):
        part = jnp.dot(y_ref[...].astype(BF16), wbr_ref[idx], preferred_element_type=F32)
        part = jax.nn.sigmoid(gt_ref[...]) * part
        merged = part if merged is None else merged + part
    out = jnp.dot(merged.astype(BF16), wout_ref[...], preferred_element_type=F32)
    o_ref[...] = x_ref[...] + _rms_scale(out, gain_ref[...])


def _merge(ya, yb, yc, proj2, x2, wbr_bf, wout_bf, gain, gate_col0):
    t, d = x2.shape
    mix = ya.shape[1]
    tm = min(MERGE_TM, t)
    gcb = gate_col0 // d
    ybs = lambda: pl.BlockSpec((tm, mix), lambda i: (i, 0))
    gate = lambda off: pl.BlockSpec((tm, d), lambda i: (i, gcb + off))
    return pl.pallas_call(
        _merge_kernel,
        grid=(t // tm,),
        in_specs=[
            ybs(), ybs(), ybs(), gate(0), gate(1), gate(2),
            pl.BlockSpec((tm, d), lambda i: (i, 0)),
            pl.BlockSpec(wbr_bf.shape, lambda i: (0, 0, 0)),
            pl.BlockSpec((d, d), lambda i: (0, 0)),
            pl.BlockSpec((1, d), lambda i: (0, 0)),
        ],
        out_specs=pl.BlockSpec((tm, d), lambda i: (i, 0)),
        out_shape=jax.ShapeDtypeStruct((t, d), F32),
        compiler_params=_params("parallel"),
        name="merge_out",
    )(ya, yb, yc, proj2, proj2, proj2, x2, wbr_bf, wout_bf, gain)


def _mlp_kernel(x_ref, g_in_ref, wup_ref, wdn_ref, g_out_ref, o_ref, h_ref, acc_ref):
    f = pl.program_id(1)

    @pl.when(f == 0)
    def _():
        h_ref[...] = _rms_scale(x_ref[...], g_in_ref[...]).astype(BF16)
        acc_ref[...] = jnp.zeros(acc_ref.shape, F32)

    u = jnp.maximum(jnp.dot(h_ref[...], wup_ref[...], preferred_element_type=F32), 0.0)
    acc_ref[...] += jnp.dot((u * u).astype(BF16), wdn_ref[...], preferred_element_type=F32)

    @pl.when(f == pl.num_programs(1) - 1)
    def _():
        o_ref[...] = x_ref[...] + _rms_scale(acc_ref[...], g_out_ref[...])


def _mlp(x2, g_in, wup_bf, wdn_bf, g_out):
    t, d = x2.shape
    dff = wup_bf.shape[1]
    tm, tf = min(MLP_TM, t), min(MLP_TF, dff)
    return pl.pallas_call(
        _mlp_kernel,
        grid=(t // tm, dff // tf),
        in_specs=[
            pl.BlockSpec((tm, d), lambda i, f: (i, 0)),
            pl.BlockSpec((1, d), lambda i, f: (0, 0)),
            pl.BlockSpec((d, tf), lambda i, f: (0, f)),
            pl.BlockSpec((tf, d), lambda i, f: (f, 0)),
            pl.BlockSpec((1, d), lambda i, f: (0, 0)),
        ],
        out_specs=pl.BlockSpec((tm, d), lambda i, f: (i, 0)),
        out_shape=jax.ShapeDtypeStruct((t, d), F32),
        scratch_shapes=[pltpu.VMEM((tm, d), BF16), pltpu.VMEM((tm, d), F32)],
        compiler_params=_params("parallel", "arbitrary"),
        name="mlp",
    )(x2, g_in, wup_bf, wdn_bf, g_out)


def _block_diag(w):
    n, c, _ = w.shape
    eye = jnp.eye(n, dtype=w.dtype)
    return (eye[:, None, :, None] * w[:, :, None, :]).reshape(n * c, n * c)


def kernel(x, ln_gains, w_in, conv_w, conv_b, rg_w_a, rg_b_a, rg_w_x, rg_b_x, rg_lambda,
           lb_logits, hgrn_norm, w_branch, w_out, w_up, w_down):
    b, s, d = x.shape
    depth = w_in.shape[0]
    mix = conv_w.shape[-1]
    t = b * s
    qkv_lo, qkv_hi = mix, 4 * mix
    gate_lo = 8 * mix
    gate_col0, hg_col0, rg_col0 = 0, 3 * d, 3 * d + 4 * mix

    blk = min(SB_BLK, s)
    tri = (lax.broadcasted_iota(jnp.int32, (blk, blk), 0)
           > lax.broadcasted_iota(jnp.int32, (blk, blk), 1)).astype(BF16)
    sel_r = lax.broadcasted_iota(jnp.int32, (HG_SUB * HG_HEAD_DIM, HG_CHUNK), 0) // HG_HEAD_DIM
    sel_c = lax.broadcasted_iota(jnp.int32, (HG_SUB * HG_HEAD_DIM, HG_CHUNK), 1) % HG_SUB
    sel = (sel_r == sel_c).astype(BF16)

    x2 = x.reshape(t, d)
    for l in range(depth):
        g = ln_gains[l]
        w_l = w_in[l]
        w_qkv = w_l[:, qkv_lo:qkv_hi].astype(BF16)
        w_rest = jnp.concatenate([w_l[:, gate_lo:], w_l[:, qkv_hi:gate_lo], w_l[:, :qkv_lo]],
                                 axis=1).astype(BF16)
        qkv = _in_proj(x2, g[0:1], w_qkv, mix, BF16, SB_HEAD_DIM ** -0.5).reshape(b, s, -1)
        proj2 = _in_proj(x2, g[0:1], w_rest, mix, F32)
        proj = proj2.reshape(b, s, -1)
        y_a = _rglru(proj, conv_w[l], conv_b[l:l + 1],
                     _block_diag(rg_w_a[l]).astype(BF16), rg_b_a[l:l + 1],
                     _block_diag(rg_w_x[l]).astype(BF16), rg_b_x[l:l + 1],
                     rg_lambda[l:l + 1], mix, rg_col0)
        y_b = _stick_breaking(qkv, tri, mix)
        y_c = _hgrn2(proj, lb_logits, hgrn_norm[l:l + 1], sel, l, mix, hg_col0)
        x2 = _merge(y_a.reshape(t, mix), y_b.reshape(t, mix), y_c.reshape(t, mix), proj2, x2,
                    w_branch[l].astype(BF16), w_out[l].astype(BF16), g[1:2], gate_col0)
        x2 = _mlp(x2, g[2:3], w_up[l].astype(BF16), w_down[l].astype(BF16), g[3:4])
    return x2.reshape(b, s, d)
```

```python
import functools

import jax
import jax.numpy as jnp
from jax import lax
from jax.experimental import pallas as pl
from jax.experimental.pallas import tpu as pltpu

F32 = jnp.float32
BF16 = jnp.bfloat16

EPS = 1e-6
RG_C = 8.0
CONV_WIDTH = 4
SB_HEAD_DIM = 64
HG_HEAD_DIM = 128
LB_FLOOR = 1e-30
LOG2_E = 1.4426950408889634
SB_ZERO_EXP = 105.0

LANES = 128
SUBLANES = 8
VMEM_LIMIT = 56 * 1024 * 1024

IN_TM = 512
RG_TB = 512
SB_BLK = 256
SB_HEADS_PER_STEP = 4
HG_TB = 512
HG_CHUNK = 64
HG_SUB = 16
MERGE_TM = 256
MLP_TM, MLP_TF = 512, 1024


def _params(*sem):
    return pltpu.CompilerParams(dimension_semantics=sem, vmem_limit_bytes=VMEM_LIMIT)


def _softplus(x):
    return jnp.maximum(x, 0.0) + jnp.log1p(jnp.exp(-jnp.abs(x)))


def _log1p_exp_neg_abs(x):
    return jnp.log(1.0 + jnp.exp(-jnp.abs(x)))


def _expm1_nonpos(y):
    u = jnp.exp(y)
    near = (y > -0.5) & (u != 1.0)
    us = jnp.where(near, u, 2.0)
    return jnp.where(near, (us - 1.0) * y / jnp.log(us), jnp.where(y > -0.5, y, u - 1.0))


def _rms_scale(x, gain):
    inv = lax.rsqrt(jnp.mean(x * x, axis=-1, keepdims=True) + EPS)
    return x * inv * gain


def _in_proj_kernel(x_ref, g_ref, w_ref, qkv_ref, h_ref, *, q_width, q_scale):
    h = _rms_scale(x_ref[...], g_ref[...]).astype(BF16)
    h_ref[...] = h
    out = jnp.dot(h, w_ref[...], preferred_element_type=F32)
    qkv_ref[:, :q_width] = (out[:, :q_width] * q_scale).astype(BF16)
    qkv_ref[:, q_width:] = out[:, q_width:].astype(BF16)


def _in_proj(x2, gain, w_qkv_bf, q_width, q_scale):
    t, d = x2.shape
    n = w_qkv_bf.shape[1]
    tm = min(IN_TM, t)
    return pl.pallas_call(
        functools.partial(_in_proj_kernel, q_width=q_width, q_scale=q_scale),
        grid=(t // tm,),
        in_specs=[
            pl.BlockSpec((tm, d), lambda i: (i, 0)),
            pl.BlockSpec((1, d), lambda i: (0, 0)),
            pl.BlockSpec((d, n), lambda i: (0, 0)),
        ],
        out_specs=[pl.BlockSpec((tm, n), lambda i: (i, 0)),
                   pl.BlockSpec((tm, d), lambda i: (i, 0))],
        out_shape=[jax.ShapeDtypeStruct((t, n), BF16), jax.ShapeDtypeStruct((t, d), BF16)],
        compiler_params=_params("parallel"),
        name="in_proj",
    )(x2, gain, w_qkv_bf)


def _rglru_kernel(hn_ref, wp_ref, cw_ref, cb_ref, wa_ref, ba_ref, wx_ref, bx_ref, lam_ref,
                  o_ref, xpad_ref, a_ref, u_ref, h_ref, *, tb):
    j = pl.program_id(1)
    w = wp_ref.shape[-1]

    @pl.when(j == 0)
    def _():
        xpad_ref[0:SUBLANES, :] = jnp.zeros((SUBLANES, w), F32)
        h_ref[...] = jnp.zeros((1, w), F32)

    x = jnp.dot(hn_ref[0], wp_ref[...], preferred_element_type=F32)
    xpad_ref[SUBLANES:SUBLANES + tb, :] = x
    xc = cb_ref[...]
    for k in range(CONV_WIDTH):
        off = SUBLANES - (CONV_WIDTH - 1) + k
        xc = xc + cw_ref[k:k + 1, :] * xpad_ref[off:off + tb, :]
    xpad_ref[0:SUBLANES, :] = x[tb - SUBLANES:tb, :]

    xcb = xc.astype(BF16)
    r = jax.nn.sigmoid(jnp.dot(xcb, wa_ref[...], preferred_element_type=F32) + ba_ref[...])
    gate_i = jax.nn.sigmoid(jnp.dot(xcb, wx_ref[...], preferred_element_type=F32) + bx_ref[...])
    log_a = -RG_C * r * _softplus(-lam_ref[...])
    a = jnp.exp(log_a)
    mult = jnp.sqrt(jnp.maximum(-_expm1_nonpos(2.0 * log_a), 0.0))
    row = lax.broadcasted_iota(jnp.int32, (tb, w), 0) + j * tb
    mult = jnp.where(row == 0, 1.0, mult)
    a_ref[...] = a
    u_ref[...] = mult * (gate_i * xc)

    def step(t, h):
        h = a_ref[pl.ds(t, 1), :] * h + u_ref[pl.ds(t, 1), :]
        o_ref[0, pl.ds(t, 1), :] = h
        return h

    h_ref[...] = lax.fori_loop(0, tb, step, h_ref[...], unroll=SUBLANES)


def _rglru(hn, w_proj, cw, cb, wa_bd, ba, wx_bd, bx, lam):
    b, s, d = hn.shape
    mix = w_proj.shape[1]
    tb = min(RG_TB, s)
    vec = lambda: pl.BlockSpec((1, mix), lambda i, j: (0, 0))
    return pl.pallas_call(
        functools.partial(_rglru_kernel, tb=tb),
        grid=(b, s // tb),
        in_specs=[
            pl.BlockSpec((1, tb, d), lambda i, j: (i, j, 0)),
            pl.BlockSpec((d, mix), lambda i, j: (0, 0)),
            pl.BlockSpec((CONV_WIDTH, mix), lambda i, j: (0, 0)),
            vec(),
            pl.BlockSpec((mix, mix), lambda i, j: (0, 0)),
            vec(),
            pl.BlockSpec((mix, mix), lambda i, j: (0, 0)),
            vec(),
            vec(),
        ],
        out_specs=pl.BlockSpec((1, tb, mix), lambda i, j: (i, j, 0)),
        out_shape=jax.ShapeDtypeStruct((b, s, mix), F32),
        scratch_shapes=[
            pltpu.VMEM((tb + SUBLANES, mix), F32),
            pltpu.VMEM((tb, mix), F32),
            pltpu.VMEM((tb, mix), F32),
            pltpu.VMEM((1, mix), F32),
        ],
        compiler_params=_params("parallel", "arbitrary"),
        name="rglru",
    )(hn, w_proj, cw, cb, wa_bd, ba, wx_bd, bx, lam)


def _sb_kernel(q_ref, k_ref, v_ref, tri_ref, o_ref, acc_ref, car_ref, *, blk, dh, n_h):
    qi = pl.program_id(2)
    row = lax.broadcasted_iota(jnp.int32, (blk, blk), 0)
    col = lax.broadcasted_iota(jnp.int32, (blk, blk), 1)
    causal = col < row
    tri = tri_ref[...]
    heads = [slice(h * dh, (h + 1) * dh) for h in range(n_h)]
    qs = [q_ref[0, :, lanes] for lanes in heads]

    acc_ref[...] = jnp.zeros(acc_ref.shape, F32)
    car_ref[...] = jnp.zeros(car_ref.shape, F32)

    def block(kj, diag):
        rows = pl.ds(pl.multiple_of(kj * blk, blk), blk)
        carry_min = None
        for h, lanes in enumerate(heads):
            kb = k_ref[0, rows, lanes]
            vb = v_ref[0, rows, lanes]
            z = lax.dot_general(qs[h], kb, (((1,), (1,)), ((), ())), preferred_element_type=F32)
            sp = jnp.maximum(z, 0.0) + _log1p_exp_neg_abs(z)
            log_beta = z - sp
            if diag:
                sp = jnp.where(causal, sp, 0.0)
            inner = jnp.dot(sp.astype(BF16), tri, preferred_element_type=F32)
            carry = car_ref[h]
            wgt = jnp.exp(log_beta - inner - carry)
            if diag:
                wgt = jnp.where(causal, wgt, 0.0)
            acc_ref[h] += jnp.dot(wgt.astype(BF16), vb, preferred_element_type=F32)
            carry = carry + jnp.sum(sp, axis=1, keepdims=True)
            car_ref[h] = carry
            c_min = jnp.min(carry)
            carry_min = c_min if carry_min is None else jnp.minimum(carry_min, c_min)
        return carry_min

    def cond(c):
        return jnp.logical_and(c[0] >= 0, c[1] < SB_ZERO_EXP)

    def body(c):
        return c[0] - 1, block(c[0], False)

    lax.while_loop(cond, body, (qi - 1, block(qi, True)))
    for h, lanes in enumerate(heads):
        o_ref[0, :, lanes] = acc_ref[h]


def _stick_breaking(qkv, tri, mix):
    b, s, _ = qkv.shape
    blk = min(SB_BLK, s)
    dh = SB_HEAD_DIM
    width = SB_HEADS_PER_STEP * dh
    n_grp = mix // width
    return pl.pallas_call(
        functools.partial(_sb_kernel, blk=blk, dh=dh, n_h=SB_HEADS_PER_STEP),
        grid=(b, n_grp, s // blk),
        in_specs=[
            pl.BlockSpec((1, blk, width), lambda i, g, t: (i, t, g)),
            pl.BlockSpec((1, s, width), lambda i, g, t: (i, 0, n_grp + g)),
            pl.BlockSpec((1, s, width), lambda i, g, t: (i, 0, 2 * n_grp + g)),
            pl.BlockSpec((blk, blk), lambda i, g, t: (0, 0)),
        ],
        out_specs=pl.BlockSpec((1, blk, width), lambda i, g, t: (i, t, g)),
        out_shape=jax.ShapeDtypeStruct((b, s, mix), F32),
        scratch_shapes=[
            pltpu.VMEM((SB_HEADS_PER_STEP, blk, dh), F32),
            pltpu.VMEM((SB_HEADS_PER_STEP, blk, 1), F32),
        ],
        compiler_params=_params("parallel", "parallel", "arbitrary"),
        name="stick_breaking",
    )(qkv, qkv, qkv, tri)


def _hgrn2_kernel(hn_ref, wp_ref, lbl_ref, ng_ref, sel_ref, o_ref,
                  st_ref, w_ref, proj_ref, qkb_ref, diag_ref, *, layer, tb, chunk, sub):
    n_b = hn_ref.shape[0]
    mix = o_ref.shape[-1]
    hk = HG_HEAD_DIM
    n_head = mix // hk
    n_sub = chunk // sub

    @pl.when(pl.program_id(0) == 0)
    def _():
        st_ref[...] = jnp.zeros(st_ref.shape, F32)

    for bi in range(n_b):
        proj_ref[bi] = jnp.dot(hn_ref[bi], wp_ref[...], preferred_element_type=F32)

    lbl = lbl_ref[...]
    p = jnp.exp(lbl - jnp.max(lbl, axis=0, keepdims=True))
    p = p / jnp.sum(p, axis=0, keepdims=True)
    lb = jnp.zeros((1, mix), F32)
    for m in range(1, layer + 1):
        lb = lb + p[m:m + 1, :]
    log_lb = jnp.log(jnp.maximum(lb, LB_FLOOR))
    log_1m_lb = jnp.log1p(-lb)

    row = lax.broadcasted_iota(jnp.int32, (chunk, chunk), 0)
    col = lax.broadcasted_iota(jnp.int32, (chunk, chunk), 1)
    same_sub = (row // sub) == (col // sub)
    later_sub = (col // sub) > (row // sub)
    causal = row <= col
    trow = lax.broadcasted_iota(jnp.int32, (chunk, mix), 0)

    def stage1(c, slot):
        rows = pl.ds(pl.multiple_of(c * chunk, chunk), chunk)
        for bi in range(n_b):
            q_raw = proj_ref[bi, rows, 0:mix]
            f_pre = proj_ref[bi, rows, mix:2 * mix]
            q = q_raw * jax.nn.sigmoid(q_raw)
            t2 = log_1m_lb - (jnp.maximum(-f_pre, 0.0) + _log1p_exp_neg_abs(f_pre))
            log_f = jnp.maximum(log_lb, t2) + _log1p_exp_neg_abs(log_lb - t2)
            k = (1.0 - lb) * jax.nn.sigmoid(-f_pre)
            bcum = log_f
            d = 1
            while d < chunk:
                bcum = bcum + jnp.where(trow >= d, pltpu.roll(bcum, d, 0), 0.0)
                d *= 2
            b3 = (bcum * LOG2_E).reshape(n_sub, sub, mix)
            q3 = q.reshape(n_sub, sub, mix)
            k3 = k.reshape(n_sub, sub, mix)
            for tl in range(sub):
                dec = jnp.exp2(jnp.minimum(b3[:, tl:tl + 1, :] - b3, 0.0))
                tile = (dec * (q3[:, tl:tl + 1, :] * k3)).reshape(chunk, mix).astype(BF16)
                for hd in range(n_head):
                    r0 = (bi * n_head + hd) * chunk
                    w_ref[r0:r0 + chunk, tl * hk:(tl + 1) * hk] = tile[:, hd * hk:(hd + 1) * hk]
            qkb_ref[slot, bi, 0] = q
            qkb_ref[slot, bi, 1] = k
            qkb_ref[slot, bi, 2] = bcum
        diag_ref[slot] = jnp.dot(w_ref[...], sel_ref[...], preferred_element_type=F32)

    def stage2(c, slot):
        rows = pl.ds(pl.multiple_of(c * chunk, chunk), chunk)
        for bi in range(n_b):
            q = qkb_ref[slot, bi, 0]
            k = qkb_ref[slot, bi, 1]
            bcum = qkb_ref[slot, bi, 2]
            b_last = bcum[chunk - 1:chunk, :]
            vb = proj_ref[bi, rows, 2 * mix:3 * mix].astype(BF16)
            q_in = (q * jnp.exp(bcum)).astype(BF16)
            k_st = (k * jnp.exp(b_last - bcum)).astype(BF16)
            decay_last = jnp.exp(b_last)
            g_raw = proj_ref[bi, rows, 3 * mix:4 * mix]
            for hd in range(n_head):
                hl = slice(hd * hk, (hd + 1) * hk)
                r0 = (bi * n_head + hd) * chunk
                at = jnp.where(same_sub & causal, diag_ref[slot, r0:r0 + chunk, :], 0.0)
                pieces = []
                for j in range(n_sub - 1):
                    r_j = bcum[(j + 1) * sub - 1:(j + 1) * sub, hl]
                    srows = slice(j * sub, (j + 1) * sub)
                    k_hat = (k[srows, hl] * jnp.exp(r_j - bcum[srows, hl])).astype(BF16)
                    q_j = (q[:, hl] * jnp.exp(jnp.minimum(bcum[:, hl] - r_j, 0.0))).astype(BF16)
                    pieces.append(lax.dot_general(k_hat, q_j, (((1,), (1,)), ((), ())),
                                                  preferred_element_type=F32))
                if pieces:
                    pieces.append(jnp.zeros((sub, chunk), F32))
                    at = jnp.where(later_sub, jnp.concatenate(pieces, axis=0), at)
                st = st_ref[bi * n_head + hd]
                o = lax.dot_general(at.astype(BF16), vb[:, hl], (((0,), (0,)), ((), ())),
                                    preferred_element_type=F32)
                o = o + lax.dot_general(q_in[:, hl], st.astype(BF16), (((1,), (1,)), ((), ())),
                                        preferred_element_type=F32)
                st_ref[bi * n_head + hd] = decay_last[:, hl] * st + lax.dot_general(
                    vb[:, hl], k_st[:, hl], (((0,), (0,)), ((), ())), preferred_element_type=F32)
                o = _rms_scale(o, ng_ref[...])
                g_h = g_raw[:, hl]
                o_ref[bi, rows, hl] = o * (g_h * jax.nn.sigmoid(g_h))

    n_chunk = tb // chunk
    stage1(0, 0)

    def body(c, _):
        stage2(c - 1, lax.rem(c - 1, 2))
        stage1(c, lax.rem(c, 2))
        return 0

    lax.fori_loop(1, n_chunk, body, 0)
    stage2(n_chunk - 1, (n_chunk - 1) % 2)


def _hgrn2(hn, w_proj, lb_logits, norm_gain, sel, layer):
    b, s, d = hn.shape
    mix = w_proj.shape[1] // 4
    tb = min(HG_TB, s)
    chunk, sub = HG_CHUNK, HG_SUB
    n_head = mix // HG_HEAD_DIM
    depth = lb_logits.shape[0]
    return pl.pallas_call(
        functools.partial(_hgrn2_kernel, layer=layer, tb=tb, chunk=chunk, sub=sub),
        grid=(s // tb,),
        in_specs=[
            pl.BlockSpec((b, tb, d), lambda j: (0, j, 0)),
            pl.BlockSpec((d, 4 * mix), lambda j: (0, 0)),
            pl.BlockSpec((depth, mix), lambda j: (0, 0)),
            pl.BlockSpec((1, HG_HEAD_DIM), lambda j: (0, 0)),
            pl.BlockSpec((sub * HG_HEAD_DIM, chunk), lambda j: (0, 0)),
        ],
        out_specs=pl.BlockSpec((b, tb, mix), lambda j: (0, j, 0)),
        out_shape=jax.ShapeDtypeStruct((b, s, mix), F32),
        scratch_shapes=[
            pltpu.VMEM((b * n_head, HG_HEAD_DIM, HG_HEAD_DIM), F32),
            pltpu.VMEM((b * n_head * chunk, sub * HG_HEAD_DIM), BF16),
            pltpu.VMEM((b, tb, 4 * mix), F32),
            pltpu.VMEM((2, b, 3, chunk, mix), F32),
            pltpu.VMEM((2, b * n_head * chunk, chunk), F32),
        ],
        compiler_params=_params("arbitrary"),
        name="hgrn2",
    )(hn, w_proj, lb_logits, norm_gain, sel)


def _merge_kernel(ya_ref, yb_ref, yc_ref, hn_ref, x_ref, wg_ref, wbr_ref, wout_ref, gain_ref, o_ref):
    d = x_ref.shape[-1]
    hn = hn_ref[...]
    merged = None
    for idx, y_ref in enumerate((ya_ref, yb_ref, yc_ref)):
        gate = jnp.dot(hn, wg_ref[:, idx * d:(idx + 1) * d], preferred_element_type=F32)
        part = jnp.dot(y_ref[...].astype(BF16), wbr_ref[idx], preferred_element_type=F32)
        part = jax.nn.sigmoid(gate) * part
        merged = part if merged is None else merged + part
    out = jnp.dot(merged.astype(BF16), wout_ref[...], preferred_element_type=F32)
    o_ref[...] = x_ref[...] + _rms_scale(out, gain_ref[...])


def _merge(ya, yb, yc, hn2, x2, wg_bf, wbr_bf, wout_bf, gain):
    t, d = x2.shape
    mix = ya.shape[1]
    tm = min(MERGE_TM, t)
    ybs = lambda: pl.BlockSpec((tm, mix), lambda i: (i, 0))
    row = lambda: pl.BlockSpec((tm, d), lambda i: (i, 0))
    return pl.pallas_call(
        _merge_kernel,
        grid=(t // tm,),
        in_specs=[
            ybs(), ybs(), ybs(), row(), row(),
            pl.BlockSpec(wg_bf.shape, lambda i: (0, 0)),
            pl.BlockSpec(wbr_bf.shape, lambda i: (0, 0, 0)),
            pl.BlockSpec((d, d), lambda i: (0, 0)),
            pl.BlockSpec((1, d), lambda i: (0, 0)),
        ],
        out_specs=row(),
        out_shape=jax.ShapeDtypeStruct((t, d), F32),
        compiler_params=_params("parallel"),
        name="merge_out",
    )(ya, yb, yc, hn2, x2, wg_bf, wbr_bf, wout_bf, gain)


def _mlp_kernel(x_ref, g_in_ref, wup_ref, wdn_ref, g_out_ref, o_ref, h_ref, acc_ref):
    f = pl.program_id(1)

    @pl.when(f == 0)
    def _():
        h_ref[...] = _rms_scale(x_ref[...], g_in_ref[...]).astype(BF16)
        acc_ref[...] = jnp.zeros(acc_ref.shape, F32)

    u = jnp.maximum(jnp.dot(h_ref[...], wup_ref[...], preferred_element_type=F32), 0.0)
    acc_ref[...] += jnp.dot((u * u).astype(BF16), wdn_ref[...], preferred_element_type=F32)

    @pl.when(f == pl.num_programs(1) - 1)
    def _():
        o_ref[...] = x_ref[...] + _rms_scale(acc_ref[...], g_out_ref[...])


def _mlp(x2, g_in, wup_bf, wdn_bf, g_out):
    t, d = x2.shape
    dff = wup_bf.shape[1]
    tm, tf = min(MLP_TM, t), min(MLP_TF, dff)
    return pl.pallas_call(
        _mlp_kernel,
        grid=(t // tm, dff // tf),
        in_specs=[
            pl.BlockSpec((tm, d), lambda i, f: (i, 0)),
            pl.BlockSpec((1, d), lambda i, f: (0, 0)),
            pl.BlockSpec((d, tf), lambda i, f: (0, f)),
            pl.BlockSpec((tf, d), lambda i, f: (f, 0)),
            pl.BlockSpec((1, d), lambda i, f: (0, 0)),
        ],
        out_specs=pl.BlockSpec((tm, d), lambda i, f: (i, 0)),
        out_shape=jax.ShapeDtypeStruct((t, d), F32),
        scratch_shapes=[pltpu.VMEM((tm, d), BF16), pltpu.VMEM((tm, d), F32)],
        compiler_params=_params("parallel", "arbitrary"),
        name="mlp",
    )(x2, g_in, wup_bf, wdn_bf, g_out)


def _block_diag(w):
    n, c, _ = w.shape
    eye = jnp.eye(n, dtype=w.dtype)
    return (eye[:, None, :, None] * w[:, :, None, :]).reshape(n * c, n * c)


def kernel(x, ln_gains, w_in, conv_w, conv_b, rg_w_a, rg_b_a, rg_w_x, rg_b_x, rg_lambda,
           lb_logits, hgrn_norm, w_branch, w_out, w_up, w_down):
    b, s, d = x.shape
    depth = w_in.shape[0]
    mix = conv_w.shape[-1]
    t = b * s
    qkv_lo, hg_lo, gate_lo = mix, 4 * mix, 8 * mix

    blk = min(SB_BLK, s)
    tri = (lax.broadcasted_iota(jnp.int32, (blk, blk), 0)
           > lax.broadcasted_iota(jnp.int32, (blk, blk), 1)).astype(BF16)
    sel_r = lax.broadcasted_iota(jnp.int32, (HG_SUB * HG_HEAD_DIM, HG_CHUNK), 0) // HG_HEAD_DIM
    sel_c = lax.broadcasted_iota(jnp.int32, (HG_SUB * HG_HEAD_DIM, HG_CHUNK), 1) % HG_SUB
    sel = (sel_r == sel_c).astype(BF16)

    x2 = x.reshape(t, d)
    for l in range(depth):
        g = ln_gains[l]
        w_l = w_in[l].astype(BF16)
        qkv, hn2 = _in_proj(x2, g[0:1], w_l[:, qkv_lo:hg_lo], mix, SB_HEAD_DIM ** -0.5)
        hn = hn2.reshape(b, s, d)
        y_a = _rglru(hn, w_l[:, :qkv_lo], conv_w[l], conv_b[l:l + 1],
                     _block_diag(rg_w_a[l]).astype(BF16), rg_b_a[l:l + 1],
                     _block_diag(rg_w_x[l]).astype(BF16), rg_b_x[l:l + 1],
                     rg_lambda[l:l + 1])
        y_b = _stick_breaking(qkv.reshape(b, s, -1), tri, mix)
        y_c = _hgrn2(hn, w_l[:, hg_lo:gate_lo], lb_logits, hgrn_norm[l:l + 1], sel, l)
        x2 = _merge(y_a.reshape(t, mix), y_b.reshape(t, mix), y_c.reshape(t, mix), hn2, x2,
                    w_l[:, gate_lo:], w_branch[l].astype(BF16), w_out[l].astype(BF16), g[1:2])
        x2 = _mlp(x2, g[2:3], w_up[l].astype(BF16), w_down[l].astype(BF16), g[3:4])
    return x2.reshape(b, s, d)
```

```python
import functools

import jax
import jax.numpy as jnp
from jax import lax
from jax.experimental import pallas as pl
from jax.experimental.pallas import tpu as pltpu

F32 = jnp.float32
BF16 = jnp.bfloat16

EPS = 1e-6
RG_C = 8.0
CONV_WIDTH = 4
SB_HEAD_DIM = 64
HG_HEAD_DIM = 128
LB_FLOOR = 1e-30
LOG2_E = 1.4426950408889634
SB_ZERO_EXP = 105.0

LANES = 128
SUBLANES = 8
VMEM_LIMIT = 56 * 1024 * 1024

IN_TM = 512
RG_TB = 512
SB_BLK = 256
SB_HEADS_PER_STEP = 8
HG_TB = 512
HG_CHUNK = 64
HG_SUB = 16
MERGE_TM = 256
MLP_TM, MLP_TF = 1024, 1024


def _params(*sem):
    return pltpu.CompilerParams(dimension_semantics=sem, vmem_limit_bytes=VMEM_LIMIT)


def _softplus(x):
    return jnp.maximum(x, 0.0) + jnp.log1p(jnp.exp(-jnp.abs(x)))


def _log1p_exp_neg_abs(x):
    return jnp.log(1.0 + jnp.exp2(jnp.abs(x) * (-LOG2_E)))


def _expm1_nonpos(y):
    u = jnp.exp(y)
    near = (y > -0.5) & (u != 1.0)
    us = jnp.where(near, u, 2.0)
    return jnp.where(near, (us - 1.0) * y / jnp.log(us), jnp.where(y > -0.5, y, u - 1.0))


def _rms_scale(x, gain):
    inv = lax.rsqrt(jnp.mean(x * x, axis=-1, keepdims=True) + EPS)
    return x * inv * gain


def _in_proj_kernel(x_ref, g_ref, w_ref, qkv_ref, h_ref, *, q_width, q_scale):
    h = _rms_scale(x_ref[...], g_ref[...]).astype(BF16)
    h_ref[...] = h
    out = jnp.dot(h, w_ref[...], preferred_element_type=F32)
    qkv_ref[:, :q_width] = (out[:, :q_width] * q_scale).astype(BF16)
    qkv_ref[:, q_width:] = out[:, q_width:].astype(BF16)


def _in_proj(x2, gain, w_qkv_bf, q_width, q_scale):
    t, d = x2.shape
    n = w_qkv_bf.shape[1]
    tm = min(IN_TM, t)
    return pl.pallas_call(
        functools.partial(_in_proj_kernel, q_width=q_width, q_scale=q_scale),
        grid=(t // tm,),
        in_specs=[
            pl.BlockSpec((tm, d), lambda i: (i, 0)),
            pl.BlockSpec((1, d), lambda i: (0, 0)),
            pl.BlockSpec((d, n), lambda i: (0, 0)),
        ],
        out_specs=[pl.BlockSpec((tm, n), lambda i: (i, 0)),
                   pl.BlockSpec((tm, d), lambda i: (i, 0))],
        out_shape=[jax.ShapeDtypeStruct((t, n), BF16), jax.ShapeDtypeStruct((t, d), BF16)],
        compiler_params=_params("parallel"),
        name="in_proj",
    )(x2, gain, w_qkv_bf)


def _rglru_kernel(hn_ref, wp_ref, cw_ref, cb_ref, wa_ref, ba_ref, wx_ref, bx_ref, lam_ref,
                  o_ref, xpad_ref, a_ref, u_ref, h_ref, *, tb):
    j = pl.program_id(1)
    w = wp_ref.shape[-1]

    @pl.when(j == 0)
    def _():
        xpad_ref[0:SUBLANES, :] = jnp.zeros((SUBLANES, w), F32)
        h_ref[...] = jnp.zeros((1, w), F32)

    x = jnp.dot(hn_ref[0], wp_ref[...], preferred_element_type=F32)
    xpad_ref[SUBLANES:SUBLANES + tb, :] = x
    xc = cb_ref[...]
    for k in range(CONV_WIDTH):
        off = SUBLANES - (CONV_WIDTH - 1) + k
        xc = xc + cw_ref[k:k + 1, :] * xpad_ref[off:off + tb, :]
    xpad_ref[0:SUBLANES, :] = x[tb - SUBLANES:tb, :]

    xcb = xc.astype(BF16)
    r = jax.nn.sigmoid(jnp.dot(xcb, wa_ref[...], preferred_element_type=F32) + ba_ref[...])
    gate_i = jax.nn.sigmoid(jnp.dot(xcb, wx_ref[...], preferred_element_type=F32) + bx_ref[...])
    log_a = -RG_C * r * _softplus(-lam_ref[...])
    a = jnp.exp(log_a)
    mult = jnp.sqrt(jnp.maximum(-_expm1_nonpos(2.0 * log_a), 0.0))
    row = lax.broadcasted_iota(jnp.int32, (tb, w), 0) + j * tb
    mult = jnp.where(row == 0, 1.0, mult)
    a_ref[...] = a
    u_ref[...] = mult * (gate_i * xc)

    def step(t, h):
        h = a_ref[pl.ds(t, 1), :] * h + u_ref[pl.ds(t, 1), :]
        o_ref[0, pl.ds(t, 1), :] = h
        return h

    h_ref[...] = lax.fori_loop(0, tb, step, h_ref[...], unroll=SUBLANES)


def _rglru(hn, w_proj, cw, cb, wa_bd, ba, wx_bd, bx, lam):
    b, s, d = hn.shape
    mix = w_proj.shape[1]
    tb = min(RG_TB, s)
    vec = lambda: pl.BlockSpec((1, mix), lambda i, j: (0, 0))
    return pl.pallas_call(
        functools.partial(_rglru_kernel, tb=tb),
        grid=(b, s // tb),
        in_specs=[
            pl.BlockSpec((1, tb, d), lambda i, j: (i, j, 0)),
            pl.BlockSpec((d, mix), lambda i, j: (0, 0)),
            pl.BlockSpec((CONV_WIDTH, mix), lambda i, j: (0, 0)),
            vec(),
            pl.BlockSpec((mix, mix), lambda i, j: (0, 0)),
            vec(),
            pl.BlockSpec((mix, mix), lambda i, j: (0, 0)),
            vec(),
            vec(),
        ],
        out_specs=pl.BlockSpec((1, tb, mix), lambda i, j: (i, j, 0)),
        out_shape=jax.ShapeDtypeStruct((b, s, mix), F32),
        scratch_shapes=[
            pltpu.VMEM((tb + SUBLANES, mix), F32),
            pltpu.VMEM((tb, mix), F32),
            pltpu.VMEM((tb, mix), F32),
            pltpu.VMEM((1, mix), F32),
        ],
        compiler_params=_params("parallel", "arbitrary"),
        name="rglru",
    )(hn, w_proj, cw, cb, wa_bd, ba, wx_bd, bx, lam)


def _sb_kernel(q_ref, k_ref, v_ref, tri_ref, o_ref, acc_ref, car_ref, *, blk, dh, n_h):
    qi = pl.program_id(2)
    row = lax.broadcasted_iota(jnp.int32, (blk, blk), 0)
    col = lax.broadcasted_iota(jnp.int32, (blk, blk), 1)
    causal = col < row
    tri = tri_ref[...]
    heads = [slice(h * dh, (h + 1) * dh) for h in range(n_h)]
    qs = [q_ref[0, :, lanes] for lanes in heads]

    acc_ref[...] = jnp.zeros(acc_ref.shape, F32)
    car_ref[...] = jnp.zeros(car_ref.shape, F32)

    def block(kj, diag, live=None):
        rows = pl.ds(pl.multiple_of(kj * blk, blk), blk)
        carry_min = None
        sps, log_betas = [], []
        for h, lanes in enumerate(heads):
            kb = k_ref[0, rows, lanes]
            z = lax.dot_general(qs[h], kb, (((1,), (1,)), ((), ())), preferred_element_type=F32)
            sp = jnp.maximum(z, 0.0) + _log1p_exp_neg_abs(z)
            log_betas.append(z - sp)
            if diag:
                sp = jnp.where(causal, sp, 0.0)
            sps.append(sp)
        inner_all = jnp.dot(jnp.concatenate([sp.astype(BF16) for sp in sps], axis=0), tri,
                            preferred_element_type=F32)
        for h, lanes in enumerate(heads):
            vb = v_ref[0, rows, lanes]
            sp, log_beta = sps[h], log_betas[h]
            inner = inner_all[h * blk:(h + 1) * blk, :]
            carry = car_ref[h]
            wgt = jnp.exp(log_beta - inner - carry)
            if diag:
                wgt = jnp.where(causal, wgt, 0.0)
            pv = jnp.dot(wgt.astype(BF16), vb, preferred_element_type=F32)
            if live is not None:
                pv = jnp.where(live, pv, 0.0)
            acc_ref[h] += pv
            carry = carry + jnp.sum(sp, axis=1, keepdims=True)
            car_ref[h] = carry
            c_min = jnp.min(carry)
            carry_min = c_min if carry_min is None else jnp.minimum(carry_min, c_min)
        return carry_min

    def cond(c):
        return jnp.logical_and(c[0] >= 0, c[1] < SB_ZERO_EXP)

    def body(c):
        return c[0] - 1, block(c[0], False)

    block(qi, True)
    first_min = block(jnp.maximum(qi - 1, 0), False, live=qi >= 1)
    lax.while_loop(cond, body, (qi - 2, first_min))
    for h, lanes in enumerate(heads):
        o_ref[0, :, lanes] = acc_ref[h]


def _stick_breaking(qkv, tri, mix):
    b, s, _ = qkv.shape
    blk = min(SB_BLK, s)
    dh = SB_HEAD_DIM
    width = SB_HEADS_PER_STEP * dh
    n_grp = mix // width
    return pl.pallas_call(
        functools.partial(_sb_kernel, blk=blk, dh=dh, n_h=SB_HEADS_PER_STEP),
        grid=(b, n_grp, s // blk),
        in_specs=[
            pl.BlockSpec((1, blk, width), lambda i, g, t: (i, t, g)),
            pl.BlockSpec((1, s, width), lambda i, g, t: (i, 0, n_grp + g)),
            pl.BlockSpec((1, s, width), lambda i, g, t: (i, 0, 2 * n_grp + g)),
            pl.BlockSpec((blk, blk), lambda i, g, t: (0, 0)),
        ],
        out_specs=pl.BlockSpec((1, blk, width), lambda i, g, t: (i, t, g)),
        out_shape=jax.ShapeDtypeStruct((b, s, mix), F32),
        scratch_shapes=[
            pltpu.VMEM((SB_HEADS_PER_STEP, blk, dh), F32),
            pltpu.VMEM((SB_HEADS_PER_STEP, blk, 1), F32),
        ],
        compiler_params=_params("parallel", "parallel", "arbitrary"),
        name="stick_breaking",
    )(qkv, qkv, qkv, tri)


def _hgrn2_kernel(hn_ref, wp_ref, lbl_ref, ng_ref, sel_ref, o_ref,
                  st_ref, proj_ref, qkb_ref, diag_ref, *, layer, tb, chunk, sub):
    n_b = hn_ref.shape[0]
    mix = o_ref.shape[-1]
    hk = HG_HEAD_DIM
    n_head = mix // hk
    n_sub = chunk // sub
    half = sub // 2

    @pl.when(pl.program_id(0) == 0)
    def _():
        st_ref[...] = jnp.zeros(st_ref.shape, F32)

    for bi in range(n_b):
        proj_ref[bi] = jnp.dot(hn_ref[bi], wp_ref[...], preferred_element_type=F32)

    lbl = lbl_ref[...]
    p = jnp.exp(lbl - jnp.max(lbl, axis=0, keepdims=True))
    p = p / jnp.sum(p, axis=0, keepdims=True)
    lb = jnp.zeros((1, mix), F32)
    for m in range(1, layer + 1):
        lb = lb + p[m:m + 1, :]
    log_lb = jnp.log(jnp.maximum(lb, LB_FLOOR))
    log_1m_lb = jnp.log1p(-lb)

    row = lax.broadcasted_iota(jnp.int32, (chunk, chunk), 0)
    col = lax.broadcasted_iota(jnp.int32, (chunk, chunk), 1)
    same_sub = (row // sub) == (col // sub)
    later_sub = (col // sub) > (row // sub)
    causal = row <= col
    trow = lax.broadcasted_iota(jnp.int32, (chunk, mix), 0)

    def stage1(c, slot):
        rows = pl.ds(pl.multiple_of(c * chunk, chunk), chunk)
        for bi in range(n_b):
            q_raw = proj_ref[bi, rows, 0:mix]
            f_pre = proj_ref[bi, rows, mix:2 * mix]
            q = q_raw * jax.nn.sigmoid(q_raw)
            t2 = log_1m_lb - (jnp.maximum(-f_pre, 0.0) + _log1p_exp_neg_abs(f_pre))
            log_f = jnp.maximum(log_lb, t2) + _log1p_exp_neg_abs(log_lb - t2)
            k = (1.0 - lb) * jax.nn.sigmoid(-f_pre)
            bcum = log_f
            d = 1
            while d < chunk:
                bcum = bcum + jnp.where(trow >= d, pltpu.roll(bcum, d, 0), 0.0)
                d *= 2
            b3 = (bcum * LOG2_E).reshape(n_sub, sub, mix)
            q3 = q.reshape(n_sub, sub, mix)
            k3 = k.reshape(n_sub, sub, mix)
            qkb_ref[slot, bi, 0] = q
            qkb_ref[slot, bi, 1] = k
            qkb_ref[slot, bi, 2] = bcum
            tiles = ([], [])
            for tl in range(sub):
                b_t = b3[:, tl:tl + 1, :]
                q_t = q3[:, tl:tl + 1, :]
                for hf in range(1 if tl < half else 2):
                    srows = slice(hf * half, (hf + 1) * half)
                    dec = jnp.exp2(jnp.minimum(b_t - b3[:, srows, :], 0.0))
                    tile = (dec * (q_t * k3[:, srows, :])).reshape(n_sub * half, mix).astype(BF16)
                    tiles[hf].append(tile)
            for hf in range(2):
                lhs = jnp.concatenate(
                    [jnp.concatenate([t[:, hd * hk:(hd + 1) * hk] for t in tiles[hf]], axis=1)
                     for hd in range(n_head)], axis=0)
                diag_ref[slot, bi, hf] = jnp.dot(lhs, sel_ref[hf * half * hk:, :],
                                                 preferred_element_type=F32)

    def stage2(c, slot):
        rows = pl.ds(pl.multiple_of(c * chunk, chunk), chunk)
        for bi in range(n_b):
            q = qkb_ref[slot, bi, 0]
            k = qkb_ref[slot, bi, 1]
            bcum = qkb_ref[slot, bi, 2]
            b_last = bcum[chunk - 1:chunk, :]
            vb = proj_ref[bi, rows, 2 * mix:3 * mix].astype(BF16)
            q_in = (q * jnp.exp(bcum)).astype(BF16)
            k_st = (k * jnp.exp(b_last - bcum)).astype(BF16)
            decay_last = jnp.exp(b_last)
            g_raw = proj_ref[bi, rows, 3 * mix:4 * mix]
            for hd in range(n_head):
                hl = slice(hd * hk, (hd + 1) * hk)
                r0 = hd * n_sub * half
                dg = jnp.concatenate(
                    [diag_ref[slot, bi, hf, r0 + j * half:r0 + (j + 1) * half, :]
                     for j in range(n_sub) for hf in range(2)], axis=0)
                at = jnp.where(same_sub & causal, dg, 0.0)
                k_rows, q_cols = [], []
                for j in range(n_sub - 1):
                    r_j = bcum[(j + 1) * sub - 1:(j + 1) * sub, hl]
                    srows = slice(j * sub, (j + 1) * sub)
                    trows = slice((j + 1) * sub, chunk)
                    k_hat = (k[srows, hl] * jnp.exp(r_j - bcum[srows, hl])).astype(BF16)
                    q_j = (q[trows, hl] * jnp.exp(bcum[trows, hl] - r_j)).astype(BF16)
                    q_cols.append(jnp.concatenate([jnp.zeros(((j + 1) * sub, hk), BF16), q_j], axis=0))
                    k_rows.append(jnp.concatenate(
                        [k_hat if jj == j else jnp.zeros((sub, hk), BF16) for jj in range(n_sub - 1)],
                        axis=1))
                if k_rows:
                    k_rows.append(jnp.zeros((sub, (n_sub - 1) * hk), BF16))
                    a_off = lax.dot_general(jnp.concatenate(k_rows, axis=0),
                                            jnp.concatenate(q_cols, axis=1),
                                            (((1,), (1,)), ((), ())), preferred_element_type=F32)
                    at = jnp.where(later_sub, a_off, at)
                st = st_ref[bi * n_head + hd]
                o = lax.dot_general(at.astype(BF16), vb[:, hl], (((0,), (0,)), ((), ())),
                                    preferred_element_type=F32)
                o = o + lax.dot_general(q_in[:, hl], st.astype(BF16), (((1,), (1,)), ((), ())),
                                        preferred_element_type=F32)
                st_ref[bi * n_head + hd] = decay_last[:, hl] * st + lax.dot_general(
                    vb[:, hl], k_st[:, hl], (((0,), (0,)), ((), ())), preferred_element_type=F32)
                o = _rms_scale(o, ng_ref[...])
                g_h = g_raw[:, hl]
                o_ref[bi, rows, hl] = o * (g_h * jax.nn.sigmoid(g_h))

    n_chunk = tb // chunk
    stage1(0, 0)

    def body(c, _):
        stage2(c - 1, lax.rem(c - 1, 2))
        stage1(c, lax.rem(c, 2))
        return 0

    lax.fori_loop(1, n_chunk, body, 0)
    stage2(n_chunk - 1, (n_chunk - 1) % 2)


def _hgrn2(hn, w_proj, lb_logits, norm_gain, sel, layer):
    b, s, d = hn.shape
    mix = w_proj.shape[1] // 4
    tb = min(HG_TB, s)
    chunk, sub = HG_CHUNK, HG_SUB
    assert sub == 2 * SUBLANES and chunk % sub == 0 and tb % (2 * chunk) == 0
    n_head = mix // HG_HEAD_DIM
    depth = lb_logits.shape[0]
    return pl.pallas_call(
        functools.partial(_hgrn2_kernel, layer=layer, tb=tb, chunk=chunk, sub=sub),
        grid=(s // tb,),
        in_specs=[
            pl.BlockSpec((b, tb, d), lambda j: (0, j, 0)),
            pl.BlockSpec((d, 4 * mix), lambda j: (0, 0)),
            pl.BlockSpec((depth, mix), lambda j: (0, 0)),
            pl.BlockSpec((1, HG_HEAD_DIM), lambda j: (0, 0)),
            pl.BlockSpec((sub * HG_HEAD_DIM, chunk), lambda j: (0, 0)),
        ],
        out_specs=pl.BlockSpec((b, tb, mix), lambda j: (0, j, 0)),
        out_shape=jax.ShapeDtypeStruct((b, s, mix), F32),
        scratch_shapes=[
            pltpu.VMEM((b * n_head, HG_HEAD_DIM, HG_HEAD_DIM), F32),
            pltpu.VMEM((b, tb, 4 * mix), F32),
            pltpu.VMEM((2, b, 3, chunk, mix), F32),
            pltpu.VMEM((2, b, 2, n_head * chunk // 2, chunk), F32),
        ],
        compiler_params=_params("arbitrary"),
        name="hgrn2",
    )(hn, w_proj, lb_logits, norm_gain, sel)


def _merge_kernel(ya_ref, yb_ref, yc_ref, hn_ref, x_ref, wg_ref, wbr_ref, wout_ref, gain_ref, o_ref):
    d = x_ref.shape[-1]
    hn = hn_ref[...]
    merged = None
    for idx, y_ref in enumerate((ya_ref, yb_ref, yc_ref)):
        gate = jnp.dot(hn, wg_ref[:, idx * d:(idx + 1) * d], preferred_element_type=F32)
        part = jnp.dot(y_ref[...].astype(BF16), wbr_ref[idx], preferred_element_type=F32)
        part = jax.nn.sigmoid(gate) * part
        merged = part if merged is None else merged + part
    out = jnp.dot(merged.astype(BF16), wout_ref[...], preferred_element_type=F32)
    o_ref[...] = x_ref[...] + _rms_scale(out, gain_ref[...])


def _merge(ya, yb, yc, hn2, x2, wg_bf, wbr_bf, wout_bf, gain):
    t, d = x2.shape
    mix = ya.shape[1]
    tm = min(MERGE_TM, t)
    ybs = lambda: pl.BlockSpec((tm, mix), lambda i: (i, 0))
    row = lambda: pl.BlockSpec((tm, d), lambda i: (i, 0))
    return pl.pallas_call(
        _merge_kernel,
        grid=(t // tm,),
        in_specs=[
            ybs(), ybs(), ybs(), row(), row(),
            pl.BlockSpec(wg_bf.shape, lambda i: (0, 0)),
            pl.BlockSpec(wbr_bf.shape, lambda i: (0, 0, 0)),
            pl.BlockSpec((d, d), lambda i: (0, 0)),
            pl.BlockSpec((1, d), lambda i: (0, 0)),
        ],
        out_specs=row(),
        out_shape=jax.ShapeDtypeStruct((t, d), F32),
        compiler_params=_params("parallel"),
        name="merge_out",
    )(ya, yb, yc, hn2, x2, wg_bf, wbr_bf, wout_bf, gain)


def _mlp_kernel(x_ref, g_in_ref, wup_ref, wdn_ref, g_out_ref, o_ref, h_ref, acc_ref):
    f = pl.program_id(1)

    @pl.when(f == 0)
    def _():
        h_ref[...] = _rms_scale(x_ref[...], g_in_ref[...]).astype(BF16)
        acc_ref[...] = jnp.zeros(acc_ref.shape, F32)

    u = jnp.maximum(jnp.dot(h_ref[...], wup_ref[...], preferred_element_type=F32), 0.0)
    acc_ref[...] += jnp.dot((u * u).astype(BF16), wdn_ref[...], preferred_element_type=F32)

    @pl.when(f == pl.num_programs(1) - 1)
    def _():
        o_ref[...] = x_ref[...] + _rms_scale(acc_ref[...], g_out_ref[...])


def _mlp(x2, g_in, wup_bf, wdn_bf, g_out):
    t, d = x2.shape
    dff = wup_bf.shape[1]
    tm, tf = min(MLP_TM, t), min(MLP_TF, dff)
    return pl.pallas_call(
        _mlp_kernel,
        grid=(t // tm, dff // tf),
        in_specs=[
            pl.BlockSpec((tm, d), lambda i, f: (i, 0)),
            pl.BlockSpec((1, d), lambda i, f: (0, 0)),
            pl.BlockSpec((d, tf), lambda i, f: (0, f)),
            pl.BlockSpec((tf, d), lambda i, f: (f, 0)),
            pl.BlockSpec((1, d), lambda i, f: (0, 0)),
        ],
        out_specs=pl.BlockSpec((tm, d), lambda i, f: (i, 0)),
        out_shape=jax.ShapeDtypeStruct((t, d), F32),
        scratch_shapes=[pltpu.VMEM((tm, d), BF16), pltpu.VMEM((tm, d), F32)],
        compiler_params=_params("parallel", "arbitrary"),
        name="mlp",
    )(x2, g_in, wup_bf, wdn_bf, g_out)


def _block_diag(w):
    n, c, _ = w.shape
    eye = jnp.eye(n, dtype=w.dtype)
    return (eye[:, None, :, None] * w[:, :, None, :]).reshape(n * c, n * c)


def kernel(x, ln_gains, w_in, conv_w, conv_b, rg_w_a, rg_b_a, rg_w_x, rg_b_x, rg_lambda,
           lb_logits, hgrn_norm, w_branch, w_out, w_up, w_down):
    b, s, d = x.shape
    depth = w_in.shape[0]
    mix = conv_w.shape[-1]
    t = b * s
    qkv_lo, hg_lo, gate_lo = mix, 4 * mix, 8 * mix

    blk = min(SB_BLK, s)
    tri = (lax.broadcasted_iota(jnp.int32, (blk, blk), 0)
           > lax.broadcasted_iota(jnp.int32, (blk, blk), 1)).astype(BF16)
    sel_r = lax.broadcasted_iota(jnp.int32, (HG_SUB * HG_HEAD_DIM, HG_CHUNK), 0) // HG_HEAD_DIM
    sel_c = lax.broadcasted_iota(jnp.int32, (HG_SUB * HG_HEAD_DIM, HG_CHUNK), 1) % HG_SUB
    sel = (sel_r == sel_c).astype(BF16)

    x2 = x.reshape(t, d)
    for l in range(depth):
        g = ln_gains[l]
        w_l = w_in[l].astype(BF16)
        qkv, hn2 = _in_proj(x2, g[0:1], w_l[:, qkv_lo:hg_lo], mix, SB_HEAD_DIM ** -0.5)
        hn = hn2.reshape(b, s, d)
        y_a = _rglru(hn, w_l[:, :qkv_lo], conv_w[l], conv_b[l:l + 1],
                     _block_diag(rg_w_a[l]).astype(BF16), rg_b_a[l:l + 1],
                     _block_diag(rg_w_x[l]).astype(BF16), rg_b_x[l:l + 1],
                     rg_lambda[l:l + 1])
        y_b = _stick_breaking(qkv.reshape(b, s, -1), tri, mix)
        y_c = _hgrn2(hn, w_l[:, hg_lo:gate_lo], lb_logits, hgrn_norm[l:l + 1], sel, l)
        x2 = _merge(y_a.reshape(t, mix), y_b.reshape(t, mix), y_c.reshape(t, mix), hn2, x2,
                    w_l[:, gate_lo:], w_branch[l].astype(BF16), w_out[l].astype(BF16), g[1:2])
        x2 = _mlp(x2, g[2:3], w_up[l].astype(BF16), w_down[l].astype(BF16), g[3:4])
    return x2.reshape(b, s, d)
```

```python
import functools

import jax
import jax.numpy as jnp
from jax import lax
from jax.experimental import pallas as pl
from jax.experimental.pallas import tpu as pltpu

F32 = jnp.float32
BF16 = jnp.bfloat16

EPS = 1e-6
RG_C = 8.0
CONV_WIDTH = 4
SB_HEAD_DIM = 64
HG_HEAD_DIM = 128
LB_FLOOR = 1e-30
LOG2_E = 1.4426950408889634
SB_ZERO_EXP = 105.0

LANES = 128
SUBLANES = 8
VMEM_LIMIT = 56 * 1024 * 1024

IN_TM = 512
RG_TB = 512
SB_BLK = 256
SB_HEADS_PER_STEP = 8
HG_TB = 512
HG_CHUNK = 64
HG_SUB = 16
MERGE_TM = 256
MLP_TM = 512


def _params(*sem):
    return pltpu.CompilerParams(dimension_semantics=sem, vmem_limit_bytes=VMEM_LIMIT)


def _softplus(x):
    return jnp.maximum(x, 0.0) + jnp.log1p(jnp.exp(-jnp.abs(x)))


def _log1p_exp_neg_abs(x):
    return jnp.log(1.0 + jnp.exp2(jnp.abs(x) * (-LOG2_E)))


def _rms_scale(x, gain):
    inv = lax.rsqrt(jnp.mean(x * x, axis=-1, keepdims=True) + EPS)
    return x * inv * gain


def _in_proj_kernel(x_ref, g_ref, w_ref, qkv_ref, h_ref, *, q_width, q_scale):
    h = _rms_scale(x_ref[...], g_ref[...]).astype(BF16)
    h_ref[...] = h
    out = jnp.dot(h, w_ref[...], preferred_element_type=F32)
    qkv_ref[:, :q_width] = (out[:, :q_width] * q_scale).astype(BF16)
    qkv_ref[:, q_width:] = out[:, q_width:].astype(BF16)


def _in_proj(x2, gain, w_qkv_bf, q_width, q_scale):
    t, d = x2.shape
    n = w_qkv_bf.shape[1]
    tm = min(IN_TM, t)
    return pl.pallas_call(
        functools.partial(_in_proj_kernel, q_width=q_width, q_scale=q_scale),
        grid=(t // tm,),
        in_specs=[
            pl.BlockSpec((tm, d), lambda i: (i, 0)),
            pl.BlockSpec((1, d), lambda i: (0, 0)),
            pl.BlockSpec((d, n), lambda i: (0, 0)),
        ],
        out_specs=[pl.BlockSpec((tm, n), lambda i: (i, 0)),
                   pl.BlockSpec((tm, d), lambda i: (i, 0))],
        out_shape=[jax.ShapeDtypeStruct((t, n), BF16), jax.ShapeDtypeStruct((t, d), BF16)],
        compiler_params=_params("parallel"),
        name="in_proj",
    )(x2, gain, w_qkv_bf)


def _rglru_kernel(hn_ref, wp_ref, cw_ref, cb_ref, wa_ref, ba_ref, wx_ref, bx_ref, lam_ref,
                  o_ref, tail_ref, a_ref, u_ref, h_ref, *, tb):
    j = pl.program_id(0)
    n_b = hn_ref.shape[0]
    w = wp_ref.shape[-1]

    @pl.when(j == 0)
    def _():
        tail_ref[...] = jnp.zeros(tail_ref.shape, F32)
        h_ref[...] = jnp.zeros(h_ref.shape, F32)

    sp_lam = _softplus(-lam_ref[...])
    row = lax.broadcasted_iota(jnp.int32, (tb, w), 0) + j * tb
    for bi in range(n_b):
        x = jnp.dot(hn_ref[bi], wp_ref[...], preferred_element_type=F32)
        xp = jnp.concatenate([tail_ref[bi], x], axis=0)
        acc = cw_ref[0:1, :] * xp
        for k in range(1, CONV_WIDTH):
            acc = pltpu.roll(acc, 1, 0) + cw_ref[k:k + 1, :] * xp
        xc = acc[SUBLANES:, :] + cb_ref[...]
        tail_ref[bi] = x[tb - SUBLANES:tb, :]

        xcb = xc.astype(BF16)
        r = jax.nn.sigmoid(jnp.dot(xcb, wa_ref[...], preferred_element_type=F32) + ba_ref[...])
        gate_i = jax.nn.sigmoid(jnp.dot(xcb, wx_ref[...], preferred_element_type=F32) + bx_ref[...])
        a = jnp.exp(-RG_C * r * sp_lam)
        mult = jnp.sqrt(jnp.maximum(1.0 - a * a, 0.0))
        mult = jnp.where(row == 0, 1.0, mult)
        a_ref[bi] = a
        u_ref[bi] = mult * (gate_i * xc)

    def step(t, hs):
        out = []
        for bi in range(n_b):
            h = a_ref[bi, pl.ds(t, 1), :] * hs[bi] + u_ref[bi, pl.ds(t, 1), :]
            o_ref[bi, pl.ds(t, 1), :] = h
            out.append(h)
        return tuple(out)

    hs = lax.fori_loop(0, tb, step, tuple(h_ref[bi] for bi in range(n_b)), unroll=SUBLANES)
    for bi in range(n_b):
        h_ref[bi] = hs[bi]


def _rglru(hn, w_proj, cw, cb, wa_bd, ba, wx_bd, bx, lam):
    b, s, d = hn.shape
    mix = w_proj.shape[1]
    tb = min(RG_TB, s)
    vec = lambda: pl.BlockSpec((1, mix), lambda j: (0, 0))
    return pl.pallas_call(
        functools.partial(_rglru_kernel, tb=tb),
        grid=(s // tb,),
        in_specs=[
            pl.BlockSpec((b, tb, d), lambda j: (0, j, 0)),
            pl.BlockSpec((d, mix), lambda j: (0, 0)),
            pl.BlockSpec((CONV_WIDTH, mix), lambda j: (0, 0)),
            vec(),
            pl.BlockSpec((mix, mix), lambda j: (0, 0)),
            vec(),
            pl.BlockSpec((mix, mix), lambda j: (0, 0)),
            vec(),
            vec(),
        ],
        out_specs=pl.BlockSpec((b, tb, mix), lambda j: (0, j, 0)),
        out_shape=jax.ShapeDtypeStruct((b, s, mix), F32),
        scratch_shapes=[
            pltpu.VMEM((b, SUBLANES, mix), F32),
            pltpu.VMEM((b, tb, mix), F32),
            pltpu.VMEM((b, tb, mix), F32),
            pltpu.VMEM((b, 1, mix), F32),
        ],
        compiler_params=_params("arbitrary"),
        name="rglru",
    )(hn, w_proj, cw, cb, wa_bd, ba, wx_bd, bx, lam)


def _sb_kernel(q_ref, k_ref, v_ref, tri_ref, o_ref, acc_ref, car_ref, *, blk, dh, n_h):
    qi = pl.program_id(2)
    row = lax.broadcasted_iota(jnp.int32, (blk, blk), 0)
    col = lax.broadcasted_iota(jnp.int32, (blk, blk), 1)
    causal = col < row
    tri = tri_ref[...]
    heads = [slice(h * dh, (h + 1) * dh) for h in range(n_h)]
    qs = [q_ref[0, :, lanes] for lanes in heads]

    acc_ref[...] = jnp.zeros(acc_ref.shape, F32)
    car_ref[...] = jnp.zeros(car_ref.shape, F32)

    def block(kj, diag, live=None):
        rows = pl.ds(pl.multiple_of(kj * blk, blk), blk)
        carry_min = None
        sps, log_betas = [], []
        for h, lanes in enumerate(heads):
            kb = k_ref[0, rows, lanes]
            z = lax.dot_general(qs[h], kb, (((1,), (1,)), ((), ())), preferred_element_type=F32)
            sp = jnp.maximum(z, 0.0) + _log1p_exp_neg_abs(z)
            log_betas.append(z - sp)
            if diag:
                sp = jnp.where(causal, sp, 0.0)
            sps.append(sp)
        inner_all = jnp.dot(jnp.concatenate([sp.astype(BF16) for sp in sps], axis=0), tri,
                            preferred_element_type=F32)
        for h, lanes in enumerate(heads):
            vb = v_ref[0, rows, lanes]
            sp, log_beta = sps[h], log_betas[h]
            inner = inner_all[h * blk:(h + 1) * blk, :]
            carry = car_ref[h]
            wgt = jnp.exp(log_beta - inner - carry)
            if diag:
                wgt = jnp.where(causal, wgt, 0.0)
            pv = jnp.dot(wgt.astype(BF16), vb, preferred_element_type=F32)
            if live is not None:
                pv = jnp.where(live, pv, 0.0)
            acc_ref[h] += pv
            carry = carry + jnp.sum(sp, axis=1, keepdims=True)
            car_ref[h] = carry
            c_min = jnp.min(carry)
            carry_min = c_min if carry_min is None else jnp.minimum(carry_min, c_min)
        return carry_min

    def cond(c):
        return jnp.logical_and(c[0] >= 0, c[1] < SB_ZERO_EXP)

    def body(c):
        return c[0] - 1, block(c[0], False)

    block(qi, True)
    first_min = block(jnp.maximum(qi - 1, 0), False, live=qi >= 1)
    lax.while_loop(cond, body, (qi - 2, first_min))
    for h, lanes in enumerate(heads):
        o_ref[0, :, lanes] = acc_ref[h]


def _stick_breaking(qkv, tri, mix):
    b, s, _ = qkv.shape
    blk = min(SB_BLK, s)
    dh = SB_HEAD_DIM
    width = SB_HEADS_PER_STEP * dh
    n_grp = mix // width
    return pl.pallas_call(
        functools.partial(_sb_kernel, blk=blk, dh=dh, n_h=SB_HEADS_PER_STEP),
        grid=(b, n_grp, s // blk),
        in_specs=[
            pl.BlockSpec((1, blk, width), lambda i, g, t: (i, t, g)),
            pl.BlockSpec((1, s, width), lambda i, g, t: (i, 0, n_grp + g)),
            pl.BlockSpec((1, s, width), lambda i, g, t: (i, 0, 2 * n_grp + g)),
            pl.BlockSpec((blk, blk), lambda i, g, t: (0, 0)),
        ],
        out_specs=pl.BlockSpec((1, blk, width), lambda i, g, t: (i, t, g)),
        out_shape=jax.ShapeDtypeStruct((b, s, mix), F32),
        scratch_shapes=[
            pltpu.VMEM((SB_HEADS_PER_STEP, blk, dh), F32),
            pltpu.VMEM((SB_HEADS_PER_STEP, blk, 1), F32),
        ],
        compiler_params=_params("parallel", "parallel", "arbitrary"),
        name="stick_breaking",
    )(qkv, qkv, qkv, tri)


def _hgrn2_kernel(hn_ref, wp_ref, lbl_ref, ng_ref, sel_ref, o_ref,
                  st_ref, proj_ref, qkb_ref, diag_ref, *, layer, tb, chunk, sub):
    n_b = hn_ref.shape[0]
    mix = o_ref.shape[-1]
    hk = HG_HEAD_DIM
    n_head = mix // hk
    n_sub = chunk // sub
    half = sub // 2

    @pl.when(pl.program_id(0) == 0)
    def _():
        st_ref[...] = jnp.zeros(st_ref.shape, F32)

    for bi in range(n_b):
        proj_ref[bi] = jnp.dot(hn_ref[bi], wp_ref[...], preferred_element_type=F32)

    lbl = lbl_ref[...]
    p = jnp.exp(lbl - jnp.max(lbl, axis=0, keepdims=True))
    p = p / jnp.sum(p, axis=0, keepdims=True)
    lb = jnp.zeros((1, mix), F32)
    for m in range(1, layer + 1):
        lb = lb + p[m:m + 1, :]
    log_lb = jnp.log(jnp.maximum(lb, LB_FLOOR))
    log_1m_lb = jnp.log1p(-lb)

    row = lax.broadcasted_iota(jnp.int32, (chunk, chunk), 0)
    col = lax.broadcasted_iota(jnp.int32, (chunk, chunk), 1)
    same_sub = (row // sub) == (col // sub)
    later_sub = (col // sub) > (row // sub)
    causal = row <= col
    trow = lax.broadcasted_iota(jnp.int32, (chunk, mix), 0)

    def stage1(c, slot):
        rows = pl.ds(pl.multiple_of(c * chunk, chunk), chunk)
        for bi in range(n_b):
            q_raw = proj_ref[bi, rows, 0:mix]
            f_pre = proj_ref[bi, rows, mix:2 * mix]
            q = q_raw * jax.nn.sigmoid(q_raw)
            t2 = log_1m_lb - (jnp.maximum(-f_pre, 0.0) + _log1p_exp_neg_abs(f_pre))
            log_f = jnp.maximum(log_lb, t2) + _log1p_exp_neg_abs(log_lb - t2)
            k = (1.0 - lb) * jax.nn.sigmoid(-f_pre)
            bcum = log_f
            d = 1
            while d < chunk:
                bcum = bcum + jnp.where(trow >= d, pltpu.roll(bcum, d, 0), 0.0)
                d *= 2
            b3 = (bcum * LOG2_E).reshape(n_sub, sub, mix)
            q3 = q.reshape(n_sub, sub, mix)
            k3 = k.reshape(n_sub, sub, mix)
            qkb_ref[slot, bi, 0] = q
            qkb_ref[slot, bi, 1] = k
            qkb_ref[slot, bi, 2] = bcum
            tiles = ([], [])
            for tl in range(sub):
                b_t = b3[:, tl:tl + 1, :]
                q_t = q3[:, tl:tl + 1, :]
                for hf in range(1 if tl < half else 2):
                    srows = slice(hf * half, (hf + 1) * half)
                    dec = jnp.exp2(jnp.minimum(b_t - b3[:, srows, :], 0.0))
                    tile = (dec * (q_t * k3[:, srows, :])).reshape(n_sub * half, mix).astype(BF16)
                    tiles[hf].append(tile)
            for hf in range(2):
                lhs = jnp.concatenate(
                    [jnp.concatenate([t[:, hd * hk:(hd + 1) * hk] for t in tiles[hf]], axis=1)
                     for hd in range(n_head)], axis=0)
                diag_ref[slot, bi, hf] = jnp.dot(lhs, sel_ref[hf * half * hk:, :],
                                                 preferred_element_type=F32)

    def stage2(c, slot):
        rows = pl.ds(pl.multiple_of(c * chunk, chunk), chunk)
        for bi in range(n_b):
            q = qkb_ref[slot, bi, 0]
            k = qkb_ref[slot, bi, 1]
            bcum = qkb_ref[slot, bi, 2]
            b_last = bcum[chunk - 1:chunk, :]
            vb = proj_ref[bi, rows, 2 * mix:3 * mix].astype(BF16)
            q_in = (q * jnp.exp(bcum)).astype(BF16)
            k_st = (k * jnp.exp(b_last - bcum)).astype(BF16)
            decay_last = jnp.exp(b_last)
            g_raw = proj_ref[bi, rows, 3 * mix:4 * mix]
            for hd in range(n_head):
                hl = slice(hd * hk, (hd + 1) * hk)
                r0 = hd * n_sub * half
                dg = jnp.concatenate(
                    [diag_ref[slot, bi, hf, r0 + j * half:r0 + (j + 1) * half, :]
                     for j in range(n_sub) for hf in range(2)], axis=0)
                at = jnp.where(same_sub & causal, dg, 0.0)
                k_rows, q_cols = [], []
                for j in range(n_sub - 1):
                    r_j = bcum[(j + 1) * sub - 1:(j + 1) * sub, hl]
                    srows = slice(j * sub, (j + 1) * sub)
                    trows = slice((j + 1) * sub, chunk)
                    k_hat = (k[srows, hl] * jnp.exp(r_j - bcum[srows, hl])).astype(BF16)
                    q_j = (q[trows, hl] * jnp.exp(bcum[trows, hl] - r_j)).astype(BF16)
                    q_cols.append(jnp.concatenate([jnp.zeros(((j + 1) * sub, hk), BF16), q_j], axis=0))
                    k_rows.append(jnp.concatenate(
                        [k_hat if jj == j else jnp.zeros((sub, hk), BF16) for jj in range(n_sub - 1)],
                        axis=1))
                if k_rows:
                    k_rows.append(jnp.zeros((sub, (n_sub - 1) * hk), BF16))
                    a_off = lax.dot_general(jnp.concatenate(k_rows, axis=0),
                                            jnp.concatenate(q_cols, axis=1),
                                            (((1,), (1,)), ((), ())), preferred_element_type=F32)
                    at = jnp.where(later_sub, a_off, at)
                st = st_ref[bi * n_head + hd]
                o = lax.dot_general(at.astype(BF16), vb[:, hl], (((0,), (0,)), ((), ())),
                                    preferred_element_type=F32)
                o = o + lax.dot_general(q_in[:, hl], st.astype(BF16), (((1,), (1,)), ((), ())),
                                        preferred_element_type=F32)
                st_ref[bi * n_head + hd] = decay_last[:, hl] * st + lax.dot_general(
                    vb[:, hl], k_st[:, hl], (((0,), (0,)), ((), ())), preferred_element_type=F32)
                o = _rms_scale(o, ng_ref[...])
                g_h = g_raw[:, hl]
                o_ref[bi, rows, hl] = o * (g_h * jax.nn.sigmoid(g_h))

    n_chunk = tb // chunk
    stage1(0, 0)

    def body(c, _):
        stage2(c - 1, lax.rem(c - 1, 2))
        stage1(c, lax.rem(c, 2))
        return 0

    lax.fori_loop(1, n_chunk, body, 0)
    stage2(n_chunk - 1, (n_chunk - 1) % 2)


def _hgrn2(hn, w_proj, lb_logits, norm_gain, sel, layer):
    b, s, d = hn.shape
    mix = w_proj.shape[1] // 4
    tb = min(HG_TB, s)
    chunk, sub = HG_CHUNK, HG_SUB
    assert sub == 2 * SUBLANES and chunk % sub == 0 and tb % (2 * chunk) == 0
    n_head = mix // HG_HEAD_DIM
    depth = lb_logits.shape[0]
    return pl.pallas_call(
        functools.partial(_hgrn2_kernel, layer=layer, tb=tb, chunk=chunk, sub=sub),
        grid=(s // tb,),
        in_specs=[
            pl.BlockSpec((b, tb, d), lambda j: (0, j, 0)),
            pl.BlockSpec((d, 4 * mix), lambda j: (0, 0)),
            pl.BlockSpec((depth, mix), lambda j: (0, 0)),
            pl.BlockSpec((1, HG_HEAD_DIM), lambda j: (0, 0)),
            pl.BlockSpec((sub * HG_HEAD_DIM, chunk), lambda j: (0, 0)),
        ],
        out_specs=pl.BlockSpec((b, tb, mix), lambda j: (0, j, 0)),
        out_shape=jax.ShapeDtypeStruct((b, s, mix), F32),
        scratch_shapes=[
            pltpu.VMEM((b * n_head, HG_HEAD_DIM, HG_HEAD_DIM), F32),
            pltpu.VMEM((b, tb, 4 * mix), F32),
            pltpu.VMEM((2, b, 3, chunk, mix), F32),
            pltpu.VMEM((2, b, 2, n_head * chunk // 2, chunk), F32),
        ],
        compiler_params=_params("arbitrary"),
        name="hgrn2",
    )(hn, w_proj, lb_logits, norm_gain, sel)


def _merge_kernel(ya_ref, yb_ref, yc_ref, hn_ref, x_ref, wg_ref, wbr_ref, wout_ref, gain_ref, o_ref):
    d = x_ref.shape[-1]
    hn = hn_ref[...]
    merged = None
    for idx, y_ref in enumerate((ya_ref, yb_ref, yc_ref)):
        gate = jnp.dot(hn, wg_ref[:, idx * d:(idx + 1) * d], preferred_element_type=F32)
        part = jnp.dot(y_ref[...].astype(BF16), wbr_ref[idx], preferred_element_type=F32)
        part = jax.nn.sigmoid(gate) * part
        merged = part if merged is None else merged + part
    out = jnp.dot(merged.astype(BF16), wout_ref[...], preferred_element_type=F32)
    o_ref[...] = x_ref[...] + _rms_scale(out, gain_ref[...])


def _merge(ya, yb, yc, hn2, x2, wg_bf, wbr_bf, wout_bf, gain):
    t, d = x2.shape
    mix = ya.shape[1]
    tm = min(MERGE_TM, t)
    ybs = lambda: pl.BlockSpec((tm, mix), lambda i: (i, 0))
    row = lambda: pl.BlockSpec((tm, d), lambda i: (i, 0))
    return pl.pallas_call(
        _merge_kernel,
        grid=(t // tm,),
        in_specs=[
            ybs(), ybs(), ybs(), row(), row(),
            pl.BlockSpec(wg_bf.shape, lambda i: (0, 0)),
            pl.BlockSpec(wbr_bf.shape, lambda i: (0, 0, 0)),
            pl.BlockSpec((d, d), lambda i: (0, 0)),
            pl.BlockSpec((1, d), lambda i: (0, 0)),
        ],
        out_specs=row(),
        out_shape=jax.ShapeDtypeStruct((t, d), F32),
        compiler_params=_params("parallel"),
        name="merge_out",
    )(ya, yb, yc, hn2, x2, wg_bf, wbr_bf, wout_bf, gain)


def _mlp_kernel(x_ref, g_in_ref, wup_ref, wdn_ref, g_out_ref, o_ref):
    x = x_ref[...]
    h = _rms_scale(x, g_in_ref[...]).astype(BF16)
    u = jnp.maximum(jnp.dot(h, wup_ref[...], preferred_element_type=F32), 0.0)
    m = jnp.dot((u * u).astype(BF16), wdn_ref[...], preferred_element_type=F32)
    o_ref[...] = x + _rms_scale(m, g_out_ref[...])


def _mlp(x2, g_in, wup_bf, wdn_bf, g_out):
    t, d = x2.shape
    dff = wup_bf.shape[1]
    tm = min(MLP_TM, t)
    resident = dict(pipeline_mode=pl.Buffered(1))
    return pl.pallas_call(
        _mlp_kernel,
        grid=(t // tm,),
        in_specs=[
            pl.BlockSpec((tm, d), lambda i: (i, 0)),
            pl.BlockSpec((1, d), lambda i: (0, 0)),
            pl.BlockSpec((d, dff), lambda i: (0, 0), **resident),
            pl.BlockSpec((dff, d), lambda i: (0, 0), **resident),
            pl.BlockSpec((1, d), lambda i: (0, 0)),
        ],
        out_specs=pl.BlockSpec((tm, d), lambda i: (i, 0)),
        out_shape=jax.ShapeDtypeStruct((t, d), F32),
        compiler_params=_params("parallel"),
        name="mlp",
    )(x2, g_in, wup_bf, wdn_bf, g_out)


def _block_diag(w):
    n, c, _ = w.shape
    eye = jnp.eye(n, dtype=w.dtype)
    return (eye[:, None, :, None] * w[:, :, None, :]).reshape(n * c, n * c)


def kernel(x, ln_gains, w_in, conv_w, conv_b, rg_w_a, rg_b_a, rg_w_x, rg_b_x, rg_lambda,
           lb_logits, hgrn_norm, w_branch, w_out, w_up, w_down):
    b, s, d = x.shape
    depth = w_in.shape[0]
    mix = conv_w.shape[-1]
    t = b * s
    qkv_lo, hg_lo, gate_lo = mix, 4 * mix, 8 * mix

    blk = min(SB_BLK, s)
    tri = (lax.broadcasted_iota(jnp.int32, (blk, blk), 0)
           > lax.broadcasted_iota(jnp.int32, (blk, blk), 1)).astype(BF16)
    sel_r = lax.broadcasted_iota(jnp.int32, (HG_SUB * HG_HEAD_DIM, HG_CHUNK), 0) // HG_HEAD_DIM
    sel_c = lax.broadcasted_iota(jnp.int32, (HG_SUB * HG_HEAD_DIM, HG_CHUNK), 1) % HG_SUB
    sel = (sel_r == sel_c).astype(BF16)

    x2 = x.reshape(t, d)
    for l in range(depth):
        g = ln_gains[l]
        w_l = w_in[l].astype(BF16)
        qkv, hn2 = _in_proj(x2, g[0:1], w_l[:, qkv_lo:hg_lo], mix, SB_HEAD_DIM ** -0.5)
        hn = hn2.reshape(b, s, d)
        y_a = _rglru(hn, w_l[:, :qkv_lo], conv_w[l], conv_b[l:l + 1],
                     _block_diag(rg_w_a[l]).astype(BF16), rg_b_a[l:l + 1],
                     _block_diag(rg_w_x[l]).astype(BF16), rg_b_x[l:l + 1],
                     rg_lambda[l:l + 1])
        y_b = _stick_breaking(qkv.reshape(b, s, -1), tri, mix)
        y_c = _hgrn2(hn, w_l[:, hg_lo:gate_lo], lb_logits, hgrn_norm[l:l + 1], sel, l)
        x2 = _merge(y_a.reshape(t, mix), y_b.reshape(t, mix), y_c.reshape(t, mix), hn2, x2,
                    w_l[:, gate_lo:], w_branch[l].astype(BF16), w_out[l].astype(BF16), g[1:2])
        x2 = _mlp(x2, g[2:3], w_up[l].astype(BF16), w_down[l].astype(BF16), g[3:4])
    return x2.reshape(b, s, d)
```

```python
import functools

import jax
import jax.numpy as jnp
from jax import lax
from jax.experimental import pallas as pl
from jax.experimental.pallas import tpu as pltpu

F32 = jnp.float32
BF16 = jnp.bfloat16

EPS = 1e-6
RG_C = 8.0
CONV_WIDTH = 4
SB_HEAD_DIM = 64
HG_HEAD_DIM = 128
LB_FLOOR = 1e-30
LOG2_E = 1.4426950408889634
SB_ZERO_EXP = 105.0

LANES = 128
SUBLANES = 8
VMEM_LIMIT = 56 * 1024 * 1024

IN_TM = 512
RG_TB = 512
SB_BLK = 256
SB_HEADS_PER_STEP = 8
HG_TB = 512
HG_CHUNK = 64
HG_SUB = 16
MERGE_TM = 256
MLP_TM = 512


def _params(*sem):
    return pltpu.CompilerParams(dimension_semantics=sem, vmem_limit_bytes=VMEM_LIMIT)


def _softplus(x):
    return jnp.maximum(x, 0.0) + jnp.log1p(jnp.exp(-jnp.abs(x)))


def _log1p_exp_neg_abs(x):
    return jnp.log(1.0 + jnp.exp2(jnp.abs(x) * (-LOG2_E)))


def _rms_scale(x, gain):
    inv = lax.rsqrt(jnp.mean(x * x, axis=-1, keepdims=True) + EPS)
    return x * inv * gain


def _in_proj_kernel(x_ref, g_ref, wq_ref, wk_ref, wv_ref, qkv_ref, h_ref, *, q_scale):
    h = _rms_scale(x_ref[...], g_ref[...]).astype(BF16)
    h_ref[...] = h
    w = wq_ref.shape[-1]
    for idx, (w_ref, scale) in enumerate(((wq_ref, q_scale), (wk_ref, 1.0), (wv_ref, 1.0))):
        out = jnp.dot(h, w_ref[...], preferred_element_type=F32)
        if scale != 1.0:
            out = out * scale
        qkv_ref[:, idx * w:(idx + 1) * w] = out.astype(BF16)


def _in_proj(x2, gain, w_bf, mix, q_col0, q_scale):
    t, d = x2.shape
    tm = min(IN_TM, t)
    cb = q_col0 // mix
    wcol = lambda off: pl.BlockSpec((d, mix), lambda i: (0, cb + off))
    return pl.pallas_call(
        functools.partial(_in_proj_kernel, q_scale=q_scale),
        grid=(t // tm,),
        in_specs=[
            pl.BlockSpec((tm, d), lambda i: (i, 0)),
            pl.BlockSpec((1, d), lambda i: (0, 0)),
            wcol(0), wcol(1), wcol(2),
        ],
        out_specs=[pl.BlockSpec((tm, 3 * mix), lambda i: (i, 0)),
                   pl.BlockSpec((tm, d), lambda i: (i, 0))],
        out_shape=[jax.ShapeDtypeStruct((t, 3 * mix), BF16), jax.ShapeDtypeStruct((t, d), BF16)],
        compiler_params=_params("parallel"),
        name="in_proj",
    )(x2, gain, w_bf, w_bf, w_bf)


def _rglru_kernel(hn_ref, wp_ref, cw_ref, cb_ref, wa_ref, ba_ref, wx_ref, bx_ref, lam_ref,
                  o_ref, tail_ref, a_ref, u_ref, h_ref, *, tb):
    j = pl.program_id(0)
    n_b = hn_ref.shape[0]
    w = wp_ref.shape[-1]

    @pl.when(j == 0)
    def _():
        tail_ref[...] = jnp.zeros(tail_ref.shape, F32)
        h_ref[...] = jnp.zeros(h_ref.shape, F32)

    sp_lam = _softplus(-lam_ref[...])
    row = lax.broadcasted_iota(jnp.int32, (tb, w), 0) + j * tb
    for bi in range(n_b):
        x = jnp.dot(hn_ref[bi], wp_ref[...], preferred_element_type=F32)
        xp = jnp.concatenate([tail_ref[bi], x], axis=0)
        acc = cw_ref[0:1, :] * xp
        for k in range(1, CONV_WIDTH):
            acc = pltpu.roll(acc, 1, 0) + cw_ref[k:k + 1, :] * xp
        xc = acc[SUBLANES:, :] + cb_ref[...]
        tail_ref[bi] = x[tb - SUBLANES:tb, :]

        xcb = xc.astype(BF16)
        r = jax.nn.sigmoid(jnp.dot(xcb, wa_ref[...], preferred_element_type=F32) + ba_ref[...])
        gate_i = jax.nn.sigmoid(jnp.dot(xcb, wx_ref[...], preferred_element_type=F32) + bx_ref[...])
        a = jnp.exp(-RG_C * r * sp_lam)
        mult = jnp.sqrt(jnp.maximum(1.0 - a * a, 0.0))
        mult = jnp.where(row == 0, 1.0, mult)
        a_ref[bi] = a
        u_ref[bi] = mult * (gate_i * xc)

    def step(t, hs):
        out = []
        for bi in range(n_b):
            h = a_ref[bi, pl.ds(t, 1), :] * hs[bi] + u_ref[bi, pl.ds(t, 1), :]
            o_ref[bi, pl.ds(t, 1), :] = h
            out.append(h)
        return tuple(out)

    hs = lax.fori_loop(0, tb, step, tuple(h_ref[bi] for bi in range(n_b)), unroll=SUBLANES)
    for bi in range(n_b):
        h_ref[bi] = hs[bi]


def _rglru(hn, w_bf, col0, cw, cb, wa_bd, ba, wx_bd, bx, lam):
    b, s, d = hn.shape
    mix = cw.shape[-1]
    wcb = col0 // mix
    tb = min(RG_TB, s)
    vec = lambda: pl.BlockSpec((1, mix), lambda j: (0, 0))
    return pl.pallas_call(
        functools.partial(_rglru_kernel, tb=tb),
        grid=(s // tb,),
        in_specs=[
            pl.BlockSpec((b, tb, d), lambda j: (0, j, 0)),
            pl.BlockSpec((d, mix), lambda j: (0, wcb)),
            pl.BlockSpec((CONV_WIDTH, mix), lambda j: (0, 0)),
            vec(),
            pl.BlockSpec((mix, mix), lambda j: (0, 0)),
            vec(),
            pl.BlockSpec((mix, mix), lambda j: (0, 0)),
            vec(),
            vec(),
        ],
        out_specs=pl.BlockSpec((b, tb, mix), lambda j: (0, j, 0)),
        out_shape=jax.ShapeDtypeStruct((b, s, mix), F32),
        scratch_shapes=[
            pltpu.VMEM((b, SUBLANES, mix), F32),
            pltpu.VMEM((b, tb, mix), F32),
            pltpu.VMEM((b, tb, mix), F32),
            pltpu.VMEM((b, 1, mix), F32),
        ],
        compiler_params=_params("arbitrary"),
        name="rglru",
    )(hn, w_bf, cw, cb, wa_bd, ba, wx_bd, bx, lam)


def _sb_kernel(q_ref, k_ref, v_ref, tri_ref, o_ref, acc_ref, car_ref, *, blk, dh, n_h):
    qi = pl.program_id(2)
    row = lax.broadcasted_iota(jnp.int32, (blk, blk), 0)
    col = lax.broadcasted_iota(jnp.int32, (blk, blk), 1)
    causal = col < row
    tri = tri_ref[...]
    heads = [slice(h * dh, (h + 1) * dh) for h in range(n_h)]
    qs = [q_ref[0, :, lanes] for lanes in heads]

    acc_ref[...] = jnp.zeros(acc_ref.shape, F32)
    car_ref[...] = jnp.zeros(car_ref.shape, F32)

    def block(kj, diag, live=None):
        rows = pl.ds(pl.multiple_of(kj * blk, blk), blk)
        carry_min = None
        sps, log_betas = [], []
        for h, lanes in enumerate(heads):
            kb = k_ref[0, rows, lanes]
            z = lax.dot_general(qs[h], kb, (((1,), (1,)), ((), ())), preferred_element_type=F32)
            sp = jnp.maximum(z, 0.0) + _log1p_exp_neg_abs(z)
            log_betas.append(z - sp)
            if diag:
                sp = jnp.where(causal, sp, 0.0)
            sps.append(sp)
        inner_all = jnp.dot(jnp.concatenate([sp.astype(BF16) for sp in sps], axis=0), tri,
                            preferred_element_type=F32)
        for h, lanes in enumerate(heads):
            vb = v_ref[0, rows, lanes]
            sp, log_beta = sps[h], log_betas[h]
            inner = inner_all[h * blk:(h + 1) * blk, :]
            carry = car_ref[h]
            wgt = jnp.exp(log_beta - inner - carry)
            if diag:
                wgt = jnp.where(causal, wgt, 0.0)
            pv = jnp.dot(wgt.astype(BF16), vb, preferred_element_type=F32)
            if live is not None:
                pv = jnp.where(live, pv, 0.0)
            acc_ref[h] += pv
            carry = carry + jnp.sum(sp, axis=1, keepdims=True)
            car_ref[h] = carry
            c_min = jnp.min(carry)
            carry_min = c_min if carry_min is None else jnp.minimum(carry_min, c_min)
        return carry_min

    def cond(c):
        return jnp.logical_and(c[0] >= 0, c[1] < SB_ZERO_EXP)

    def body(c):
        return c[0] - 1, block(c[0], False)

    block(qi, True)
    first_min = block(jnp.maximum(qi - 1, 0), False, live=qi >= 1)
    lax.while_loop(cond, body, (qi - 2, first_min))
    for h, lanes in enumerate(heads):
        o_ref[0, :, lanes] = acc_ref[h].astype(o_ref.dtype)


def _stick_breaking(qkv, tri, mix):
    b, s, _ = qkv.shape
    blk = min(SB_BLK, s)
    dh = SB_HEAD_DIM
    width = SB_HEADS_PER_STEP * dh
    n_grp = mix // width
    return pl.pallas_call(
        functools.partial(_sb_kernel, blk=blk, dh=dh, n_h=SB_HEADS_PER_STEP),
        grid=(b, n_grp, s // blk),
        in_specs=[
            pl.BlockSpec((1, blk, width), lambda i, g, t: (i, t, g)),
            pl.BlockSpec((1, s, width), lambda i, g, t: (i, 0, n_grp + g)),
            pl.BlockSpec((1, s, width), lambda i, g, t: (i, 0, 2 * n_grp + g)),
            pl.BlockSpec((blk, blk), lambda i, g, t: (0, 0)),
        ],
        out_specs=pl.BlockSpec((1, blk, width), lambda i, g, t: (i, t, g)),
        out_shape=jax.ShapeDtypeStruct((b, s, mix), BF16),
        scratch_shapes=[
            pltpu.VMEM((SB_HEADS_PER_STEP, blk, dh), F32),
            pltpu.VMEM((SB_HEADS_PER_STEP, blk, 1), F32),
        ],
        compiler_params=_params("parallel", "parallel", "arbitrary"),
        name="stick_breaking",
    )(qkv, qkv, qkv, tri)


def _hgrn2_kernel(hn_ref, wp_ref, lbl_ref, ng_ref, sel_ref, o_ref,
                  st_ref, proj_ref, qkb_ref, diag_ref, *, layer, tb, chunk, sub):
    n_b = hn_ref.shape[0]
    mix = o_ref.shape[-1]
    hk = HG_HEAD_DIM
    n_head = mix // hk
    n_sub = chunk // sub
    half = sub // 2

    @pl.when(pl.program_id(0) == 0)
    def _():
        st_ref[...] = jnp.zeros(st_ref.shape, F32)

    for bi in range(n_b):
        proj_ref[bi] = jnp.dot(hn_ref[bi], wp_ref[...], preferred_element_type=F32)

    lbl = lbl_ref[...]
    p = jnp.exp(lbl - jnp.max(lbl, axis=0, keepdims=True))
    p = p / jnp.sum(p, axis=0, keepdims=True)
    lb = jnp.zeros((1, mix), F32)
    for m in range(1, layer + 1):
        lb = lb + p[m:m + 1, :]
    log_lb = jnp.log(jnp.maximum(lb, LB_FLOOR))
    log_1m_lb = jnp.log1p(-lb)

    row = lax.broadcasted_iota(jnp.int32, (chunk, chunk), 0)
    col = lax.broadcasted_iota(jnp.int32, (chunk, chunk), 1)
    same_sub = (row // sub) == (col // sub)
    later_sub = (col // sub) > (row // sub)
    causal = row <= col
    trow = lax.broadcasted_iota(jnp.int32, (chunk, mix), 0)

    def stage1(c, slot):
        rows = pl.ds(pl.multiple_of(c * chunk, chunk), chunk)
        for bi in range(n_b):
            q_raw = proj_ref[bi, rows, 0:mix]
            f_pre = proj_ref[bi, rows, mix:2 * mix]
            q = q_raw * jax.nn.sigmoid(q_raw)
            t2 = log_1m_lb - (jnp.maximum(-f_pre, 0.0) + _log1p_exp_neg_abs(f_pre))
            log_f = jnp.maximum(log_lb, t2) + _log1p_exp_neg_abs(log_lb - t2)
            k = (1.0 - lb) * jax.nn.sigmoid(-f_pre)
            bcum = log_f
            d = 1
            while d < chunk:
                bcum = bcum + jnp.where(trow >= d, pltpu.roll(bcum, d, 0), 0.0)
                d *= 2
            b3 = (bcum * LOG2_E).reshape(n_sub, sub, mix)
            q3 = q.reshape(n_sub, sub, mix)
            k3 = k.reshape(n_sub, sub, mix)
            qkb_ref[slot, bi, 0] = q
            qkb_ref[slot, bi, 1] = k
            qkb_ref[slot, bi, 2] = bcum
            tiles = ([], [])
            for tl in range(sub):
                b_t = b3[:, tl:tl + 1, :]
                q_t = q3[:, tl:tl + 1, :]
                for hf in range(1 if tl < half else 2):
                    srows = slice(hf * half, (hf + 1) * half)
                    dec = jnp.exp2(jnp.minimum(b_t - b3[:, srows, :], 0.0))
                    tile = (dec * (q_t * k3[:, srows, :])).reshape(n_sub * half, mix).astype(BF16)
                    tiles[hf].append(tile)
            for hf in range(2):
                lhs = jnp.concatenate(
                    [jnp.concatenate([t[:, hd * hk:(hd + 1) * hk] for t in tiles[hf]], axis=1)
                     for hd in range(n_head)], axis=0)
                diag_ref[slot, bi, hf] = jnp.dot(lhs, sel_ref[hf * half * hk:, :],
                                                 preferred_element_type=F32)

    def stage2(c, slot):
        rows = pl.ds(pl.multiple_of(c * chunk, chunk), chunk)
        for bi in range(n_b):
            q = qkb_ref[slot, bi, 0]
            k = qkb_ref[slot, bi, 1]
            bcum = qkb_ref[slot, bi, 2]
            b_last = bcum[chunk - 1:chunk, :]
            vb = proj_ref[bi, rows, 2 * mix:3 * mix].astype(BF16)
            q_in = (q * jnp.exp(bcum)).astype(BF16)
            k_st = (k * jnp.exp(b_last - bcum)).astype(BF16)
            decay_last = jnp.exp(b_last)
            g_raw = proj_ref[bi, rows, 3 * mix:4 * mix]
            for hd in range(n_head):
                hl = slice(hd * hk, (hd + 1) * hk)
                r0 = hd * n_sub * half
                dg = jnp.concatenate(
                    [diag_ref[slot, bi, hf, r0 + j * half:r0 + (j + 1) * half, :]
                     for j in range(n_sub) for hf in range(2)], axis=0)
                at = jnp.where(same_sub & causal, dg, 0.0)
                k_rows, q_cols = [], []
                for j in range(n_sub - 1):
                    r_j = bcum[(j + 1) * sub - 1:(j + 1) * sub, hl]
                    srows = slice(j * sub, (j + 1) * sub)
                    trows = slice((j + 1) * sub, chunk)
                    k_hat = (k[srows, hl] * jnp.exp(r_j - bcum[srows, hl])).astype(BF16)
                    q_j = (q[trows, hl] * jnp.exp(bcum[trows, hl] - r_j)).astype(BF16)
                    q_cols.append(jnp.concatenate([jnp.zeros(((j + 1) * sub, hk), BF16), q_j], axis=0))
                    k_rows.append(jnp.concatenate(
                        [k_hat if jj == j else jnp.zeros((sub, hk), BF16) for jj in range(n_sub - 1)],
                        axis=1))
                if k_rows:
                    k_rows.append(jnp.zeros((sub, (n_sub - 1) * hk), BF16))
                    a_off = lax.dot_general(jnp.concatenate(k_rows, axis=0),
                                            jnp.concatenate(q_cols, axis=1),
                                            (((1,), (1,)), ((), ())), preferred_element_type=F32)
                    at = jnp.where(later_sub, a_off, at)
                st = st_ref[bi * n_head + hd]
                o = lax.dot_general(at.astype(BF16), vb[:, hl], (((0,), (0,)), ((), ())),
                                    preferred_element_type=F32)
                o = o + lax.dot_general(q_in[:, hl], st.astype(BF16), (((1,), (1,)), ((), ())),
                                        preferred_element_type=F32)
                st_ref[bi * n_head + hd] = decay_last[:, hl] * st + lax.dot_general(
                    vb[:, hl], k_st[:, hl], (((0,), (0,)), ((), ())), preferred_element_type=F32)
                o = _rms_scale(o, ng_ref[...])
                g_h = g_raw[:, hl]
                o_ref[bi, rows, hl] = (o * (g_h * jax.nn.sigmoid(g_h))).astype(o_ref.dtype)

    n_chunk = tb // chunk
    stage1(0, 0)

    def body(c, _):
        stage2(c - 1, lax.rem(c - 1, 2))
        stage1(c, lax.rem(c, 2))
        return 0

    lax.fori_loop(1, n_chunk, body, 0)
    stage2(n_chunk - 1, (n_chunk - 1) % 2)


def _hgrn2(hn, w_bf, col0, lb_logits, norm_gain, sel, layer):
    b, s, d = hn.shape
    mix = lb_logits.shape[1]
    wcb = col0 // (4 * mix)
    tb = min(HG_TB, s)
    chunk, sub = HG_CHUNK, HG_SUB
    assert sub == 2 * SUBLANES and chunk % sub == 0 and tb % (2 * chunk) == 0
    n_head = mix // HG_HEAD_DIM
    depth = lb_logits.shape[0]
    return pl.pallas_call(
        functools.partial(_hgrn2_kernel, layer=layer, tb=tb, chunk=chunk, sub=sub),
        grid=(s // tb,),
        in_specs=[
            pl.BlockSpec((b, tb, d), lambda j: (0, j, 0)),
            pl.BlockSpec((d, 4 * mix), lambda j: (0, wcb)),
            pl.BlockSpec((depth, mix), lambda j: (0, 0)),
            pl.BlockSpec((1, HG_HEAD_DIM), lambda j: (0, 0)),
            pl.BlockSpec((sub * HG_HEAD_DIM, chunk), lambda j: (0, 0)),
        ],
        out_specs=pl.BlockSpec((b, tb, mix), lambda j: (0, j, 0)),
        out_shape=jax.ShapeDtypeStruct((b, s, mix), BF16),
        scratch_shapes=[
            pltpu.VMEM((b * n_head, HG_HEAD_DIM, HG_HEAD_DIM), F32),
            pltpu.VMEM((b, tb, 4 * mix), F32),
            pltpu.VMEM((2, b, 3, chunk, mix), F32),
            pltpu.VMEM((2, b, 2, n_head * chunk // 2, chunk), F32),
        ],
        compiler_params=_params("arbitrary"),
        name="hgrn2",
    )(hn, w_bf, lb_logits, norm_gain, sel)


def _merge_kernel(ya_ref, yb_ref, yc_ref, hn_ref, x_ref, wga_ref, wgb_ref, wgc_ref,
                  wbr_ref, wout_ref, gain_ref, o_ref):
    hn = hn_ref[...]
    merged = None
    for idx, (y_ref, wg_ref) in enumerate(((ya_ref, wga_ref), (yb_ref, wgb_ref), (yc_ref, wgc_ref))):
        gate = jnp.dot(hn, wg_ref[...], preferred_element_type=F32)
        part = jnp.dot(y_ref[...].astype(BF16), wbr_ref[idx], preferred_element_type=F32)
        part = jax.nn.sigmoid(gate) * part
        merged = part if merged is None else merged + part
    out = jnp.dot(merged.astype(BF16), wout_ref[...], preferred_element_type=F32)
    o_ref[...] = x_ref[...] + _rms_scale(out, gain_ref[...])


def _merge(ya, yb, yc, hn2, x2, w_bf, gate_col0, wbr_bf, wout_bf, gain):
    t, d = x2.shape
    mix = ya.shape[1]
    tm = min(MERGE_TM, t)
    gcb = gate_col0 // d
    ybs = lambda: pl.BlockSpec((tm, mix), lambda i: (i, 0))
    row = lambda: pl.BlockSpec((tm, d), lambda i: (i, 0))
    wgate = lambda off: pl.BlockSpec((d, d), lambda i: (0, gcb + off))
    return pl.pallas_call(
        _merge_kernel,
        grid=(t // tm,),
        in_specs=[
            ybs(), ybs(), ybs(), row(), row(),
            wgate(0), wgate(1), wgate(2),
            pl.BlockSpec(wbr_bf.shape, lambda i: (0, 0, 0)),
            pl.BlockSpec((d, d), lambda i: (0, 0)),
            pl.BlockSpec((1, d), lambda i: (0, 0)),
        ],
        out_specs=row(),
        out_shape=jax.ShapeDtypeStruct((t, d), F32),
        compiler_params=_params("parallel"),
        name="merge_out",
    )(ya, yb, yc, hn2, x2, w_bf, w_bf, w_bf, wbr_bf, wout_bf, gain)


def _mlp_kernel(x_ref, g_in_ref, wup_ref, wdn_ref, g_out_ref, o_ref):
    x = x_ref[...]
    h = _rms_scale(x, g_in_ref[...]).astype(BF16)
    u = jnp.maximum(jnp.dot(h, wup_ref[...], preferred_element_type=F32), 0.0)
    m = jnp.dot((u * u).astype(BF16), wdn_ref[...], preferred_element_type=F32)
    o_ref[...] = x + _rms_scale(m, g_out_ref[...])


def _mlp(x2, g_in, wup_bf, wdn_bf, g_out):
    t, d = x2.shape
    dff = wup_bf.shape[1]
    tm = min(MLP_TM, t)
    resident = dict(pipeline_mode=pl.Buffered(1))
    return pl.pallas_call(
        _mlp_kernel,
        grid=(t // tm,),
        in_specs=[
            pl.BlockSpec((tm, d), lambda i: (i, 0)),
            pl.BlockSpec((1, d), lambda i: (0, 0)),
            pl.BlockSpec((d, dff), lambda i: (0, 0), **resident),
            pl.BlockSpec((dff, d), lambda i: (0, 0), **resident),
            pl.BlockSpec((1, d), lambda i: (0, 0)),
        ],
        out_specs=pl.BlockSpec((tm, d), lambda i: (i, 0)),
        out_shape=jax.ShapeDtypeStruct((t, d), F32),
        compiler_params=_params("parallel"),
        name="mlp",
    )(x2, g_in, wup_bf, wdn_bf, g_out)


def _block_diag(w):
    n, c, _ = w.shape
    eye = jnp.eye(n, dtype=w.dtype)
    return (eye[:, None, :, None] * w[:, :, None, :]).reshape(n * c, n * c)


def kernel(x, ln_gains, w_in, conv_w, conv_b, rg_w_a, rg_b_a, rg_w_x, rg_b_x, rg_lambda,
           lb_logits, hgrn_norm, w_branch, w_out, w_up, w_down):
    b, s, d = x.shape
    depth = w_in.shape[0]
    mix = conv_w.shape[-1]
    t = b * s
    qkv_lo, hg_lo, gate_lo = mix, 4 * mix, 8 * mix

    blk = min(SB_BLK, s)
    tri = (lax.broadcasted_iota(jnp.int32, (blk, blk), 0)
           > lax.broadcasted_iota(jnp.int32, (blk, blk), 1)).astype(BF16)
    sel_r = lax.broadcasted_iota(jnp.int32, (HG_SUB * HG_HEAD_DIM, HG_CHUNK), 0) // HG_HEAD_DIM
    sel_c = lax.broadcasted_iota(jnp.int32, (HG_SUB * HG_HEAD_DIM, HG_CHUNK), 1) % HG_SUB
    sel = (sel_r == sel_c).astype(BF16)

    x2 = x.reshape(t, d)
    for l in range(depth):
        g = ln_gains[l]
        w_l = w_in[l].astype(BF16)
        qkv, hn2 = _in_proj(x2, g[0:1], w_l, mix, qkv_lo, SB_HEAD_DIM ** -0.5)
        hn = hn2.reshape(b, s, d)
        y_a = _rglru(hn, w_l, 0, conv_w[l], conv_b[l:l + 1],
                     _block_diag(rg_w_a[l]).astype(BF16), rg_b_a[l:l + 1],
                     _block_diag(rg_w_x[l]).astype(BF16), rg_b_x[l:l + 1],
                     rg_lambda[l:l + 1])
        y_b = _stick_breaking(qkv.reshape(b, s, -1), tri, mix)
        y_c = _hgrn2(hn, w_l, hg_lo, lb_logits, hgrn_norm[l:l + 1], sel, l)
        x2 = _merge(y_a.reshape(t, mix), y_b.reshape(t, mix), y_c.reshape(t, mix), hn2, x2,
                    w_l, gate_lo, w_branch[l].astype(BF16), w_out[l].astype(BF16), g[1:2])
        x2 = _mlp(x2, g[2:3], w_up[l].astype(BF16), w_down[l].astype(BF16), g[3:4])
    return x2.reshape(b, s, d)
```

```python
import functools

import jax
import jax.numpy as jnp
from jax import lax
from jax.experimental import pallas as pl
from jax.experimental.pallas import tpu as pltpu

F32 = jnp.float32
BF16 = jnp.bfloat16

EPS = 1e-6
RG_C = 8.0
CONV_WIDTH = 4
SB_HEAD_DIM = 64
HG_HEAD_DIM = 128
LB_FLOOR = 1e-30
LOG2_E = 1.4426950408889634
HG_SAFE_SPAN = 60.0
SB_ZERO_EXP = 105.0

LANES = 128
SUBLANES = 8
VMEM_LIMIT = 56 * 1024 * 1024

IN_TM = 512
RG_TB = 512
SB_BLK = 256
SB_HEADS_PER_STEP = 8
HG_TB = 512
HG_CHUNK = 64
HG_SUB = 16
MERGE_TM = 512
MLP_TM = 512


def _params(*sem):
    return pltpu.CompilerParams(dimension_semantics=sem, vmem_limit_bytes=VMEM_LIMIT)


def _softplus(x):
    return jnp.maximum(x, 0.0) + jnp.log1p(jnp.exp(-jnp.abs(x)))


def _log1p_exp_neg_abs(x):
    return jnp.log(1.0 + jnp.exp2(jnp.abs(x) * (-LOG2_E)))


def _rms_scale(x, gain):
    inv = lax.rsqrt(jnp.mean(x * x, axis=-1, keepdims=True) + EPS)
    return x * inv * gain


def _in_proj_kernel(x_ref, g_ref, wq_ref, wk_ref, wv_ref, qkv_ref, h_ref, *, q_scale):
    h = _rms_scale(x_ref[...], g_ref[...]).astype(BF16)
    h_ref[...] = h
    w = wq_ref.shape[-1]
    for idx, (w_ref, scale) in enumerate(((wq_ref, q_scale), (wk_ref, 1.0), (wv_ref, 1.0))):
        out = jnp.dot(h, w_ref[...], preferred_element_type=F32)
        if scale != 1.0:
            out = out * scale
        qkv_ref[:, idx * w:(idx + 1) * w] = out.astype(BF16)


def _in_proj(x2, gain, w_bf, mix, q_col0, q_scale):
    t, d = x2.shape
    tm = min(IN_TM, t)
    cb = q_col0 // mix
    wcol = lambda off: pl.BlockSpec((d, mix), lambda i: (0, cb + off))
    return pl.pallas_call(
        functools.partial(_in_proj_kernel, q_scale=q_scale),
        grid=(t // tm,),
        in_specs=[
            pl.BlockSpec((tm, d), lambda i: (i, 0)),
            pl.BlockSpec((1, d), lambda i: (0, 0)),
            wcol(0), wcol(1), wcol(2),
        ],
        out_specs=[pl.BlockSpec((tm, 3 * mix), lambda i: (i, 0)),
                   pl.BlockSpec((tm, d), lambda i: (i, 0))],
        out_shape=[jax.ShapeDtypeStruct((t, 3 * mix), BF16), jax.ShapeDtypeStruct((t, d), BF16)],
        compiler_params=_params("parallel"),
        name="in_proj",
    )(x2, gain, w_bf, w_bf, w_bf)


def _rglru_kernel(hn_ref, wp_ref, cw_ref, cb_ref, wa_ref, ba_ref, wx_ref, bx_ref, lam_ref,
                  o_ref, tail_ref, a_ref, u_ref, h_ref, *, tb):
    j = pl.program_id(0)
    n_b = hn_ref.shape[0]
    w = wp_ref.shape[-1]

    @pl.when(j == 0)
    def _():
        tail_ref[...] = jnp.zeros(tail_ref.shape, F32)
        h_ref[...] = jnp.zeros(h_ref.shape, F32)

    sp_lam = _softplus(-lam_ref[...])
    row = lax.broadcasted_iota(jnp.int32, (tb, w), 0) + j * tb
    for bi in range(n_b):
        x = jnp.dot(hn_ref[bi], wp_ref[...], preferred_element_type=F32)
        xp = jnp.concatenate([tail_ref[bi], x], axis=0)
        acc = cw_ref[0:1, :] * xp
        for k in range(1, CONV_WIDTH):
            acc = pltpu.roll(acc, 1, 0) + cw_ref[k:k + 1, :] * xp
        xc = acc[SUBLANES:, :] + cb_ref[...]
        tail_ref[bi] = x[tb - SUBLANES:tb, :]

        xcb = xc.astype(BF16)
        r = jax.nn.sigmoid(jnp.dot(xcb, wa_ref[...], preferred_element_type=F32) + ba_ref[...])
        gate_i = jax.nn.sigmoid(jnp.dot(xcb, wx_ref[...], preferred_element_type=F32) + bx_ref[...])
        a = jnp.exp(-RG_C * r * sp_lam)
        mult = jnp.sqrt(jnp.maximum(1.0 - a * a, 0.0))
        mult = jnp.where(row == 0, 1.0, mult)
        a_ref[bi] = a
        u_ref[bi] = mult * (gate_i * xc)

    def step(t, hs):
        out = []
        for bi in range(n_b):
            h = a_ref[bi, pl.ds(t, 1), :] * hs[bi] + u_ref[bi, pl.ds(t, 1), :]
            o_ref[bi, pl.ds(t, 1), :] = h
            out.append(h)
        return tuple(out)

    hs = lax.fori_loop(0, tb, step, tuple(h_ref[bi] for bi in range(n_b)), unroll=SUBLANES)
    for bi in range(n_b):
        h_ref[bi] = hs[bi]


def _rglru(hn, w_bf, col0, cw, cb, wa_bd, ba, wx_bd, bx, lam):
    b, s, d = hn.shape
    mix = cw.shape[-1]
    wcb = col0 // mix
    tb = min(RG_TB, s)
    vec = lambda: pl.BlockSpec((1, mix), lambda j: (0, 0))
    return pl.pallas_call(
        functools.partial(_rglru_kernel, tb=tb),
        grid=(s // tb,),
        in_specs=[
            pl.BlockSpec((b, tb, d), lambda j: (0, j, 0)),
            pl.BlockSpec((d, mix), lambda j: (0, wcb)),
            pl.BlockSpec((CONV_WIDTH, mix), lambda j: (0, 0)),
            vec(),
            pl.BlockSpec((mix, mix), lambda j: (0, 0)),
            vec(),
            pl.BlockSpec((mix, mix), lambda j: (0, 0)),
            vec(),
            vec(),
        ],
        out_specs=pl.BlockSpec((b, tb, mix), lambda j: (0, j, 0)),
        out_shape=jax.ShapeDtypeStruct((b, s, mix), F32),
        scratch_shapes=[
            pltpu.VMEM((b, SUBLANES, mix), F32),
            pltpu.VMEM((b, tb, mix), F32),
            pltpu.VMEM((b, tb, mix), F32),
            pltpu.VMEM((b, 1, mix), F32),
        ],
        compiler_params=_params("arbitrary"),
        name="rglru",
    )(hn, w_bf, cw, cb, wa_bd, ba, wx_bd, bx, lam)


def _sb_kernel(q_ref, k_ref, v_ref, tri_ref, o_ref, acc_ref, car_ref, *, blk, dh, n_h):
    qi = pl.program_id(2)
    row = lax.broadcasted_iota(jnp.int32, (blk, blk), 0)
    col = lax.broadcasted_iota(jnp.int32, (blk, blk), 1)
    causal = col < row
    tri = tri_ref[...]
    heads = [slice(h * dh, (h + 1) * dh) for h in range(n_h)]
    qs = [q_ref[0, :, lanes] for lanes in heads]

    acc_ref[...] = jnp.zeros(acc_ref.shape, F32)
    car_ref[...] = jnp.zeros(car_ref.shape, F32)

    def block(kj, diag, live=None):
        rows = pl.ds(pl.multiple_of(kj * blk, blk), blk)
        carry_min = None
        sps, log_betas = [], []
        for h, lanes in enumerate(heads):
            kb = k_ref[0, rows, lanes]
            z = lax.dot_general(qs[h], kb, (((1,), (1,)), ((), ())), preferred_element_type=F32)
            sp = jnp.maximum(z, 0.0) + _log1p_exp_neg_abs(z)
            log_betas.append(z - sp)
            if diag:
                sp = jnp.where(causal, sp, 0.0)
            sps.append(sp)
        inner_all = jnp.dot(jnp.concatenate([sp.astype(BF16) for sp in sps], axis=0), tri,
                            preferred_element_type=F32)
        for h, lanes in enumerate(heads):
            vb = v_ref[0, rows, lanes]
            sp, log_beta = sps[h], log_betas[h]
            inner = inner_all[h * blk:(h + 1) * blk, :]
            carry = car_ref[h]
            wgt = jnp.exp(log_beta - inner - carry)
            if diag:
                wgt = jnp.where(causal, wgt, 0.0)
            pv = jnp.dot(wgt.astype(BF16), vb, preferred_element_type=F32)
            if live is not None:
                pv = jnp.where(live, pv, 0.0)
            acc_ref[h] += pv
            carry = carry + jnp.sum(sp, axis=1, keepdims=True)
            car_ref[h] = carry
            c_min = jnp.min(carry)
            carry_min = c_min if carry_min is None else jnp.minimum(carry_min, c_min)
        return carry_min

    def cond(c):
        return jnp.logical_and(c[0] >= 0, c[1] < SB_ZERO_EXP)

    def body(c):
        return c[0] - 1, block(c[0], False)

    block(qi, True)
    first_min = block(jnp.maximum(qi - 1, 0), False, live=qi >= 1)
    lax.while_loop(cond, body, (qi - 2, first_min))
    for h, lanes in enumerate(heads):
        o_ref[0, :, lanes] = acc_ref[h].astype(o_ref.dtype)


def _stick_breaking(qkv, tri, mix):
    b, s, _ = qkv.shape
    blk = min(SB_BLK, s)
    dh = SB_HEAD_DIM
    width = SB_HEADS_PER_STEP * dh
    n_grp = mix // width
    return pl.pallas_call(
        functools.partial(_sb_kernel, blk=blk, dh=dh, n_h=SB_HEADS_PER_STEP),
        grid=(b, n_grp, s // blk),
        in_specs=[
            pl.BlockSpec((1, blk, width), lambda i, g, t: (i, t, g)),
            pl.BlockSpec((1, s, width), lambda i, g, t: (i, 0, n_grp + g)),
            pl.BlockSpec((1, s, width), lambda i, g, t: (i, 0, 2 * n_grp + g)),
            pl.BlockSpec((blk, blk), lambda i, g, t: (0, 0)),
        ],
        out_specs=pl.BlockSpec((1, blk, width), lambda i, g, t: (i, t, g)),
        out_shape=jax.ShapeDtypeStruct((b, s, mix), BF16),
        scratch_shapes=[
            pltpu.VMEM((SB_HEADS_PER_STEP, blk, dh), F32),
            pltpu.VMEM((SB_HEADS_PER_STEP, blk, 1), F32),
        ],
        compiler_params=_params("parallel", "parallel", "arbitrary"),
        name="stick_breaking",
    )(qkv, qkv, qkv, tri)


def _hgrn2_kernel(hn_ref, wp_ref, lbl_ref, ng_ref, sel_ref, o_ref,
                  st_ref, vg_ref, qkb_ref, diag_ref, *, layer, tb, chunk, sub):
    n_b = hn_ref.shape[0]
    mix = o_ref.shape[-1]
    hk = HG_HEAD_DIM
    n_head = mix // hk
    n_sub = chunk // sub
    half = sub // 2

    @pl.when(pl.program_id(0) == 0)
    def _():
        st_ref[...] = jnp.zeros(st_ref.shape, F32)

    lbl = lbl_ref[...]
    p = jnp.exp(lbl - jnp.max(lbl, axis=0, keepdims=True))
    p = p / jnp.sum(p, axis=0, keepdims=True)
    lb = jnp.zeros((1, mix), F32)
    for m in range(1, layer + 1):
        lb = lb + p[m:m + 1, :]
    log_lb = jnp.log(jnp.maximum(lb, LB_FLOOR))
    log_1m_lb = jnp.log1p(-lb)

    row = lax.broadcasted_iota(jnp.int32, (chunk, chunk), 0)
    col = lax.broadcasted_iota(jnp.int32, (chunk, chunk), 1)
    same_sub = (row // sub) == (col // sub)
    later_sub = (col // sub) > (row // sub)
    causal = row <= col
    trow = lax.broadcasted_iota(jnp.int32, (chunk, mix), 0)

    def stage1(c, slot):
        rows = pl.ds(pl.multiple_of(c * chunk, chunk), chunk)
        proj = jnp.dot(jnp.concatenate([hn_ref[bi, rows, :] for bi in range(n_b)], axis=0),
                       wp_ref[...], preferred_element_type=F32)
        per_b = []
        span = None
        for bi in range(n_b):
            brows = slice(bi * chunk, (bi + 1) * chunk)
            q_raw = proj[brows, 0:mix]
            f_pre = proj[brows, mix:2 * mix]
            vg_ref[bi, rows, :] = proj[brows, 2 * mix:4 * mix]
            q = q_raw * jax.nn.sigmoid(q_raw)
            t2 = log_1m_lb - (jnp.maximum(-f_pre, 0.0) + _log1p_exp_neg_abs(f_pre))
            log_f = jnp.maximum(log_lb, t2) + _log1p_exp_neg_abs(log_lb - t2)
            k = (1.0 - lb) * jax.nn.sigmoid(-f_pre)
            bcum = log_f
            d = 1
            while d < chunk:
                bcum = bcum + jnp.where(trow >= d, pltpu.roll(bcum, d, 0), 0.0)
                d *= 2
            qkb_ref[slot, bi, 0] = q
            qkb_ref[slot, bi, 1] = k
            qkb_ref[slot, bi, 2] = bcum
            b3 = bcum.reshape(n_sub, sub, mix)
            q3 = q.reshape(n_sub, sub, mix)
            k3 = k.reshape(n_sub, sub, mix)
            ref3 = jnp.concatenate([jnp.zeros((1, 1, mix), F32), b3[:n_sub - 1, sub - 1:sub, :]], axis=0)
            drop = jnp.max(ref3 - b3[:, sub - 1:sub, :])
            span = drop if span is None else jnp.maximum(span, drop)
            per_b.append((q3, k3, b3, ref3))

        def factored():
            for bi, (q3, k3, b3, ref3) in enumerate(per_b):
                q_s = (q3 * jnp.exp(b3 - ref3)).reshape(chunk, mix).astype(BF16)
                k_s = (k3 * jnp.exp(ref3 - b3)).reshape(chunk, mix).astype(BF16)
                for hd in range(n_head):
                    hl = slice(hd * hk, (hd + 1) * hk)
                    diag_ref[slot, bi, hd] = lax.dot_general(
                        k_s[:, hl], q_s[:, hl], (((1,), (1,)), ((), ())), preferred_element_type=F32)

        def pairwise():
            for bi, (q3, k3, b3, ref3) in enumerate(per_b):
                b3s = b3 * LOG2_E
                tiles = ([], [])
                for tl in range(sub):
                    b_t = b3s[:, tl:tl + 1, :]
                    q_t = q3[:, tl:tl + 1, :]
                    for hf in range(1 if tl < half else 2):
                        srows = slice(hf * half, (hf + 1) * half)
                        dec = jnp.exp2(jnp.minimum(b_t - b3s[:, srows, :], 0.0))
                        tile = (dec * (q_t * k3[:, srows, :])).reshape(n_sub * half, mix).astype(BF16)
                        tiles[hf].append(tile)
                halves = []
                for hf in range(2):
                    lhs = jnp.concatenate(
                        [jnp.concatenate([t[:, hd * hk:(hd + 1) * hk] for t in tiles[hf]], axis=1)
                         for hd in range(n_head)], axis=0)
                    halves.append(jnp.dot(lhs, sel_ref[hf * half * hk:, :], preferred_element_type=F32))
                for hd in range(n_head):
                    r0 = hd * n_sub * half
                    diag_ref[slot, bi, hd] = jnp.concatenate(
                        [halves[hf][r0 + j * half:r0 + (j + 1) * half, :]
                         for j in range(n_sub) for hf in range(2)], axis=0)

        lax.cond(span > HG_SAFE_SPAN, pairwise, factored)

    def stage2(c, slot):
        rows = pl.ds(pl.multiple_of(c * chunk, chunk), chunk)
        for bi in range(n_b):
            q = qkb_ref[slot, bi, 0]
            k = qkb_ref[slot, bi, 1]
            bcum = qkb_ref[slot, bi, 2]
            b_last = bcum[chunk - 1:chunk, :]
            vb = vg_ref[bi, rows, 0:mix].astype(BF16)
            q_in = (q * jnp.exp(bcum)).astype(BF16)
            k_st = (k * jnp.exp(b_last - bcum)).astype(BF16)
            decay_last = jnp.exp(b_last)
            g_raw = vg_ref[bi, rows, mix:2 * mix]
            for hd in range(n_head):
                hl = slice(hd * hk, (hd + 1) * hk)
                at = jnp.where(same_sub & causal, diag_ref[slot, bi, hd], 0.0)
                k_rows, q_cols = [], []
                for j in range(n_sub - 1):
                    r_j = bcum[(j + 1) * sub - 1:(j + 1) * sub, hl]
                    srows = slice(j * sub, (j + 1) * sub)
                    trows = slice((j + 1) * sub, chunk)
                    k_hat = (k[srows, hl] * jnp.exp(r_j - bcum[srows, hl])).astype(BF16)
                    q_j = (q[trows, hl] * jnp.exp(bcum[trows, hl] - r_j)).astype(BF16)
                    q_cols.append(jnp.concatenate([jnp.zeros(((j + 1) * sub, hk), BF16), q_j], axis=0))
                    k_rows.append(jnp.concatenate(
                        [k_hat if jj == j else jnp.zeros((sub, hk), BF16) for jj in range(n_sub - 1)],
                        axis=1))
                if k_rows:
                    k_rows.append(jnp.zeros((sub, (n_sub - 1) * hk), BF16))
                    a_off = lax.dot_general(jnp.concatenate(k_rows, axis=0),
                                            jnp.concatenate(q_cols, axis=1),
                                            (((1,), (1,)), ((), ())), preferred_element_type=F32)
                    at = jnp.where(later_sub, a_off, at)
                st = st_ref[bi * n_head + hd]
                o = lax.dot_general(at.astype(BF16), vb[:, hl], (((0,), (0,)), ((), ())),
                                    preferred_element_type=F32)
                o = o + lax.dot_general(q_in[:, hl], st.astype(BF16), (((1,), (1,)), ((), ())),
                                        preferred_element_type=F32)
                st_ref[bi * n_head + hd] = decay_last[:, hl] * st + lax.dot_general(
                    vb[:, hl], k_st[:, hl], (((0,), (0,)), ((), ())), preferred_element_type=F32)
                o = _rms_scale(o, ng_ref[...])
                g_h = g_raw[:, hl]
                o_ref[bi, rows, hl] = (o * (g_h * jax.nn.sigmoid(g_h))).astype(o_ref.dtype)

    n_chunk = tb // chunk
    stage1(0, 0)

    def body(c, _):
        stage2(c - 1, lax.rem(c - 1, 2))
        stage1(c, lax.rem(c, 2))
        return 0

    lax.fori_loop(1, n_chunk, body, 0)
    stage2(n_chunk - 1, (n_chunk - 1) % 2)


def _hgrn2(hn, w_bf, col0, lb_logits, norm_gain, sel, layer):
    b, s, d = hn.shape
    mix = lb_logits.shape[1]
    wcb = col0 // (4 * mix)
    tb = min(HG_TB, s)
    chunk, sub = HG_CHUNK, HG_SUB
    assert sub == 2 * SUBLANES and chunk % sub == 0 and tb % (2 * chunk) == 0
    n_head = mix // HG_HEAD_DIM
    depth = lb_logits.shape[0]
    return pl.pallas_call(
        functools.partial(_hgrn2_kernel, layer=layer, tb=tb, chunk=chunk, sub=sub),
        grid=(s // tb,),
        in_specs=[
            pl.BlockSpec((b, tb, d), lambda j: (0, j, 0)),
            pl.BlockSpec((d, 4 * mix), lambda j: (0, wcb)),
            pl.BlockSpec((depth, mix), lambda j: (0, 0)),
            pl.BlockSpec((1, HG_HEAD_DIM), lambda j: (0, 0)),
            pl.BlockSpec((sub * HG_HEAD_DIM, chunk), lambda j: (0, 0)),
        ],
        out_specs=pl.BlockSpec((b, tb, mix), lambda j: (0, j, 0)),
        out_shape=jax.ShapeDtypeStruct((b, s, mix), BF16),
        scratch_shapes=[
            pltpu.VMEM((b * n_head, HG_HEAD_DIM, HG_HEAD_DIM), F32),
            pltpu.VMEM((b, tb, 2 * mix), F32),
            pltpu.VMEM((2, b, 3, chunk, mix), F32),
            pltpu.VMEM((2, b, n_head, chunk, chunk), F32),
        ],
        compiler_params=_params("arbitrary"),
        name="hgrn2",
    )(hn, w_bf, lb_logits, norm_gain, sel)


def _merge_kernel(ya_ref, yb_ref, yc_ref, hn_ref, x_ref, wga_ref, wgb_ref, wgc_ref,
                  wbr_ref, wout_ref, gain_ref, o_ref):
    hn = hn_ref[...]
    merged = None
    for idx, (y_ref, wg_ref) in enumerate(((ya_ref, wga_ref), (yb_ref, wgb_ref), (yc_ref, wgc_ref))):
        gate = jnp.dot(hn, wg_ref[...], preferred_element_type=F32)
        part = jnp.dot(y_ref[...].astype(BF16), wbr_ref[idx], preferred_element_type=F32)
        part = jax.nn.sigmoid(gate) * part
        merged = part if merged is None else merged + part
    out = jnp.dot(merged.astype(BF16), wout_ref[...], preferred_element_type=F32)
    o_ref[...] = x_ref[...] + _rms_scale(out, gain_ref[...])


def _merge(ya, yb, yc, hn2, x2, w_bf, gate_col0, wbr_bf, wout_bf, gain):
    t, d = x2.shape
    mix = ya.shape[1]
    tm = min(MERGE_TM, t)
    gcb = gate_col0 // d
    ybs = lambda: pl.BlockSpec((tm, mix), lambda i: (i, 0))
    row = lambda: pl.BlockSpec((tm, d), lambda i: (i, 0))
    wgate = lambda off: pl.BlockSpec((d, d), lambda i: (0, gcb + off))
    return pl.pallas_call(
        _merge_kernel,
        grid=(t // tm,),
        in_specs=[
            ybs(), ybs(), ybs(), row(), row(),
            wgate(0), wgate(1), wgate(2),
            pl.BlockSpec(wbr_bf.shape, lambda i: (0, 0, 0)),
            pl.BlockSpec((d, d), lambda i: (0, 0)),
            pl.BlockSpec((1, d), lambda i: (0, 0)),
        ],
        out_specs=row(),
        out_shape=jax.ShapeDtypeStruct((t, d), F32),
        compiler_params=_params("parallel"),
        name="merge_out",
    )(ya, yb, yc, hn2, x2, w_bf, w_bf, w_bf, wbr_bf, wout_bf, gain)


def _mlp_kernel(x_ref, g_in_ref, wup_ref, wdn_ref, g_out_ref, o_ref):
    x = x_ref[...]
    h = _rms_scale(x, g_in_ref[...]).astype(BF16)
    u = jnp.maximum(jnp.dot(h, wup_ref[...], preferred_element_type=F32), 0.0)
    m = jnp.dot((u * u).astype(BF16), wdn_ref[...], preferred_element_type=F32)
    o_ref[...] = x + _rms_scale(m, g_out_ref[...])


def _mlp(x2, g_in, wup_bf, wdn_bf, g_out):
    t, d = x2.shape
    dff = wup_bf.shape[1]
    tm = min(MLP_TM, t)
    resident = dict(pipeline_mode=pl.Buffered(1))
    return pl.pallas_call(
        _mlp_kernel,
        grid=(t // tm,),
        in_specs=[
            pl.BlockSpec((tm, d), lambda i: (i, 0)),
            pl.BlockSpec((1, d), lambda i: (0, 0)),
            pl.BlockSpec((d, dff), lambda i: (0, 0), **resident),
            pl.BlockSpec((dff, d), lambda i: (0, 0), **resident),
            pl.BlockSpec((1, d), lambda i: (0, 0)),
        ],
        out_specs=pl.BlockSpec((tm, d), lambda i: (i, 0)),
        out_shape=jax.ShapeDtypeStruct((t, d), F32),
        compiler_params=_params("parallel"),
        name="mlp",
    )(x2, g_in, wup_bf, wdn_bf, g_out)


def _block_diag(w):
    n, c, _ = w.shape
    eye = jnp.eye(n, dtype=w.dtype)
    return (eye[:, None, :, None] * w[:, :, None, :]).reshape(n * c, n * c)


def kernel(x, ln_gains, w_in, conv_w, conv_b, rg_w_a, rg_b_a, rg_w_x, rg_b_x, rg_lambda,
           lb_logits, hgrn_norm, w_branch, w_out, w_up, w_down):
    b, s, d = x.shape
    depth = w_in.shape[0]
    mix = conv_w.shape[-1]
    t = b * s
    qkv_lo, hg_lo, gate_lo = mix, 4 * mix, 8 * mix

    blk = min(SB_BLK, s)
    tri = (lax.broadcasted_iota(jnp.int32, (blk, blk), 0)
           > lax.broadcasted_iota(jnp.int32, (blk, blk), 1)).astype(BF16)
    sel_r = lax.broadcasted_iota(jnp.int32, (HG_SUB * HG_HEAD_DIM, HG_CHUNK), 0) // HG_HEAD_DIM
    sel_c = lax.broadcasted_iota(jnp.int32, (HG_SUB * HG_HEAD_DIM, HG_CHUNK), 1) % HG_SUB
    sel = (sel_r == sel_c).astype(BF16)

    x2 = x.reshape(t, d)
    for l in range(depth):
        g = ln_gains[l]
        w_l = w_in[l].astype(BF16)
        qkv, hn2 = _in_proj(x2, g[0:1], w_l, mix, qkv_lo, SB_HEAD_DIM ** -0.5)
        hn = hn2.reshape(b, s, d)
        y_a = _rglru(hn, w_l, 0, conv_w[l], conv_b[l:l + 1],
                     _block_diag(rg_w_a[l]).astype(BF16), rg_b_a[l:l + 1],
                     _block_diag(rg_w_x[l]).astype(BF16), rg_b_x[l:l + 1],
                     rg_lambda[l:l + 1])
        y_b = _stick_breaking(qkv.reshape(b, s, -1), tri, mix)
        y_c = _hgrn2(hn, w_l, hg_lo, lb_logits, hgrn_norm[l:l + 1], sel, l)
        x2 = _merge(y_a.reshape(t, mix), y_b.reshape(t, mix), y_c.reshape(t, mix), hn2, x2,
                    w_l, gate_lo, w_branch[l].astype(BF16), w_out[l].astype(BF16), g[1:2])
        x2 = _mlp(x2, g[2:3], w_up[l].astype(BF16), w_down[l].astype(BF16), g[3:4])
    return x2.reshape(b, s, d)
```

```python
import functools

import jax
import jax.numpy as jnp
from jax import lax
from jax.experimental import pallas as pl
from jax.experimental.pallas import tpu as pltpu

F32 = jnp.float32
BF16 = jnp.bfloat16

EPS = 1e-6
RG_C = 8.0
CONV_WIDTH = 4
SB_HEAD_DIM = 64
HG_HEAD_DIM = 128
LB_FLOOR = 1e-30
LOG2_E = 1.4426950408889634
HG_SAFE_SPAN = 60.0
SB_ZERO_EXP = 105.0

LANES = 128
SUBLANES = 8
VMEM_LIMIT = 56 * 1024 * 1024

IN_TM = 512
RG_TB = 512
SB_BLK = 256
SB_HEADS_PER_STEP = 8
HG_TB = 512
HG_CHUNK = 64
HG_SUB = 16
MERGE_TM = 512
MLP_TM = 512


def _params(*sem):
    return pltpu.CompilerParams(dimension_semantics=sem, vmem_limit_bytes=VMEM_LIMIT)


def _softplus(x):
    return jnp.maximum(x, 0.0) + jnp.log1p(jnp.exp(-jnp.abs(x)))


def _log1p_exp_neg_abs(x):
    return jnp.log(1.0 + jnp.exp2(jnp.abs(x) * (-LOG2_E)))


def _rms_scale(x, gain):
    inv = lax.rsqrt(jnp.mean(x * x, axis=-1, keepdims=True) + EPS)
    return x * inv * gain


def _in_proj_kernel(x_ref, g_ref, wq_ref, wk_ref, wv_ref, qkv_ref, h_ref, *, q_scale):
    h = _rms_scale(x_ref[...], g_ref[...]).astype(BF16)
    h_ref[...] = h
    w = wq_ref.shape[-1]
    for idx, (w_ref, scale) in enumerate(((wq_ref, q_scale), (wk_ref, 1.0), (wv_ref, 1.0))):
        out = jnp.dot(h, w_ref[...], preferred_element_type=F32)
        if scale != 1.0:
            out = out * scale
        qkv_ref[:, idx * w:(idx + 1) * w] = out.astype(BF16)


def _in_proj(x2, gain, w_bf, mix, q_col0, q_scale):
    t, d = x2.shape
    tm = min(IN_TM, t)
    cb = q_col0 // mix
    wcol = lambda off: pl.BlockSpec((d, mix), lambda i: (0, cb + off))
    return pl.pallas_call(
        functools.partial(_in_proj_kernel, q_scale=q_scale),
        grid=(t // tm,),
        in_specs=[
            pl.BlockSpec((tm, d), lambda i: (i, 0)),
            pl.BlockSpec((1, d), lambda i: (0, 0)),
            wcol(0), wcol(1), wcol(2),
        ],
        out_specs=[pl.BlockSpec((tm, 3 * mix), lambda i: (i, 0)),
                   pl.BlockSpec((tm, d), lambda i: (i, 0))],
        out_shape=[jax.ShapeDtypeStruct((t, 3 * mix), BF16), jax.ShapeDtypeStruct((t, d), BF16)],
        compiler_params=_params("parallel"),
        name="in_proj",
    )(x2, gain, w_bf, w_bf, w_bf)


def _rglru_kernel(hn_ref, wp_ref, cw_ref, cb_ref, wa_ref, ba_ref, wx_ref, bx_ref, lam_ref,
                  o_ref, tail_ref, a_ref, u_ref, h_ref, *, tb):
    j = pl.program_id(0)
    n_b = hn_ref.shape[0]
    w = wp_ref.shape[-1]

    @pl.when(j == 0)
    def _():
        tail_ref[...] = jnp.zeros(tail_ref.shape, F32)
        h_ref[...] = jnp.zeros(h_ref.shape, F32)

    sp_lam = _softplus(-lam_ref[...])
    row = lax.broadcasted_iota(jnp.int32, (tb, w), 0) + j * tb
    for bi in range(n_b):
        x = jnp.dot(hn_ref[bi], wp_ref[...], preferred_element_type=F32)
        xp = jnp.concatenate([tail_ref[bi], x], axis=0)
        acc = cw_ref[0:1, :] * xp
        for k in range(1, CONV_WIDTH):
            acc = pltpu.roll(acc, 1, 0) + cw_ref[k:k + 1, :] * xp
        xc = acc[SUBLANES:, :] + cb_ref[...]
        tail_ref[bi] = x[tb - SUBLANES:tb, :]

        xcb = xc.astype(BF16)
        r = jax.nn.sigmoid(jnp.dot(xcb, wa_ref[...], preferred_element_type=F32) + ba_ref[...])
        gate_i = jax.nn.sigmoid(jnp.dot(xcb, wx_ref[...], preferred_element_type=F32) + bx_ref[...])
        a = jnp.exp(-RG_C * r * sp_lam)
        mult = jnp.sqrt(jnp.maximum(1.0 - a * a, 0.0))
        mult = jnp.where(row == 0, 1.0, mult)
        a_ref[bi] = a
        u_ref[bi] = mult * (gate_i * xc)

    def step(t, hs):
        out = []
        for bi in range(n_b):
            h = a_ref[bi, pl.ds(t, 1), :] * hs[bi] + u_ref[bi, pl.ds(t, 1), :]
            o_ref[bi, pl.ds(t, 1), :] = h
            out.append(h)
        return tuple(out)

    hs = lax.fori_loop(0, tb, step, tuple(h_ref[bi] for bi in range(n_b)), unroll=SUBLANES)
    for bi in range(n_b):
        h_ref[bi] = hs[bi]


def _rglru(hn, w_bf, col0, cw, cb, wa_bd, ba, wx_bd, bx, lam):
    b, s, d = hn.shape
    mix = cw.shape[-1]
    wcb = col0 // mix
    tb = min(RG_TB, s)
    vec = lambda: pl.BlockSpec((1, mix), lambda j: (0, 0))
    return pl.pallas_call(
        functools.partial(_rglru_kernel, tb=tb),
        grid=(s // tb,),
        in_specs=[
            pl.BlockSpec((b, tb, d), lambda j: (0, j, 0)),
            pl.BlockSpec((d, mix), lambda j: (0, wcb)),
            pl.BlockSpec((CONV_WIDTH, mix), lambda j: (0, 0)),
            vec(),
            pl.BlockSpec((mix, mix), lambda j: (0, 0)),
            vec(),
            pl.BlockSpec((mix, mix), lambda j: (0, 0)),
            vec(),
            vec(),
        ],
        out_specs=pl.BlockSpec((b, tb, mix), lambda j: (0, j, 0)),
        out_shape=jax.ShapeDtypeStruct((b, s, mix), F32),
        scratch_shapes=[
            pltpu.VMEM((b, SUBLANES, mix), F32),
            pltpu.VMEM((b, tb, mix), F32),
            pltpu.VMEM((b, tb, mix), F32),
            pltpu.VMEM((b, 1, mix), F32),
        ],
        compiler_params=_params("arbitrary"),
        name="rglru",
    )(hn, w_bf, cw, cb, wa_bd, ba, wx_bd, bx, lam)


def _sb_kernel(q_ref, k_ref, v_ref, tri_ref, o_ref, acc_ref, car_ref, *, blk, dh, n_h):
    qi = pl.program_id(2)
    row = lax.broadcasted_iota(jnp.int32, (blk, blk), 0)
    col = lax.broadcasted_iota(jnp.int32, (blk, blk), 1)
    causal = col < row
    tri = tri_ref[...]
    heads = [slice(h * dh, (h + 1) * dh) for h in range(n_h)]
    qs = [q_ref[0, :, lanes] for lanes in heads]

    acc_ref[...] = jnp.zeros(acc_ref.shape, F32)
    car_ref[...] = jnp.zeros(car_ref.shape, F32)

    def block(kj, diag, live=None):
        rows = pl.ds(pl.multiple_of(kj * blk, blk), blk)
        carry_min = None
        sps, log_betas = [], []
        for h, lanes in enumerate(heads):
            kb = k_ref[0, rows, lanes]
            z = lax.dot_general(qs[h], kb, (((1,), (1,)), ((), ())), preferred_element_type=F32)
            sp = jnp.maximum(z, 0.0) + _log1p_exp_neg_abs(z)
            log_betas.append(z - sp)
            if diag:
                sp = jnp.where(causal, sp, 0.0)
            sps.append(sp)
        inner_all = jnp.dot(jnp.concatenate([sp.astype(BF16) for sp in sps], axis=0), tri,
                            preferred_element_type=F32)
        for h, lanes in enumerate(heads):
            vb = v_ref[0, rows, lanes]
            sp, log_beta = sps[h], log_betas[h]
            inner = inner_all[h * blk:(h + 1) * blk, :]
            carry = car_ref[h]
            wgt = jnp.exp(log_beta - inner - carry)
            if diag:
                wgt = jnp.where(causal, wgt, 0.0)
            pv = jnp.dot(wgt.astype(BF16), vb, preferred_element_type=F32)
            if live is not None:
                pv = jnp.where(live, pv, 0.0)
            acc_ref[h] += pv
            carry = carry + jnp.sum(sp, axis=1, keepdims=True)
            car_ref[h] = carry
            carry_min = carry if carry_min is None else jnp.minimum(carry_min, carry)
        return jnp.min(carry_min)

    def cond(c):
        return jnp.logical_and(c[0] >= 0, c[1] < SB_ZERO_EXP)

    def body(c):
        return c[0] - 1, block(c[0], False)

    block(qi, True)
    first_min = block(jnp.maximum(qi - 1, 0), False, live=qi >= 1)
    lax.while_loop(cond, body, (qi - 2, first_min))
    for h, lanes in enumerate(heads):
        o_ref[0, :, lanes] = acc_ref[h].astype(o_ref.dtype)


def _stick_breaking(qkv, tri, mix):
    b, s, _ = qkv.shape
    blk = min(SB_BLK, s)
    dh = SB_HEAD_DIM
    width = SB_HEADS_PER_STEP * dh
    n_grp = mix // width
    return pl.pallas_call(
        functools.partial(_sb_kernel, blk=blk, dh=dh, n_h=SB_HEADS_PER_STEP),
        grid=(b, n_grp, s // blk),
        in_specs=[
            pl.BlockSpec((1, blk, width), lambda i, g, t: (i, t, g)),
            pl.BlockSpec((1, s, width), lambda i, g, t: (i, 0, n_grp + g)),
            pl.BlockSpec((1, s, width), lambda i, g, t: (i, 0, 2 * n_grp + g)),
            pl.BlockSpec((blk, blk), lambda i, g, t: (0, 0)),
        ],
        out_specs=pl.BlockSpec((1, blk, width), lambda i, g, t: (i, t, g)),
        out_shape=jax.ShapeDtypeStruct((b, s, mix), BF16),
        scratch_shapes=[
            pltpu.VMEM((SB_HEADS_PER_STEP, blk, dh), F32),
            pltpu.VMEM((SB_HEADS_PER_STEP, blk, 1), F32),
        ],
        compiler_params=_params("parallel", "parallel", "arbitrary"),
        name="stick_breaking",
    )(qkv, qkv, qkv, tri)


def _hgrn2_kernel(hn_ref, wp_ref, lbl_ref, ng_ref, sel_ref, o_ref,
                  st_ref, vg_ref, qkb_ref, diag_ref, *, layer, tb, chunk, sub):
    n_b = hn_ref.shape[0]
    mix = o_ref.shape[-1]
    hk = HG_HEAD_DIM
    n_head = mix // hk
    n_sub = chunk // sub
    grp = SUBLANES
    n_grp = sub // grp

    @pl.when(pl.program_id(0) == 0)
    def _():
        st_ref[...] = jnp.zeros(st_ref.shape, F32)

    lbl = lbl_ref[...]
    p = jnp.exp(lbl - jnp.max(lbl, axis=0, keepdims=True))
    p = p / jnp.sum(p, axis=0, keepdims=True)
    lb = jnp.zeros((1, mix), F32)
    for m in range(1, layer + 1):
        lb = lb + p[m:m + 1, :]
    log_lb = jnp.log(jnp.maximum(lb, LB_FLOOR))
    log_1m_lb = jnp.log1p(-lb)

    row = lax.broadcasted_iota(jnp.int32, (chunk, chunk), 0)
    col = lax.broadcasted_iota(jnp.int32, (chunk, chunk), 1)
    same_sub = (row // sub) == (col // sub)
    later_sub = (col // sub) > (row // sub)
    causal = row <= col
    trow = lax.broadcasted_iota(jnp.int32, (chunk, mix), 0)

    def stage1(c, slot):
        rows = pl.ds(pl.multiple_of(c * chunk, chunk), chunk)
        proj = jnp.dot(jnp.concatenate([hn_ref[bi, rows, :] for bi in range(n_b)], axis=0),
                       wp_ref[...], preferred_element_type=F32)
        per_b = []
        span = None
        for bi in range(n_b):
            brows = slice(bi * chunk, (bi + 1) * chunk)
            q_raw = proj[brows, 0:mix]
            f_pre = proj[brows, mix:2 * mix]
            vg_ref[bi, rows, :] = proj[brows, 2 * mix:4 * mix]
            q = q_raw * jax.nn.sigmoid(q_raw)
            t2 = log_1m_lb - (jnp.maximum(-f_pre, 0.0) + _log1p_exp_neg_abs(f_pre))
            log_f = jnp.maximum(log_lb, t2) + _log1p_exp_neg_abs(log_lb - t2)
            k = (1.0 - lb) * jax.nn.sigmoid(-f_pre)
            bcum = log_f
            d = 1
            while d < chunk:
                bcum = bcum + jnp.where(trow >= d, pltpu.roll(bcum, d, 0), 0.0)
                d *= 2
            qkb_ref[slot, bi, 0] = q
            qkb_ref[slot, bi, 1] = k
            qkb_ref[slot, bi, 2] = bcum
            b3 = bcum.reshape(n_sub, sub, mix)
            q3 = q.reshape(n_sub, sub, mix)
            k3 = k.reshape(n_sub, sub, mix)
            ref3 = jnp.concatenate([jnp.zeros((1, 1, mix), F32), b3[:n_sub - 1, sub - 1:sub, :]], axis=0)
            drop = jnp.max(ref3 - b3[:, sub - 1:sub, :])
            span = drop if span is None else jnp.maximum(span, drop)
            per_b.append((q3, k3, b3, ref3))

        def factored():
            for bi, (q3, k3, b3, ref3) in enumerate(per_b):
                q_s = (q3 * jnp.exp(b3 - ref3)).reshape(chunk, mix).astype(BF16)
                k_s = (k3 * jnp.exp(ref3 - b3)).reshape(chunk, mix).astype(BF16)
                for hd in range(n_head):
                    hl = slice(hd * hk, (hd + 1) * hk)
                    diag_ref[slot, bi, hd] = lax.dot_general(
                        k_s[:, hl], q_s[:, hl], (((1,), (1,)), ((), ())), preferred_element_type=F32)

        def pairwise():
            for bi, (q3, k3, b3, ref3) in enumerate(per_b):
                b3s = b3 * LOG2_E
                tiles = tuple([] for _ in range(n_grp))
                for tl in range(sub):
                    b_t = b3s[:, tl:tl + 1, :]
                    q_t = q3[:, tl:tl + 1, :]
                    for gi in range(tl // grp + 1):
                        srows = slice(gi * grp, (gi + 1) * grp)
                        dec = jnp.exp2(jnp.minimum(b_t - b3s[:, srows, :], 0.0))
                        tile = (dec * (q_t * k3[:, srows, :])).reshape(n_sub * grp, mix).astype(BF16)
                        tiles[gi].append(tile)
                sums = []
                for gi in range(n_grp):
                    lhs = jnp.concatenate(
                        [jnp.concatenate([t[:, hd * hk:(hd + 1) * hk] for t in tiles[gi]], axis=1)
                         for hd in range(n_head)], axis=0)
                    sums.append(jnp.dot(lhs, sel_ref[gi * grp * hk:, :], preferred_element_type=F32))
                for hd in range(n_head):
                    r0 = hd * n_sub * grp
                    diag_ref[slot, bi, hd] = jnp.concatenate(
                        [sums[gi][r0 + j * grp:r0 + (j + 1) * grp, :]
                         for j in range(n_sub) for gi in range(n_grp)], axis=0)

        factored()
        pl.when(span > HG_SAFE_SPAN)(pairwise)

    def stage2(c, slot):
        rows = pl.ds(pl.multiple_of(c * chunk, chunk), chunk)
        for bi in range(n_b):
            q = qkb_ref[slot, bi, 0]
            k = qkb_ref[slot, bi, 1]
            bcum = qkb_ref[slot, bi, 2]
            b_last = bcum[chunk - 1:chunk, :]
            vb = vg_ref[bi, rows, 0:mix].astype(BF16)
            q_in = (q * jnp.exp(bcum)).astype(BF16)
            k_st = (k * jnp.exp(b_last - bcum)).astype(BF16)
            decay_last = jnp.exp(b_last)
            g_raw = vg_ref[bi, rows, mix:2 * mix]
            for hd in range(n_head):
                hl = slice(hd * hk, (hd + 1) * hk)
                at = jnp.where(same_sub & causal, diag_ref[slot, bi, hd], 0.0)
                k_rows, q_cols = [], []
                for j in range(n_sub - 1):
                    r_j = bcum[(j + 1) * sub - 1:(j + 1) * sub, hl]
                    srows = slice(j * sub, (j + 1) * sub)
                    trows = slice((j + 1) * sub, chunk)
                    k_hat = (k[srows, hl] * jnp.exp(r_j - bcum[srows, hl])).astype(BF16)
                    q_j = (q[trows, hl] * jnp.exp(bcum[trows, hl] - r_j)).astype(BF16)
                    q_cols.append(jnp.concatenate([jnp.zeros(((j + 1) * sub, hk), BF16), q_j], axis=0))
                    k_rows.append(jnp.concatenate(
                        [k_hat if jj == j else jnp.zeros((sub, hk), BF16) for jj in range(n_sub - 1)],
                        axis=1))
                if k_rows:
                    k_rows.append(jnp.zeros((sub, (n_sub - 1) * hk), BF16))
                    a_off = lax.dot_general(jnp.concatenate(k_rows, axis=0),
                                            jnp.concatenate(q_cols, axis=1),
                                            (((1,), (1,)), ((), ())), preferred_element_type=F32)
                    at = jnp.where(later_sub, a_off, at)
                st = st_ref[bi * n_head + hd]
                o = lax.dot_general(at.astype(BF16), vb[:, hl], (((0,), (0,)), ((), ())),
                                    preferred_element_type=F32)
                o = o + lax.dot_general(q_in[:, hl], st.astype(BF16), (((1,), (1,)), ((), ())),
                                        preferred_element_type=F32)
                st_ref[bi * n_head + hd] = decay_last[:, hl] * st + lax.dot_general(
                    vb[:, hl], k_st[:, hl], (((0,), (0,)), ((), ())), preferred_element_type=F32)
                o = _rms_scale(o, ng_ref[...])
                g_h = g_raw[:, hl]
                o_ref[bi, rows, hl] = (o * (g_h * jax.nn.sigmoid(g_h))).astype(o_ref.dtype)

    n_chunk = tb // chunk
    stage1(0, 0)

    def body(c, _):
        stage2(c - 1, lax.rem(c - 1, 2))
        stage1(c, lax.rem(c, 2))
        return 0

    lax.fori_loop(1, n_chunk, body, 0)
    stage2(n_chunk - 1, (n_chunk - 1) % 2)


def _hgrn2(hn, w_bf, col0, lb_logits, norm_gain, sel, layer):
    b, s, d = hn.shape
    mix = lb_logits.shape[1]
    wcb = col0 // (4 * mix)
    tb = min(HG_TB, s)
    chunk, sub = HG_CHUNK, HG_SUB
    assert sub % (2 * SUBLANES) == 0 and chunk % sub == 0 and tb % chunk == 0
    n_head = mix // HG_HEAD_DIM
    depth = lb_logits.shape[0]
    return pl.pallas_call(
        functools.partial(_hgrn2_kernel, layer=layer, tb=tb, chunk=chunk, sub=sub),
        grid=(s // tb,),
        in_specs=[
            pl.BlockSpec((b, tb, d), lambda j: (0, j, 0)),
            pl.BlockSpec((d, 4 * mix), lambda j: (0, wcb)),
            pl.BlockSpec((depth, mix), lambda j: (0, 0)),
            pl.BlockSpec((1, HG_HEAD_DIM), lambda j: (0, 0)),
            pl.BlockSpec((sub * HG_HEAD_DIM, chunk), lambda j: (0, 0)),
        ],
        out_specs=pl.BlockSpec((b, tb, mix), lambda j: (0, j, 0)),
        out_shape=jax.ShapeDtypeStruct((b, s, mix), BF16),
        scratch_shapes=[
            pltpu.VMEM((b * n_head, HG_HEAD_DIM, HG_HEAD_DIM), F32),
            pltpu.VMEM((b, tb, 2 * mix), F32),
            pltpu.VMEM((2, b, 3, chunk, mix), F32),
            pltpu.VMEM((2, b, n_head, chunk, chunk), F32),
        ],
        compiler_params=_params("arbitrary"),
        name="hgrn2",
    )(hn, w_bf, lb_logits, norm_gain, sel)


def _merge_kernel(ya_ref, yb_ref, yc_ref, hn_ref, x_ref, wga_ref, wgb_ref, wgc_ref,
                  wbr_ref, wout_ref, gain_ref, o_ref):
    hn = hn_ref[...]
    merged = None
    for idx, (y_ref, wg_ref) in enumerate(((ya_ref, wga_ref), (yb_ref, wgb_ref), (yc_ref, wgc_ref))):
        gate = jnp.dot(hn, wg_ref[...], preferred_element_type=F32)
        part = jnp.dot(y_ref[...].astype(BF16), wbr_ref[idx], preferred_element_type=F32)
        part = jax.nn.sigmoid(gate) * part
        merged = part if merged is None else merged + part
    out = jnp.dot(merged.astype(BF16), wout_ref[...], preferred_element_type=F32)
    o_ref[...] = x_ref[...] + _rms_scale(out, gain_ref[...])


def _merge(ya, yb, yc, hn2, x2, w_bf, gate_col0, wbr_bf, wout_bf, gain):
    t, d = x2.shape
    mix = ya.shape[1]
    tm = min(MERGE_TM, t)
    gcb = gate_col0 // d
    ybs = lambda: pl.BlockSpec((tm, mix), lambda i: (i, 0))
    row = lambda: pl.BlockSpec((tm, d), lambda i: (i, 0))
    wgate = lambda off: pl.BlockSpec((d, d), lambda i: (0, gcb + off))
    return pl.pallas_call(
        _merge_kernel,
        grid=(t // tm,),
        in_specs=[
            ybs(), ybs(), ybs(), row(), row(),
            wgate(0), wgate(1), wgate(2),
            pl.BlockSpec(wbr_bf.shape, lambda i: (0, 0, 0)),
            pl.BlockSpec((d, d), lambda i: (0, 0)),
            pl.BlockSpec((1, d), lambda i: (0, 0)),
        ],
        out_specs=row(),
        out_shape=jax.ShapeDtypeStruct((t, d), F32),
        compiler_params=_params("parallel"),
        name="merge_out",
    )(ya, yb, yc, hn2, x2, w_bf, w_bf, w_bf, wbr_bf, wout_bf, gain)


def _mlp_kernel(x_ref, g_in_ref, wup_ref, wdn_ref, g_out_ref, o_ref):
    x = x_ref[...]
    h = _rms_scale(x, g_in_ref[...]).astype(BF16)
    u = jnp.maximum(jnp.dot(h, wup_ref[...], preferred_element_type=F32), 0.0)
    m = jnp.dot((u * u).astype(BF16), wdn_ref[...], preferred_element_type=F32)
    o_ref[...] = x + _rms_scale(m, g_out_ref[...])


def _mlp(x2, g_in, wup_bf, wdn_bf, g_out):
    t, d = x2.shape
    dff = wup_bf.shape[1]
    tm = min(MLP_TM, t)
    resident = dict(pipeline_mode=pl.Buffered(1))
    return pl.pallas_call(
        _mlp_kernel,
        grid=(t // tm,),
        in_specs=[
            pl.BlockSpec((tm, d), lambda i: (i, 0)),
            pl.BlockSpec((1, d), lambda i: (0, 0)),
            pl.BlockSpec((d, dff), lambda i: (0, 0), **resident),
            pl.BlockSpec((dff, d), lambda i: (0, 0), **resident),
            pl.BlockSpec((1, d), lambda i: (0, 0)),
        ],
        out_specs=pl.BlockSpec((tm, d), lambda i: (i, 0)),
        out_shape=jax.ShapeDtypeStruct((t, d), F32),
        compiler_params=_params("parallel"),
        name="mlp",
    )(x2, g_in, wup_bf, wdn_bf, g_out)


def _block_diag(w):
    n, c, _ = w.shape
    eye = jnp.eye(n, dtype=w.dtype)
    return (eye[:, None, :, None] * w[:, :, None, :]).reshape(n * c, n * c)


def kernel(x, ln_gains, w_in, conv_w, conv_b, rg_w_a, rg_b_a, rg_w_x, rg_b_x, rg_lambda,
           lb_logits, hgrn_norm, w_branch, w_out, w_up, w_down):
    b, s, d = x.shape
    depth = w_in.shape[0]
    mix = conv_w.shape[-1]
    t = b * s
    qkv_lo, hg_lo, gate_lo = mix, 4 * mix, 8 * mix

    blk = min(SB_BLK, s)
    tri = (lax.broadcasted_iota(jnp.int32, (blk, blk), 0)
           > lax.broadcasted_iota(jnp.int32, (blk, blk), 1)).astype(BF16)
    sel_r = lax.broadcasted_iota(jnp.int32, (HG_SUB * HG_HEAD_DIM, HG_CHUNK), 0) // HG_HEAD_DIM
    sel_c = lax.broadcasted_iota(jnp.int32, (HG_SUB * HG_HEAD_DIM, HG_CHUNK), 1) % HG_SUB
    sel = (sel_r == sel_c).astype(BF16)

    x2 = x.reshape(t, d)
    for l in range(depth):
        g = ln_gains[l]
        w_l = w_in[l].astype(BF16)
        qkv, hn2 = _in_proj(x2, g[0:1], w_l, mix, qkv_lo, SB_HEAD_DIM ** -0.5)
        hn = hn2.reshape(b, s, d)
        y_a = _rglru(hn, w_l, 0, conv_w[l], conv_b[l:l + 1],
                     _block_diag(rg_w_a[l]).astype(BF16), rg_b_a[l:l + 1],
                     _block_diag(rg_w_x[l]).astype(BF16), rg_b_x[l:l + 1],
                     rg_lambda[l:l + 1])
        y_b = _stick_breaking(qkv.reshape(b, s, -1), tri, mix)
        y_c = _hgrn2(hn, w_l, hg_lo, lb_logits, hgrn_norm[l:l + 1], sel, l)
        x2 = _merge(y_a.reshape(t, mix), y_b.reshape(t, mix), y_c.reshape(t, mix), hn2, x2,
                    w_l, gate_lo, w_branch[l].astype(BF16), w_out[l].astype(BF16), g[1:2])
        x2 = _mlp(x2, g[2:3], w_up[l].astype(BF16), w_down[l].astype(BF16), g[3:4])
    return x2.reshape(b, s, d)
```

```python
import functools

import jax
import jax.numpy as jnp
from jax import lax
from jax.experimental import pallas as pl
from jax.experimental.pallas import tpu as pltpu

F32 = jnp.float32
BF16 = jnp.bfloat16

EPS = 1e-6
RG_C = 8.0
CONV_WIDTH = 4
SB_HEAD_DIM = 64
HG_HEAD_DIM = 128
LB_FLOOR = 1e-30
LOG2_E = 1.4426950408889634
HG_SAFE_SPAN = 60.0
SB_ZERO_EXP = 105.0

SUBLANES = 8
VMEM_LIMIT = 56 * 1024 * 1024

IN_TM = 512
RG_TB = 512
SB_BLK = 256
SB_HEADS_PER_STEP = 8
HG_TB = 512
HG_CHUNK = 64
HG_SUB = 16
MERGE_TM = 512
MLP_TM = 512


def _params(*sem):
    return pltpu.CompilerParams(dimension_semantics=sem, vmem_limit_bytes=VMEM_LIMIT)


def _softplus(x):
    return jnp.maximum(x, 0.0) + jnp.log1p(jnp.exp(-jnp.abs(x)))


def _log1p_exp_neg_abs(x):
    return jnp.log(1.0 + jnp.exp2(jnp.abs(x) * (-LOG2_E)))


def _rms_scale(x, gain):
    inv = lax.rsqrt(jnp.mean(x * x, axis=-1, keepdims=True) + EPS)
    return x * inv * gain


def _in_proj_kernel(x_ref, g_ref, wq_ref, wk_ref, wv_ref, qkv_ref, h_ref, *, q_scale):
    h = _rms_scale(x_ref[...], g_ref[...]).astype(BF16)
    h_ref[...] = h
    w = wq_ref.shape[-1]
    for idx, (w_ref, scale) in enumerate(((wq_ref, q_scale), (wk_ref, 1.0), (wv_ref, 1.0))):
        out = jnp.dot(h, w_ref[...], preferred_element_type=F32)
        if scale != 1.0:
            out = out * scale
        qkv_ref[:, idx * w:(idx + 1) * w] = out.astype(BF16)


def _in_proj(x2, gain, w_bf, layer, mix, q_col0, q_scale):
    t, d = x2.shape
    tm = min(IN_TM, t)
    cb = q_col0 // mix
    wcol = lambda off: pl.BlockSpec((None, d, mix), lambda i: (layer, 0, cb + off))
    return pl.pallas_call(
        functools.partial(_in_proj_kernel, q_scale=q_scale),
        grid=(t // tm,),
        in_specs=[
            pl.BlockSpec((tm, d), lambda i: (i, 0)),
            pl.BlockSpec((1, d), lambda i: (0, 0)),
            wcol(0), wcol(1), wcol(2),
        ],
        out_specs=[pl.BlockSpec((tm, 3 * mix), lambda i: (i, 0)),
                   pl.BlockSpec((tm, d), lambda i: (i, 0))],
        out_shape=[jax.ShapeDtypeStruct((t, 3 * mix), BF16), jax.ShapeDtypeStruct((t, d), BF16)],
        compiler_params=_params("parallel"),
        name="in_proj",
    )(x2, gain, w_bf, w_bf, w_bf)


def _rglru_kernel(hn_ref, wp_ref, cw_ref, cb_ref, wa_ref, ba_ref, wx_ref, bx_ref, lam_ref,
                  o_ref, tail_ref, a_ref, u_ref, h_ref, *, tb):
    j = pl.program_id(0)
    n_b = hn_ref.shape[0]
    w = wp_ref.shape[-1]

    @pl.when(j == 0)
    def _():
        tail_ref[...] = jnp.zeros(tail_ref.shape, F32)
        h_ref[...] = jnp.zeros(h_ref.shape, F32)

    sp_lam = _softplus(-lam_ref[...])
    row = lax.broadcasted_iota(jnp.int32, (tb, w), 0) + j * tb
    for bi in range(n_b):
        x = jnp.dot(hn_ref[bi], wp_ref[...], preferred_element_type=F32)
        xp = jnp.concatenate([tail_ref[bi], x], axis=0)
        acc = cw_ref[0:1, :] * xp
        for k in range(1, CONV_WIDTH):
            acc = pltpu.roll(acc, 1, 0) + cw_ref[k:k + 1, :] * xp
        xc = acc[SUBLANES:, :] + cb_ref[...]
        tail_ref[bi] = x[tb - SUBLANES:tb, :]

        xcb = xc.astype(BF16)
        r = jax.nn.sigmoid(jnp.dot(xcb, wa_ref[...], preferred_element_type=F32) + ba_ref[...])
        gate_i = jax.nn.sigmoid(jnp.dot(xcb, wx_ref[...], preferred_element_type=F32) + bx_ref[...])
        a = jnp.exp(-RG_C * r * sp_lam)
        mult = jnp.sqrt(jnp.maximum(1.0 - a * a, 0.0))
        mult = jnp.where(row == 0, 1.0, mult)
        a_ref[bi] = a
        u_ref[bi] = mult * (gate_i * xc)

    def step(t, hs):
        out = []
        for bi in range(n_b):
            h = a_ref[bi, pl.ds(t, 1), :] * hs[bi] + u_ref[bi, pl.ds(t, 1), :]
            o_ref[bi, pl.ds(t, 1), :] = h
            out.append(h)
        return tuple(out)

    hs = lax.fori_loop(0, tb, step, tuple(h_ref[bi] for bi in range(n_b)), unroll=SUBLANES)
    for bi in range(n_b):
        h_ref[bi] = hs[bi]


def _rglru(hn, w_bf, layer, col0, cw, cb, wa_bd, ba, wx_bd, bx, lam):
    b, s, d = hn.shape
    mix = cw.shape[-1]
    wcb = col0 // mix
    tb = min(RG_TB, s)
    vec = lambda: pl.BlockSpec((1, mix), lambda j: (0, 0))
    return pl.pallas_call(
        functools.partial(_rglru_kernel, tb=tb),
        grid=(s // tb,),
        in_specs=[
            pl.BlockSpec((b, tb, d), lambda j: (0, j, 0)),
            pl.BlockSpec((None, d, mix), lambda j: (layer, 0, wcb)),
            pl.BlockSpec((CONV_WIDTH, mix), lambda j: (0, 0)),
            vec(),
            pl.BlockSpec((mix, mix), lambda j: (0, 0)),
            vec(),
            pl.BlockSpec((mix, mix), lambda j: (0, 0)),
            vec(),
            vec(),
        ],
        out_specs=pl.BlockSpec((b, tb, mix), lambda j: (0, j, 0)),
        out_shape=jax.ShapeDtypeStruct((b, s, mix), F32),
        scratch_shapes=[
            pltpu.VMEM((b, SUBLANES, mix), F32),
            pltpu.VMEM((b, tb, mix), F32),
            pltpu.VMEM((b, tb, mix), F32),
            pltpu.VMEM((b, 1, mix), F32),
        ],
        compiler_params=_params("arbitrary"),
        name="rglru",
    )(hn, w_bf, cw, cb, wa_bd, ba, wx_bd, bx, lam)


def _sb_kernel(q_ref, k_ref, v_ref, tri_ref, o_ref, acc_ref, car_ref, *, blk, dh, n_h):
    qi = pl.program_id(2)
    row = lax.broadcasted_iota(jnp.int32, (blk, blk), 0)
    col = lax.broadcasted_iota(jnp.int32, (blk, blk), 1)
    causal = col < row
    tri = tri_ref[...]
    heads = [slice(h * dh, (h + 1) * dh) for h in range(n_h)]
    qs = [q_ref[0, :, lanes] for lanes in heads]

    acc_ref[...] = jnp.zeros(acc_ref.shape, F32)
    car_ref[...] = jnp.zeros(car_ref.shape, F32)

    def block(kj, diag, live=None):
        rows = pl.ds(pl.multiple_of(kj * blk, blk), blk)
        carry_min = None
        sps, log_betas = [], []
        for h, lanes in enumerate(heads):
            kb = k_ref[0, rows, lanes]
            z = lax.dot_general(qs[h], kb, (((1,), (1,)), ((), ())), preferred_element_type=F32)
            sp = jnp.maximum(z, 0.0) + _log1p_exp_neg_abs(z)
            log_betas.append(z - sp)
            if diag:
                sp = jnp.where(causal, sp, 0.0)
            sps.append(sp)
        inner_all = jnp.dot(jnp.concatenate([sp.astype(BF16) for sp in sps], axis=0), tri,
                            preferred_element_type=F32)
        for h, lanes in enumerate(heads):
            vb = v_ref[0, rows, lanes]
            sp, log_beta = sps[h], log_betas[h]
            inner = inner_all[h * blk:(h + 1) * blk, :]
            carry = car_ref[h]
            wgt = jnp.exp(log_beta - inner - carry)
            if diag:
                wgt = jnp.where(causal, wgt, 0.0)
            pv = jnp.dot(wgt.astype(BF16), vb, preferred_element_type=F32)
            if live is not None:
                pv = jnp.where(live, pv, 0.0)
            acc_ref[h] += pv
            carry = carry + jnp.sum(sp, axis=1, keepdims=True)
            car_ref[h] = carry
            carry_min = carry if carry_min is None else jnp.minimum(carry_min, carry)
        return jnp.min(carry_min)

    def cond(c):
        return jnp.logical_and(c[0] >= 0, c[1] < SB_ZERO_EXP)

    def body(c):
        return c[0] - 1, block(c[0], False)

    block(qi, True)
    first_min = block(jnp.maximum(qi - 1, 0), False, live=qi >= 1)
    lax.while_loop(cond, body, (qi - 2, first_min))
    for h, lanes in enumerate(heads):
        o_ref[0, :, lanes] = acc_ref[h].astype(o_ref.dtype)


def _stick_breaking(qkv, tri, mix):
    b, s, _ = qkv.shape
    blk = min(SB_BLK, s)
    dh = SB_HEAD_DIM
    width = SB_HEADS_PER_STEP * dh
    n_grp = mix // width
    return pl.pallas_call(
        functools.partial(_sb_kernel, blk=blk, dh=dh, n_h=SB_HEADS_PER_STEP),
        grid=(b, n_grp, s // blk),
        in_specs=[
            pl.BlockSpec((1, blk, width), lambda i, g, t: (i, t, g)),
            pl.BlockSpec((1, s, width), lambda i, g, t: (i, 0, n_grp + g)),
            pl.BlockSpec((1, s, width), lambda i, g, t: (i, 0, 2 * n_grp + g)),
            pl.BlockSpec((blk, blk), lambda i, g, t: (0, 0)),
        ],
        out_specs=pl.BlockSpec((1, blk, width), lambda i, g, t: (i, t, g)),
        out_shape=jax.ShapeDtypeStruct((b, s, mix), BF16),
        scratch_shapes=[
            pltpu.VMEM((SB_HEADS_PER_STEP, blk, dh), F32),
            pltpu.VMEM((SB_HEADS_PER_STEP, blk, 1), F32),
        ],
        compiler_params=_params("parallel", "parallel", "arbitrary"),
        name="stick_breaking",
    )(qkv, qkv, qkv, tri)


def _hgrn2_kernel(hn_ref, wp_ref, lbl_ref, ng_ref, sel_ref, o_ref,
                  st_ref, vg_ref, qkb_ref, diag_ref, *, layer, tb, chunk, sub):
    n_b = hn_ref.shape[0]
    mix = o_ref.shape[-1]
    hk = HG_HEAD_DIM
    n_head = mix // hk
    n_sub = chunk // sub
    grp = SUBLANES
    n_grp = sub // grp

    @pl.when(pl.program_id(0) == 0)
    def _():
        st_ref[...] = jnp.zeros(st_ref.shape, F32)

    lbl = lbl_ref[...]
    p = jnp.exp(lbl - jnp.max(lbl, axis=0, keepdims=True))
    p = p / jnp.sum(p, axis=0, keepdims=True)
    lb = jnp.zeros((1, mix), F32)
    for m in range(1, layer + 1):
        lb = lb + p[m:m + 1, :]
    log_lb = jnp.log(jnp.maximum(lb, LB_FLOOR))
    log_1m_lb = jnp.log1p(-lb)

    row = lax.broadcasted_iota(jnp.int32, (chunk, chunk), 0)
    col = lax.broadcasted_iota(jnp.int32, (chunk, chunk), 1)
    same_sub = (row // sub) == (col // sub)
    later_sub = (col // sub) > (row // sub)
    causal = row <= col
    trow = lax.broadcasted_iota(jnp.int32, (chunk, mix), 0)

    def stage1(c, slot):
        rows = pl.ds(pl.multiple_of(c * chunk, chunk), chunk)
        proj = jnp.dot(jnp.concatenate([hn_ref[bi, rows, :] for bi in range(n_b)], axis=0),
                       wp_ref[...], preferred_element_type=F32)
        per_b = []
        span = None
        for bi in range(n_b):
            brows = slice(bi * chunk, (bi + 1) * chunk)
            q_raw = proj[brows, 0:mix]
            f_pre = proj[brows, mix:2 * mix]
            vg_ref[bi, rows, :] = proj[brows, 2 * mix:4 * mix]
            q = q_raw * jax.nn.sigmoid(q_raw)
            t2 = log_1m_lb - (jnp.maximum(-f_pre, 0.0) + _log1p_exp_neg_abs(f_pre))
            log_f = jnp.maximum(log_lb, t2) + _log1p_exp_neg_abs(log_lb - t2)
            k = (1.0 - lb) * jax.nn.sigmoid(-f_pre)
            bcum = log_f
            d = 1
            while d < chunk:
                bcum = bcum + jnp.where(trow >= d, pltpu.roll(bcum, d, 0), 0.0)
                d *= 2
            qkb_ref[slot, bi, 0] = q
            qkb_ref[slot, bi, 1] = k
            qkb_ref[slot, bi, 2] = bcum
            b3 = bcum.reshape(n_sub, sub, mix)
            q3 = q.reshape(n_sub, sub, mix)
            k3 = k.reshape(n_sub, sub, mix)
            ref3 = jnp.concatenate([jnp.zeros((1, 1, mix), F32), b3[:n_sub - 1, sub - 1:sub, :]], axis=0)
            drop = jnp.max(ref3 - b3[:, sub - 1:sub, :])
            span = drop if span is None else jnp.maximum(span, drop)
            per_b.append((q3, k3, b3, ref3))

        def factored():
            for bi, (q3, k3, b3, ref3) in enumerate(per_b):
                q_s = (q3 * jnp.exp(b3 - ref3)).reshape(chunk, mix).astype(BF16)
                k_s = (k3 * jnp.exp(ref3 - b3)).reshape(chunk, mix).astype(BF16)
                for hd in range(n_head):
                    hl = slice(hd * hk, (hd + 1) * hk)
                    diag_ref[slot, bi, hd] = lax.dot_general(
                        k_s[:, hl], q_s[:, hl], (((1,), (1,)), ((), ())), preferred_element_type=F32)

        def pairwise():
            for bi, (q3, k3, b3, ref3) in enumerate(per_b):
                b3s = b3 * LOG2_E
                tiles = tuple([] for _ in range(n_grp))
                for tl in range(sub):
                    b_t = b3s[:, tl:tl + 1, :]
                    q_t = q3[:, tl:tl + 1, :]
                    for gi in range(tl // grp + 1):
                        srows = slice(gi * grp, (gi + 1) * grp)
                        dec = jnp.exp2(jnp.minimum(b_t - b3s[:, srows, :], 0.0))
                        tile = (dec * (q_t * k3[:, srows, :])).reshape(n_sub * grp, mix).astype(BF16)
                        tiles[gi].append(tile)
                sums = []
                for gi in range(n_grp):
                    lhs = jnp.concatenate(
                        [jnp.concatenate([t[:, hd * hk:(hd + 1) * hk] for t in tiles[gi]], axis=1)
                         for hd in range(n_head)], axis=0)
                    sums.append(jnp.dot(lhs, sel_ref[gi * grp * hk:, :], preferred_element_type=F32))
                for hd in range(n_head):
                    r0 = hd * n_sub * grp
                    diag_ref[slot, bi, hd] = jnp.concatenate(
                        [sums[gi][r0 + j * grp:r0 + (j + 1) * grp, :]
                         for j in range(n_sub) for gi in range(n_grp)], axis=0)

        factored()
        pl.when(span > HG_SAFE_SPAN)(pairwise)

    def stage2(c, slot):
        rows = pl.ds(pl.multiple_of(c * chunk, chunk), chunk)
        for bi in range(n_b):
            q = qkb_ref[slot, bi, 0]
            k = qkb_ref[slot, bi, 1]
            bcum = qkb_ref[slot, bi, 2]
            b_last = bcum[chunk - 1:chunk, :]
            vb = vg_ref[bi, rows, 0:mix].astype(BF16)
            q_in = (q * jnp.exp(bcum)).astype(BF16)
            k_st = (k * jnp.exp(b_last - bcum)).astype(BF16)
            decay_last = jnp.exp(b_last)
            g_raw = vg_ref[bi, rows, mix:2 * mix]
            for hd in range(n_head):
                hl = slice(hd * hk, (hd + 1) * hk)
                at = jnp.where(same_sub & causal, diag_ref[slot, bi, hd], 0.0)
                k_rows, q_cols = [], []
                for j in range(n_sub - 1):
                    r_j = bcum[(j + 1) * sub - 1:(j + 1) * sub, hl]
                    srows = slice(j * sub, (j + 1) * sub)
                    trows = slice((j + 1) * sub, chunk)
                    k_hat = (k[srows, hl] * jnp.exp(r_j - bcum[srows, hl])).astype(BF16)
                    q_j = (q[trows, hl] * jnp.exp(bcum[trows, hl] - r_j)).astype(BF16)
                    q_cols.append(jnp.concatenate([jnp.zeros(((j + 1) * sub, hk), BF16), q_j], axis=0))
                    k_rows.append(jnp.concatenate(
                        [k_hat if jj == j else jnp.zeros((sub, hk), BF16) for jj in range(n_sub - 1)],
                        axis=1))
                if k_rows:
                    k_rows.append(jnp.zeros((sub, (n_sub - 1) * hk), BF16))
                    a_off = lax.dot_general(jnp.concatenate(k_rows, axis=0),
                                            jnp.concatenate(q_cols, axis=1),
                                            (((1,), (1,)), ((), ())), preferred_element_type=F32)
                    at = jnp.where(later_sub, a_off, at)
                st = st_ref[bi * n_head + hd]
                o = lax.dot_general(at.astype(BF16), vb[:, hl], (((0,), (0,)), ((), ())),
                                    preferred_element_type=F32)
                o = o + lax.dot_general(q_in[:, hl], st.astype(BF16), (((1,), (1,)), ((), ())),
                                        preferred_element_type=F32)
                st_ref[bi * n_head + hd] = decay_last[:, hl] * st + lax.dot_general(
                    vb[:, hl], k_st[:, hl], (((0,), (0,)), ((), ())), preferred_element_type=F32)
                o = _rms_scale(o, ng_ref[...])
                g_h = g_raw[:, hl]
                o_ref[bi, rows, hl] = (o * (g_h * jax.nn.sigmoid(g_h))).astype(o_ref.dtype)

    n_chunk = tb // chunk
    stage1(0, 0)

    def body(c, _):
        stage2(c - 1, lax.rem(c - 1, 2))
        stage1(c, lax.rem(c, 2))
        return 0

    lax.fori_loop(1, n_chunk, body, 0)
    stage2(n_chunk - 1, (n_chunk - 1) % 2)


def _hgrn2(hn, w_bf, col0, lb_logits, norm_gain, sel, layer):
    b, s, d = hn.shape
    mix = lb_logits.shape[1]
    wcb = col0 // (4 * mix)
    tb = min(HG_TB, s)
    chunk, sub = HG_CHUNK, HG_SUB
    assert sub % (2 * SUBLANES) == 0 and chunk % sub == 0 and tb % chunk == 0
    n_head = mix // HG_HEAD_DIM
    depth = lb_logits.shape[0]
    return pl.pallas_call(
        functools.partial(_hgrn2_kernel, layer=layer, tb=tb, chunk=chunk, sub=sub),
        grid=(s // tb,),
        in_specs=[
            pl.BlockSpec((b, tb, d), lambda j: (0, j, 0)),
            pl.BlockSpec((None, d, 4 * mix), lambda j: (layer, 0, wcb)),
            pl.BlockSpec((depth, mix), lambda j: (0, 0)),
            pl.BlockSpec((1, HG_HEAD_DIM), lambda j: (0, 0)),
            pl.BlockSpec((sub * HG_HEAD_DIM, chunk), lambda j: (0, 0)),
        ],
        out_specs=pl.BlockSpec((b, tb, mix), lambda j: (0, j, 0)),
        out_shape=jax.ShapeDtypeStruct((b, s, mix), BF16),
        scratch_shapes=[
            pltpu.VMEM((b * n_head, HG_HEAD_DIM, HG_HEAD_DIM), F32),
            pltpu.VMEM((b, tb, 2 * mix), F32),
            pltpu.VMEM((2, b, 3, chunk, mix), F32),
            pltpu.VMEM((2, b, n_head, chunk, chunk), F32),
        ],
        compiler_params=_params("arbitrary"),
        name="hgrn2",
    )(hn, w_bf, lb_logits, norm_gain, sel)


def _merge_kernel(ya_ref, yb_ref, yc_ref, hn_ref, x_ref, wga_ref, wgb_ref, wgc_ref,
                  wbr_ref, wout_ref, gain_ref, o_ref):
    hn = hn_ref[...]
    merged = None
    for idx, (y_ref, wg_ref) in enumerate(((ya_ref, wga_ref), (yb_ref, wgb_ref), (yc_ref, wgc_ref))):
        gate = jnp.dot(hn, wg_ref[...], preferred_element_type=F32)
        part = jnp.dot(y_ref[...].astype(BF16), wbr_ref[idx], preferred_element_type=F32)
        part = jax.nn.sigmoid(gate) * part
        merged = part if merged is None else merged + part
    out = jnp.dot(merged.astype(BF16), wout_ref[...], preferred_element_type=F32)
    o_ref[...] = x_ref[...] + _rms_scale(out, gain_ref[...])


def _merge(ya, yb, yc, hn2, x2, w_bf, layer, gate_col0, wbr_bf, wout_bf, gain):
    t, d = x2.shape
    mix = ya.shape[1]
    tm = min(MERGE_TM, t)
    gcb = gate_col0 // d
    ybs = lambda: pl.BlockSpec((tm, mix), lambda i: (i, 0))
    row = lambda: pl.BlockSpec((tm, d), lambda i: (i, 0))
    wgate = lambda off: pl.BlockSpec((None, d, d), lambda i: (layer, 0, gcb + off))
    return pl.pallas_call(
        _merge_kernel,
        grid=(t // tm,),
        in_specs=[
            ybs(), ybs(), ybs(), row(), row(),
            wgate(0), wgate(1), wgate(2),
            pl.BlockSpec((None,) + wbr_bf.shape[1:], lambda i: (layer, 0, 0, 0)),
            pl.BlockSpec((None, d, d), lambda i: (layer, 0, 0)),
            pl.BlockSpec((1, d), lambda i: (0, 0)),
        ],
        out_specs=row(),
        out_shape=jax.ShapeDtypeStruct((t, d), F32),
        compiler_params=_params("parallel"),
        name="merge_out",
    )(ya, yb, yc, hn2, x2, w_bf, w_bf, w_bf, wbr_bf, wout_bf, gain)


def _mlp_kernel(x_ref, g_in_ref, wup_ref, wdn_ref, g_out_ref, o_ref):
    x = x_ref[...]
    h = _rms_scale(x, g_in_ref[...]).astype(BF16)
    u = jnp.maximum(jnp.dot(h, wup_ref[...], preferred_element_type=F32), 0.0)
    m = jnp.dot((u * u).astype(BF16), wdn_ref[...], preferred_element_type=F32)
    o_ref[...] = x + _rms_scale(m, g_out_ref[...])


def _mlp(x2, g_in, wup_bf, wdn_bf, layer, g_out):
    t, d = x2.shape
    dff = wup_bf.shape[-1]
    tm = min(MLP_TM, t)
    resident = dict(pipeline_mode=pl.Buffered(1))
    return pl.pallas_call(
        _mlp_kernel,
        grid=(t // tm,),
        in_specs=[
            pl.BlockSpec((tm, d), lambda i: (i, 0)),
            pl.BlockSpec((1, d), lambda i: (0, 0)),
            pl.BlockSpec((None, d, dff), lambda i: (layer, 0, 0), **resident),
            pl.BlockSpec((None, dff, d), lambda i: (layer, 0, 0), **resident),
            pl.BlockSpec((1, d), lambda i: (0, 0)),
        ],
        out_specs=pl.BlockSpec((tm, d), lambda i: (i, 0)),
        out_shape=jax.ShapeDtypeStruct((t, d), F32),
        compiler_params=_params("parallel"),
        name="mlp",
    )(x2, g_in, wup_bf, wdn_bf, g_out)


def _block_diag(w):
    n, c, _ = w.shape
    eye = jnp.eye(n, dtype=w.dtype)
    return (eye[:, None, :, None] * w[:, :, None, :]).reshape(n * c, n * c)


def kernel(x, ln_gains, w_in, conv_w, conv_b, rg_w_a, rg_b_a, rg_w_x, rg_b_x, rg_lambda,
           lb_logits, hgrn_norm, w_branch, w_out, w_up, w_down):
    b, s, d = x.shape
    depth = w_in.shape[0]
    mix = conv_w.shape[-1]
    t = b * s
    qkv_lo, hg_lo, gate_lo = mix, 4 * mix, 8 * mix

    blk = min(SB_BLK, s)
    tri = (lax.broadcasted_iota(jnp.int32, (blk, blk), 0)
           > lax.broadcasted_iota(jnp.int32, (blk, blk), 1)).astype(BF16)
    sel_r = lax.broadcasted_iota(jnp.int32, (HG_SUB * HG_HEAD_DIM, HG_CHUNK), 0) // HG_HEAD_DIM
    sel_c = lax.broadcasted_iota(jnp.int32, (HG_SUB * HG_HEAD_DIM, HG_CHUNK), 1) % HG_SUB
    sel = (sel_r == sel_c).astype(BF16)

    w_in_bf, w_br_bf, w_out_bf = w_in.astype(BF16), w_branch.astype(BF16), w_out.astype(BF16)
    w_up_bf, w_dn_bf = w_up.astype(BF16), w_down.astype(BF16)

    x2 = x.reshape(t, d)
    for l in range(depth):
        g = ln_gains[l]
        qkv, hn2 = _in_proj(x2, g[0:1], w_in_bf, l, mix, qkv_lo, SB_HEAD_DIM ** -0.5)
        hn = hn2.reshape(b, s, d)
        y_a = _rglru(hn, w_in_bf, l, 0, conv_w[l], conv_b[l:l + 1],
                     _block_diag(rg_w_a[l]).astype(BF16), rg_b_a[l:l + 1],
                     _block_diag(rg_w_x[l]).astype(BF16), rg_b_x[l:l + 1],
                     rg_lambda[l:l + 1])
        y_b = _stick_breaking(qkv.reshape(b, s, -1), tri, mix)
        y_c = _hgrn2(hn, w_in_bf, hg_lo, lb_logits, hgrn_norm[l:l + 1], sel, l)
        x2 = _merge(y_a.reshape(t, mix), y_b.reshape(t, mix), y_c.reshape(t, mix), hn2, x2,
                    w_in_bf, l, gate_lo, w_br_bf, w_out_bf, g[1:2])
        x2 = _mlp(x2, g[2:3], w_up_bf, w_dn_bf, l, g[3:4])
    return x2.reshape(b, s, d)
```

```python
import functools

import jax
import jax.numpy as jnp
from jax import lax
from jax.experimental import pallas as pl
from jax.experimental.pallas import tpu as pltpu

F32 = jnp.float32
BF16 = jnp.bfloat16

EPS = 1e-6
RG_C = 8.0
CONV_WIDTH = 4
SB_HEAD_DIM = 64
HG_HEAD_DIM = 128
LB_FLOOR = 1e-30
LOG2_E = 1.4426950408889634
HG_SAFE_SPAN = 60.0
SB_ZERO_EXP = 105.0

SUBLANES = 8
VMEM_LIMIT = 56 * 1024 * 1024

RG_TB = 512
SB_BLK = 256
SB_HEADS_PER_STEP = 8
HG_TB = 512
HG_CHUNK = 64
HG_SUB = 16
MERGE_TM = 512
MLP_TM = 512


def _params(*sem):
    return pltpu.CompilerParams(dimension_semantics=sem, vmem_limit_bytes=VMEM_LIMIT)


def _softplus(x):
    return jnp.maximum(x, 0.0) + jnp.log1p(jnp.exp(-jnp.abs(x)))


def _log1p_exp_neg_abs(x):
    return jnp.log(1.0 + jnp.exp2(jnp.abs(x) * (-LOG2_E)))


def _rms_scale(x, gain):
    inv = lax.rsqrt(jnp.mean(x * x, axis=-1, keepdims=True) + EPS)
    return x * inv * gain


def _rglru_kernel(x_ref, g_ref, wp_ref, cw_ref, cb_ref, wa_ref, ba_ref, wx_ref, bx_ref, lam_ref,
                  o_ref, hn_ref, qkv_ref, tail_ref, a_ref, u_ref, h_ref, *, tb, q_scale):
    j = pl.program_id(0)
    n_b = x_ref.shape[0]
    w = cw_ref.shape[-1]

    @pl.when(j == 0)
    def _():
        tail_ref[...] = jnp.zeros(tail_ref.shape, F32)
        h_ref[...] = jnp.zeros(h_ref.shape, F32)

    sp_lam = _softplus(-lam_ref[...])
    row = lax.broadcasted_iota(jnp.int32, (tb, w), 0) + j * tb
    for bi in range(n_b):
        hn = _rms_scale(x_ref[bi], g_ref[...]).astype(BF16)
        hn_ref[bi] = hn
        proj = jnp.dot(hn, wp_ref[...], preferred_element_type=F32)
        qkv_ref[bi, :, :w] = (proj[:, w:2 * w] * q_scale).astype(BF16)
        qkv_ref[bi, :, w:] = proj[:, 2 * w:].astype(BF16)
        x = proj[:, :w]
        xp = jnp.concatenate([tail_ref[bi], x], axis=0)
        acc = cw_ref[0:1, :] * xp
        for k in range(1, CONV_WIDTH):
            acc = pltpu.roll(acc, 1, 0) + cw_ref[k:k + 1, :] * xp
        xc = acc[SUBLANES:, :] + cb_ref[...]
        tail_ref[bi] = x[tb - SUBLANES:tb, :]

        xcb = xc.astype(BF16)
        r = jax.nn.sigmoid(jnp.dot(xcb, wa_ref[...], preferred_element_type=F32) + ba_ref[...])
        gate_i = jax.nn.sigmoid(jnp.dot(xcb, wx_ref[...], preferred_element_type=F32) + bx_ref[...])
        a = jnp.exp(-RG_C * r * sp_lam)
        mult = jnp.sqrt(jnp.maximum(1.0 - a * a, 0.0))
        mult = jnp.where(row == 0, 1.0, mult)
        a_ref[bi] = a
        u_ref[bi] = mult * (gate_i * xc)

    def step(t, hs):
        out = []
        for bi in range(n_b):
            h = a_ref[bi, pl.ds(t, 1), :] * hs[bi] + u_ref[bi, pl.ds(t, 1), :]
            o_ref[bi, pl.ds(t, 1), :] = h
            out.append(h)
        return tuple(out)

    hs = lax.fori_loop(0, tb, step, tuple(h_ref[bi] for bi in range(n_b)), unroll=SUBLANES)
    for bi in range(n_b):
        h_ref[bi] = hs[bi]


def _norm_qkv_rglru(x3, gain, w_bf, layer, cw, cb, wa_bd, ba, wx_bd, bx, lam):
    b, s, d = x3.shape
    mix = cw.shape[-1]
    tb = min(RG_TB, s)
    vec = lambda: pl.BlockSpec((1, mix), lambda j: (0, 0))
    blk3 = lambda width: pl.BlockSpec((b, tb, width), lambda j: (0, j, 0))
    return pl.pallas_call(
        functools.partial(_rglru_kernel, tb=tb, q_scale=SB_HEAD_DIM ** -0.5),
        grid=(s // tb,),
        in_specs=[
            blk3(d),
            pl.BlockSpec((1, d), lambda j: (0, 0)),
            pl.BlockSpec((None, d, 4 * mix), lambda j: (layer, 0, 0)),
            pl.BlockSpec((CONV_WIDTH, mix), lambda j: (0, 0)),
            vec(),
            pl.BlockSpec((mix, mix), lambda j: (0, 0)),
            vec(),
            pl.BlockSpec((mix, mix), lambda j: (0, 0)),
            vec(),
            vec(),
        ],
        out_specs=[blk3(mix), blk3(d), blk3(3 * mix)],
        out_shape=[jax.ShapeDtypeStruct((b, s, mix), F32),
                   jax.ShapeDtypeStruct((b, s, d), BF16),
                   jax.ShapeDtypeStruct((b, s, 3 * mix), BF16)],
        scratch_shapes=[
            pltpu.VMEM((b, SUBLANES, mix), F32),
            pltpu.VMEM((b, tb, mix), F32),
            pltpu.VMEM((b, tb, mix), F32),
            pltpu.VMEM((b, 1, mix), F32),
        ],
        compiler_params=_params("arbitrary"),
        name="norm_qkv_rglru",
    )(x3, gain, w_bf, cw, cb, wa_bd, ba, wx_bd, bx, lam)


def _sb_kernel(q_ref, k_ref, v_ref, tri_ref, o_ref, acc_ref, car_ref, *, blk, dh, n_h):
    qi = pl.program_id(2)
    row = lax.broadcasted_iota(jnp.int32, (blk, blk), 0)
    col = lax.broadcasted_iota(jnp.int32, (blk, blk), 1)
    causal = col < row
    tri = tri_ref[...]
    heads = [slice(h * dh, (h + 1) * dh) for h in range(n_h)]
    qs = [q_ref[0, :, lanes] for lanes in heads]

    acc_ref[...] = jnp.zeros(acc_ref.shape, F32)
    car_ref[...] = jnp.zeros(car_ref.shape, F32)

    def block(kj, diag, live=None):
        rows = pl.ds(pl.multiple_of(kj * blk, blk), blk)
        carry_min = None
        sps, log_betas = [], []
        for h, lanes in enumerate(heads):
            kb = k_ref[0, rows, lanes]
            z = lax.dot_general(qs[h], kb, (((1,), (1,)), ((), ())), preferred_element_type=F32)
            sp = jnp.maximum(z, 0.0) + _log1p_exp_neg_abs(z)
            log_betas.append(z - sp)
            if diag:
                sp = jnp.where(causal, sp, 0.0)
            sps.append(sp)
        inner_all = jnp.dot(jnp.concatenate([sp.astype(BF16) for sp in sps], axis=0), tri,
                            preferred_element_type=F32)
        for h, lanes in enumerate(heads):
            vb = v_ref[0, rows, lanes]
            sp, log_beta = sps[h], log_betas[h]
            inner = inner_all[h * blk:(h + 1) * blk, :]
            carry = car_ref[h]
            wgt = jnp.exp(log_beta - inner - carry)
            if diag:
                wgt = jnp.where(causal, wgt, 0.0)
            pv = jnp.dot(wgt.astype(BF16), vb, preferred_element_type=F32)
            if live is not None:
                pv = jnp.where(live, pv, 0.0)
            acc_ref[h] += pv
            carry = carry + jnp.sum(sp, axis=1, keepdims=True)
            car_ref[h] = carry
            carry_min = carry if carry_min is None else jnp.minimum(carry_min, carry)
        return jnp.min(carry_min)

    def cond(c):
        return jnp.logical_and(c[0] >= 0, c[1] < SB_ZERO_EXP)

    def body(c):
        return c[0] - 1, block(c[0], False)

    block(qi, True)
    first_min = block(jnp.maximum(qi - 1, 0), False, live=qi >= 1)
    lax.while_loop(cond, body, (qi - 2, first_min))
    for h, lanes in enumerate(heads):
        o_ref[0, :, lanes] = acc_ref[h].astype(o_ref.dtype)


def _stick_breaking(qkv, tri, mix):
    b, s, _ = qkv.shape
    blk = min(SB_BLK, s)
    dh = SB_HEAD_DIM
    width = SB_HEADS_PER_STEP * dh
    n_grp = mix // width
    return pl.pallas_call(
        functools.partial(_sb_kernel, blk=blk, dh=dh, n_h=SB_HEADS_PER_STEP),
        grid=(b, n_grp, s // blk),
        in_specs=[
            pl.BlockSpec((1, blk, width), lambda i, g, t: (i, t, g)),
            pl.BlockSpec((1, s, width), lambda i, g, t: (i, 0, n_grp + g)),
            pl.BlockSpec((1, s, width), lambda i, g, t: (i, 0, 2 * n_grp + g)),
            pl.BlockSpec((blk, blk), lambda i, g, t: (0, 0)),
        ],
        out_specs=pl.BlockSpec((1, blk, width), lambda i, g, t: (i, t, g)),
        out_shape=jax.ShapeDtypeStruct((b, s, mix), BF16),
        scratch_shapes=[
            pltpu.VMEM((SB_HEADS_PER_STEP, blk, dh), F32),
            pltpu.VMEM((SB_HEADS_PER_STEP, blk, 1), F32),
        ],
        compiler_params=_params("parallel", "parallel", "arbitrary"),
        name="stick_breaking",
    )(qkv, qkv, qkv, tri)


def _hgrn2_kernel(hn_ref, wp_ref, lbl_ref, ng_ref, sel_ref, o_ref,
                  st_ref, vg_ref, qkb_ref, diag_ref, *, layer, tb, chunk, sub):
    n_b = hn_ref.shape[0]
    mix = o_ref.shape[-1]
    hk = HG_HEAD_DIM
    n_head = mix // hk
    n_sub = chunk // sub
    grp = SUBLANES
    n_grp = sub // grp

    @pl.when(pl.program_id(0) == 0)
    def _():
        st_ref[...] = jnp.zeros(st_ref.shape, F32)

    lbl = lbl_ref[...]
    p = jnp.exp(lbl - jnp.max(lbl, axis=0, keepdims=True))
    p = p / jnp.sum(p, axis=0, keepdims=True)
    lb = jnp.zeros((1, mix), F32)
    for m in range(1, layer + 1):
        lb = lb + p[m:m + 1, :]
    log_lb = jnp.log(jnp.maximum(lb, LB_FLOOR))
    log_1m_lb = jnp.log1p(-lb)

    row = lax.broadcasted_iota(jnp.int32, (chunk, chunk), 0)
    col = lax.broadcasted_iota(jnp.int32, (chunk, chunk), 1)
    same_sub = (row // sub) == (col // sub)
    later_sub = (col // sub) > (row // sub)
    causal = row <= col
    trow = lax.broadcasted_iota(jnp.int32, (chunk, mix), 0)

    def stage1(c, slot):
        rows = pl.ds(pl.multiple_of(c * chunk, chunk), chunk)
        proj = jnp.dot(jnp.concatenate([hn_ref[bi, rows, :] for bi in range(n_b)], axis=0),
                       wp_ref[...], preferred_element_type=F32)
        per_b = []
        span = None
        for bi in range(n_b):
            brows = slice(bi * chunk, (bi + 1) * chunk)
            q_raw = proj[brows, 0:mix]
            f_pre = proj[brows, mix:2 * mix]
            vg_ref[bi, rows, :] = proj[brows, 2 * mix:4 * mix]
            q = q_raw * jax.nn.sigmoid(q_raw)
            t2 = log_1m_lb - (jnp.maximum(-f_pre, 0.0) + _log1p_exp_neg_abs(f_pre))
            log_f = jnp.maximum(log_lb, t2) + _log1p_exp_neg_abs(log_lb - t2)
            k = (1.0 - lb) * jax.nn.sigmoid(-f_pre)
            bcum = log_f
            d = 1
            while d < chunk:
                bcum = bcum + jnp.where(trow >= d, pltpu.roll(bcum, d, 0), 0.0)
                d *= 2
            qkb_ref[slot, bi, 0] = q
            qkb_ref[slot, bi, 1] = k
            qkb_ref[slot, bi, 2] = bcum
            b3 = bcum.reshape(n_sub, sub, mix)
            q3 = q.reshape(n_sub, sub, mix)
            k3 = k.reshape(n_sub, sub, mix)
            ref3 = jnp.concatenate([jnp.zeros((1, 1, mix), F32), b3[:n_sub - 1, sub - 1:sub, :]], axis=0)
            drop = jnp.max(ref3 - b3[:, sub - 1:sub, :])
            span = drop if span is None else jnp.maximum(span, drop)
            per_b.append((q3, k3, b3, ref3))

        def factored():
            for bi, (q3, k3, b3, ref3) in enumerate(per_b):
                q_s = (q3 * jnp.exp(b3 - ref3)).reshape(chunk, mix).astype(BF16)
                k_s = (k3 * jnp.exp(ref3 - b3)).reshape(chunk, mix).astype(BF16)
                for hd in range(n_head):
                    hl = slice(hd * hk, (hd + 1) * hk)
                    diag_ref[slot, bi, hd] = lax.dot_general(
                        k_s[:, hl], q_s[:, hl], (((1,), (1,)), ((), ())), preferred_element_type=F32)

        def pairwise():
            for bi, (q3, k3, b3, ref3) in enumerate(per_b):
                b3s = b3 * LOG2_E
                tiles = tuple([] for _ in range(n_grp))
                for tl in range(sub):
                    b_t = b3s[:, tl:tl + 1, :]
                    q_t = q3[:, tl:tl + 1, :]
                    for gi in range(tl // grp + 1):
                        srows = slice(gi * grp, (gi + 1) * grp)
                        dec = jnp.exp2(jnp.minimum(b_t - b3s[:, srows, :], 0.0))
                        tile = (dec * (q_t * k3[:, srows, :])).reshape(n_sub * grp, mix).astype(BF16)
                        tiles[gi].append(tile)
                sums = []
                for gi in range(n_grp):
                    lhs = jnp.concatenate(
                        [jnp.concatenate([t[:, hd * hk:(hd + 1) * hk] for t in tiles[gi]], axis=1)
                         for hd in range(n_head)], axis=0)
                    sums.append(jnp.dot(lhs, sel_ref[gi * grp * hk:, :], preferred_element_type=F32))
                for hd in range(n_head):
                    r0 = hd * n_sub * grp
                    diag_ref[slot, bi, hd] = jnp.concatenate(
                        [sums[gi][r0 + j * grp:r0 + (j + 1) * grp, :]
                         for j in range(n_sub) for gi in range(n_grp)], axis=0)

        factored()
        pl.when(span > HG_SAFE_SPAN)(pairwise)

    def stage2(c, slot):
        rows = pl.ds(pl.multiple_of(c * chunk, chunk), chunk)
        for bi in range(n_b):
            q = qkb_ref[slot, bi, 0]
            k = qkb_ref[slot, bi, 1]
            bcum = qkb_ref[slot, bi, 2]
            b_last = bcum[chunk - 1:chunk, :]
            vb = vg_ref[bi, rows, 0:mix].astype(BF16)
            q_in = (q * jnp.exp(bcum)).astype(BF16)
            k_st = (k * jnp.exp(b_last - bcum)).astype(BF16)
            decay_last = jnp.exp(b_last)
            g_raw = vg_ref[bi, rows, mix:2 * mix]
            for hd in range(n_head):
                hl = slice(hd * hk, (hd + 1) * hk)
                at = jnp.where(same_sub & causal, diag_ref[slot, bi, hd], 0.0)
                k_rows, q_cols = [], []
                for j in range(n_sub - 1):
                    r_j = bcum[(j + 1) * sub - 1:(j + 1) * sub, hl]
                    srows = slice(j * sub, (j + 1) * sub)
                    trows = slice((j + 1) * sub, chunk)
                    k_hat = (k[srows, hl] * jnp.exp(r_j - bcum[srows, hl])).astype(BF16)
                    q_j = (q[trows, hl] * jnp.exp(bcum[trows, hl] - r_j)).astype(BF16)
                    q_cols.append(jnp.concatenate([jnp.zeros(((j + 1) * sub, hk), BF16), q_j], axis=0))
                    k_rows.append(jnp.concatenate(
                        [k_hat if jj == j else jnp.zeros((sub, hk), BF16) for jj in range(n_sub - 1)],
                        axis=1))
                if k_rows:
                    k_rows.append(jnp.zeros((sub, (n_sub - 1) * hk), BF16))
                    a_off = lax.dot_general(jnp.concatenate(k_rows, axis=0),
                                            jnp.concatenate(q_cols, axis=1),
                                            (((1,), (1,)), ((), ())), preferred_element_type=F32)
                    at = jnp.where(later_sub, a_off, at)
                st = st_ref[bi * n_head + hd]
                o = lax.dot_general(at.astype(BF16), vb[:, hl], (((0,), (0,)), ((), ())),
                                    preferred_element_type=F32)
                o = o + lax.dot_general(q_in[:, hl], st.astype(BF16), (((1,), (1,)), ((), ())),
                                        preferred_element_type=F32)
                st_ref[bi * n_head + hd] = decay_last[:, hl] * st + lax.dot_general(
                    vb[:, hl], k_st[:, hl], (((0,), (0,)), ((), ())), preferred_element_type=F32)
                o = _rms_scale(o, ng_ref[...])
                g_h = g_raw[:, hl]
                o_ref[bi, rows, hl] = (o * (g_h * jax.nn.sigmoid(g_h))).astype(o_ref.dtype)

    n_chunk = tb // chunk
    stage1(0, 0)

    def body(c, _):
        stage2(c - 1, lax.rem(c - 1, 2))
        stage1(c, lax.rem(c, 2))
        return 0

    lax.fori_loop(1, n_chunk, body, 0)
    stage2(n_chunk - 1, (n_chunk - 1) % 2)


def _hgrn2(hn, w_bf, col0, lb_logits, norm_gain, sel, layer):
    b, s, d = hn.shape
    mix = lb_logits.shape[1]
    wcb = col0 // (4 * mix)
    tb = min(HG_TB, s)
    chunk, sub = HG_CHUNK, HG_SUB
    assert sub % (2 * SUBLANES) == 0 and chunk % sub == 0 and tb % chunk == 0
    n_head = mix // HG_HEAD_DIM
    depth = lb_logits.shape[0]
    return pl.pallas_call(
        functools.partial(_hgrn2_kernel, layer=layer, tb=tb, chunk=chunk, sub=sub),
        grid=(s // tb,),
        in_specs=[
            pl.BlockSpec((b, tb, d), lambda j: (0, j, 0)),
            pl.BlockSpec((None, d, 4 * mix), lambda j: (layer, 0, wcb)),
            pl.BlockSpec((depth, mix), lambda j: (0, 0)),
            pl.BlockSpec((1, HG_HEAD_DIM), lambda j: (0, 0)),
            pl.BlockSpec((sub * HG_HEAD_DIM, chunk), lambda j: (0, 0)),
        ],
        out_specs=pl.BlockSpec((b, tb, mix), lambda j: (0, j, 0)),
        out_shape=jax.ShapeDtypeStruct((b, s, mix), BF16),
        scratch_shapes=[
            pltpu.VMEM((b * n_head, HG_HEAD_DIM, HG_HEAD_DIM), F32),
            pltpu.VMEM((b, tb, 2 * mix), F32),
            pltpu.VMEM((2, b, 3, chunk, mix), F32),
            pltpu.VMEM((2, b, n_head, chunk, chunk), F32),
        ],
        compiler_params=_params("arbitrary"),
        name="hgrn2",
    )(hn, w_bf, lb_logits, norm_gain, sel)


def _merge_kernel(ya_ref, yb_ref, yc_ref, hn_ref, x_ref, wga_ref, wgb_ref, wgc_ref,
                  wbr_ref, wout_ref, gain_ref, o_ref):
    hn = hn_ref[...]
    merged = None
    for idx, (y_ref, wg_ref) in enumerate(((ya_ref, wga_ref), (yb_ref, wgb_ref), (yc_ref, wgc_ref))):
        gate = jnp.dot(hn, wg_ref[...], preferred_element_type=F32)
        part = jnp.dot(y_ref[...].astype(BF16), wbr_ref[idx], preferred_element_type=F32)
        part = jax.nn.sigmoid(gate) * part
        merged = part if merged is None else merged + part
    out = jnp.dot(merged.astype(BF16), wout_ref[...], preferred_element_type=F32)
    o_ref[...] = x_ref[...] + _rms_scale(out, gain_ref[...])


def _merge(ya, yb, yc, hn2, x2, w_bf, layer, gate_col0, wbr_bf, wout_bf, gain):
    t, d = x2.shape
    mix = ya.shape[1]
    tm = min(MERGE_TM, t)
    gcb = gate_col0 // d
    ybs = lambda: pl.BlockSpec((tm, mix), lambda i: (i, 0))
    row = lambda: pl.BlockSpec((tm, d), lambda i: (i, 0))
    wgate = lambda off: pl.BlockSpec((None, d, d), lambda i: (layer, 0, gcb + off))
    return pl.pallas_call(
        _merge_kernel,
        grid=(t // tm,),
        in_specs=[
            ybs(), ybs(), ybs(), row(), row(),
            wgate(0), wgate(1), wgate(2),
            pl.BlockSpec((None,) + wbr_bf.shape[1:], lambda i: (layer, 0, 0, 0)),
            pl.BlockSpec((None, d, d), lambda i: (layer, 0, 0)),
            pl.BlockSpec((1, d), lambda i: (0, 0)),
        ],
        out_specs=row(),
        out_shape=jax.ShapeDtypeStruct((t, d), F32),
        compiler_params=_params("parallel"),
        name="merge_out",
    )(ya, yb, yc, hn2, x2, w_bf, w_bf, w_bf, wbr_bf, wout_bf, gain)


def _mlp_kernel(x_ref, g_in_ref, wup_ref, wdn_ref, g_out_ref, o_ref):
    x = x_ref[...]
    h = _rms_scale(x, g_in_ref[...]).astype(BF16)
    u = jnp.maximum(jnp.dot(h, wup_ref[...], preferred_element_type=F32), 0.0)
    m = jnp.dot((u * u).astype(BF16), wdn_ref[...], preferred_element_type=F32)
    o_ref[...] = x + _rms_scale(m, g_out_ref[...])


def _mlp(x2, g_in, wup_bf, wdn_bf, layer, g_out):
    t, d = x2.shape
    dff = wup_bf.shape[-1]
    tm = min(MLP_TM, t)
    resident = dict(pipeline_mode=pl.Buffered(1))
    return pl.pallas_call(
        _mlp_kernel,
        grid=(t // tm,),
        in_specs=[
            pl.BlockSpec((tm, d), lambda i: (i, 0)),
            pl.BlockSpec((1, d), lambda i: (0, 0)),
            pl.BlockSpec((None, d, dff), lambda i: (layer, 0, 0), **resident),
            pl.BlockSpec((None, dff, d), lambda i: (layer, 0, 0), **resident),
            pl.BlockSpec((1, d), lambda i: (0, 0)),
        ],
        out_specs=pl.BlockSpec((tm, d), lambda i: (i, 0)),
        out_shape=jax.ShapeDtypeStruct((t, d), F32),
        compiler_params=_params("parallel"),
        name="mlp",
    )(x2, g_in, wup_bf, wdn_bf, g_out)


def _block_diag(w):
    n, c, _ = w.shape
    eye = jnp.eye(n, dtype=w.dtype)
    return (eye[:, None, :, None] * w[:, :, None, :]).reshape(n * c, n * c)


def kernel(x, ln_gains, w_in, conv_w, conv_b, rg_w_a, rg_b_a, rg_w_x, rg_b_x, rg_lambda,
           lb_logits, hgrn_norm, w_branch, w_out, w_up, w_down):
    b, s, d = x.shape
    depth = w_in.shape[0]
    mix = conv_w.shape[-1]
    t = b * s
    hg_lo, gate_lo = 4 * mix, 8 * mix

    blk = min(SB_BLK, s)
    tri = (lax.broadcasted_iota(jnp.int32, (blk, blk), 0)
           > lax.broadcasted_iota(jnp.int32, (blk, blk), 1)).astype(BF16)
    sel_r = lax.broadcasted_iota(jnp.int32, (HG_SUB * HG_HEAD_DIM, HG_CHUNK), 0) // HG_HEAD_DIM
    sel_c = lax.broadcasted_iota(jnp.int32, (HG_SUB * HG_HEAD_DIM, HG_CHUNK), 1) % HG_SUB
    sel = (sel_r == sel_c).astype(BF16)

    w_in_bf, w_br_bf, w_out_bf = w_in.astype(BF16), w_branch.astype(BF16), w_out.astype(BF16)
    w_up_bf, w_dn_bf = w_up.astype(BF16), w_down.astype(BF16)

    x2 = x.reshape(t, d)
    for l in range(depth):
        g = ln_gains[l]
        y_a, hn, qkv = _norm_qkv_rglru(x2.reshape(b, s, d), g[0:1], w_in_bf, l, conv_w[l], conv_b[l:l + 1],
                                       _block_diag(rg_w_a[l]).astype(BF16), rg_b_a[l:l + 1],
                                       _block_diag(rg_w_x[l]).astype(BF16), rg_b_x[l:l + 1],
                                       rg_lambda[l:l + 1])
        y_b = _stick_breaking(qkv, tri, mix)
        y_c = _hgrn2(hn, w_in_bf, hg_lo, lb_logits, hgrn_norm[l:l + 1], sel, l)
        x2 = _merge(y_a.reshape(t, mix), y_b.reshape(t, mix), y_c.reshape(t, mix), hn.reshape(t, d), x2,
                    w_in_bf, l, gate_lo, w_br_bf, w_out_bf, g[1:2])
        x2 = _mlp(x2, g[2:3], w_up_bf, w_dn_bf, l, g[3:4])
    return x2.reshape(b, s, d)
```

```python
import functools

import jax
import jax.numpy as jnp
from jax import lax
from jax.experimental import pallas as pl
from jax.experimental.pallas import tpu as pltpu

F32 = jnp.float32
BF16 = jnp.bfloat16

EPS = 1e-6
RG_C = 8.0
CONV_WIDTH = 4
SB_HEAD_DIM = 64
HG_HEAD_DIM = 128
LB_FLOOR = 1e-30
LOG2_E = 1.4426950408889634
HG_SAFE_SPAN = 60.0
SB_ZERO_EXP = 105.0

SUBLANES = 8
VMEM_LIMIT = 56 * 1024 * 1024

RG_TB = 512
SB_BLK = 256
SB_HEADS_PER_STEP = 8
HG_TB = 512
HG_CHUNK = 64
HG_SUB = 16
MERGE_TM = 512
MLP_TM = 512


def _params(*sem):
    return pltpu.CompilerParams(dimension_semantics=sem, vmem_limit_bytes=VMEM_LIMIT)


def _softplus(x):
    return jnp.maximum(x, 0.0) + jnp.log1p(jnp.exp(-jnp.abs(x)))


def _log1p_exp_neg_abs(x):
    return jnp.log(1.0 + jnp.exp2(jnp.abs(x) * (-LOG2_E)))


def _rms_scale(x, gain):
    inv = lax.rsqrt(jnp.mean(x * x, axis=-1, keepdims=True) + EPS)
    return x * inv * gain


def _rglru_kernel(x_ref, g_ref, wp_ref, cw_ref, cb_ref, wa_ref, ba_ref, wx_ref, bx_ref, lam_ref,
                  o_ref, hn_ref, qkv_ref, tail_ref, a_ref, u_ref, h_ref, *, tb, q_scale):
    j = pl.program_id(0)
    n_b = x_ref.shape[0]
    w = cw_ref.shape[-1]

    @pl.when(j == 0)
    def _():
        tail_ref[...] = jnp.zeros(tail_ref.shape, F32)
        h_ref[...] = jnp.zeros(h_ref.shape, F32)

    sp_lam = _softplus(-lam_ref[...])
    row = lax.broadcasted_iota(jnp.int32, (tb, w), 0) + j * tb
    for bi in range(n_b):
        hn = _rms_scale(x_ref[bi], g_ref[...]).astype(BF16)
        hn_ref[bi] = hn
        proj = jnp.dot(hn, wp_ref[...], preferred_element_type=F32)
        qkv_ref[bi, :, :w] = (proj[:, w:2 * w] * q_scale).astype(BF16)
        qkv_ref[bi, :, w:] = proj[:, 2 * w:].astype(BF16)
        x = proj[:, :w]
        xp = jnp.concatenate([tail_ref[bi], x], axis=0)
        acc = cw_ref[0:1, :] * xp
        for k in range(1, CONV_WIDTH):
            acc = pltpu.roll(acc, 1, 0) + cw_ref[k:k + 1, :] * xp
        xc = acc[SUBLANES:, :] + cb_ref[...]
        tail_ref[bi] = x[tb - SUBLANES:tb, :]

        xcb = xc.astype(BF16)
        r = jax.nn.sigmoid(jnp.dot(xcb, wa_ref[...], preferred_element_type=F32) + ba_ref[...])
        gate_i = jax.nn.sigmoid(jnp.dot(xcb, wx_ref[...], preferred_element_type=F32) + bx_ref[...])
        a = jnp.exp(-RG_C * r * sp_lam)
        mult = jnp.sqrt(jnp.maximum(1.0 - a * a, 0.0))
        mult = jnp.where(row == 0, 1.0, mult)
        a_ref[bi] = a
        u_ref[bi] = mult * (gate_i * xc)

    def step(t, hs):
        out = []
        for bi in range(n_b):
            h = a_ref[bi, pl.ds(t, 1), :] * hs[bi] + u_ref[bi, pl.ds(t, 1), :]
            o_ref[bi, pl.ds(t, 1), :] = h
            out.append(h)
        return tuple(out)

    hs = lax.fori_loop(0, tb, step, tuple(h_ref[bi] for bi in range(n_b)), unroll=SUBLANES)
    for bi in range(n_b):
        h_ref[bi] = hs[bi]


def _norm_qkv_rglru(x3, gain, w_bf, layer, cw, cb, wa_bd, ba, wx_bd, bx, lam):
    b, s, d = x3.shape
    mix = cw.shape[-1]
    tb = min(RG_TB, s)
    vec = lambda: pl.BlockSpec((1, mix), lambda j: (0, 0))
    blk3 = lambda width: pl.BlockSpec((b, tb, width), lambda j: (0, j, 0))
    return pl.pallas_call(
        functools.partial(_rglru_kernel, tb=tb, q_scale=SB_HEAD_DIM ** -0.5),
        grid=(s // tb,),
        in_specs=[
            blk3(d),
            pl.BlockSpec((1, d), lambda j: (0, 0)),
            pl.BlockSpec((None, d, 4 * mix), lambda j: (layer, 0, 0)),
            pl.BlockSpec((CONV_WIDTH, mix), lambda j: (0, 0)),
            vec(),
            pl.BlockSpec((mix, mix), lambda j: (0, 0)),
            vec(),
            pl.BlockSpec((mix, mix), lambda j: (0, 0)),
            vec(),
            vec(),
        ],
        out_specs=[blk3(mix), blk3(d), blk3(3 * mix)],
        out_shape=[jax.ShapeDtypeStruct((b, s, mix), F32),
                   jax.ShapeDtypeStruct((b, s, d), BF16),
                   jax.ShapeDtypeStruct((b, s, 3 * mix), BF16)],
        scratch_shapes=[
            pltpu.VMEM((b, SUBLANES, mix), F32),
            pltpu.VMEM((b, tb, mix), F32),
            pltpu.VMEM((b, tb, mix), F32),
            pltpu.VMEM((b, 1, mix), F32),
        ],
        compiler_params=_params("arbitrary"),
        name="norm_qkv_rglru",
    )(x3, gain, w_bf, cw, cb, wa_bd, ba, wx_bd, bx, lam)


def _sb_kernel(q_ref, k_ref, v_ref, tri_ref, o_ref, acc_ref, car_ref, *, blk, dh, n_h):
    qi = pl.program_id(2)
    row = lax.broadcasted_iota(jnp.int32, (blk, blk), 0)
    col = lax.broadcasted_iota(jnp.int32, (blk, blk), 1)
    causal = col < row
    tri = tri_ref[...]
    heads = [slice(h * dh, (h + 1) * dh) for h in range(n_h)]
    qs = [q_ref[0, :, lanes] for lanes in heads]

    acc_ref[...] = jnp.zeros(acc_ref.shape, F32)
    car_ref[...] = jnp.zeros(car_ref.shape, F32)

    def block(kj, diag, live=None, nr=blk):
        rows = pl.ds(pl.multiple_of(kj * blk, blk), blk)
        carry_min = None
        sps, log_betas = [], []
        for h, lanes in enumerate(heads):
            kb = k_ref[0, rows, lanes]
            z = lax.dot_general(qs[h][:nr], kb, (((1,), (1,)), ((), ())), preferred_element_type=F32)
            sp = jnp.maximum(z, 0.0) + _log1p_exp_neg_abs(z)
            log_betas.append(z - sp)
            if diag:
                sp = jnp.where(causal, sp, 0.0)
            sps.append(sp)
        inner_all = jnp.dot(jnp.concatenate([sp.astype(BF16) for sp in sps], axis=0), tri,
                            preferred_element_type=F32)
        for h, lanes in enumerate(heads):
            vb = v_ref[0, rows, lanes]
            sp, log_beta = sps[h], log_betas[h]
            inner = inner_all[h * nr:(h + 1) * nr, :]
            carry = car_ref[h, :nr]
            wgt = jnp.exp(log_beta - inner - carry)
            if diag:
                wgt = jnp.where(causal, wgt, 0.0)
            pv = jnp.dot(wgt.astype(BF16), vb, preferred_element_type=F32)
            if live is not None:
                pv = jnp.where(live, pv, 0.0)
            acc_ref[h, :nr] += pv
            carry = carry + jnp.sum(sp, axis=1, keepdims=True)
            car_ref[h, :nr] = carry
            carry_min = carry if carry_min is None else jnp.minimum(carry_min, carry)
        return jnp.min(carry_min)

    def cond(c):
        return jnp.logical_and(c[0] >= 0, c[1] < SB_ZERO_EXP)

    def sweep_rest(kj0, first_min, nr):
        lax.while_loop(cond, lambda c: (c[0] - 1, block(c[0], False, nr=nr)), (kj0, first_min))

    block(qi, True)
    first_min = block(jnp.maximum(qi - 1, 0), False, live=qi >= 1)
    half = blk // 2
    lower = car_ref[0, half:]
    for h in range(1, n_h):
        lower = jnp.minimum(lower, car_ref[h, half:])
    lax.cond(jnp.min(lower) >= SB_ZERO_EXP,
             lambda: sweep_rest(qi - 2, first_min, half),
             lambda: sweep_rest(qi - 2, first_min, blk))
    for h, lanes in enumerate(heads):
        o_ref[0, :, lanes] = acc_ref[h].astype(o_ref.dtype)


def _stick_breaking(qkv, tri, mix):
    b, s, _ = qkv.shape
    blk = min(SB_BLK, s)
    dh = SB_HEAD_DIM
    width = SB_HEADS_PER_STEP * dh
    n_grp = mix // width
    return pl.pallas_call(
        functools.partial(_sb_kernel, blk=blk, dh=dh, n_h=SB_HEADS_PER_STEP),
        grid=(b, n_grp, s // blk),
        in_specs=[
            pl.BlockSpec((1, blk, width), lambda i, g, t: (i, t, g)),
            pl.BlockSpec((1, s, width), lambda i, g, t: (i, 0, n_grp + g)),
            pl.BlockSpec((1, s, width), lambda i, g, t: (i, 0, 2 * n_grp + g)),
            pl.BlockSpec((blk, blk), lambda i, g, t: (0, 0)),
        ],
        out_specs=pl.BlockSpec((1, blk, width), lambda i, g, t: (i, t, g)),
        out_shape=jax.ShapeDtypeStruct((b, s, mix), BF16),
        scratch_shapes=[
            pltpu.VMEM((SB_HEADS_PER_STEP, blk, dh), F32),
            pltpu.VMEM((SB_HEADS_PER_STEP, blk, 1), F32),
        ],
        compiler_params=_params("parallel", "parallel", "arbitrary"),
        name="stick_breaking",
    )(qkv, qkv, qkv, tri)


def _hgrn2_kernel(hn_ref, wp_ref, lbl_ref, ng_ref, sel_ref, o_ref,
                  st_ref, vg_ref, qkb_ref, diag_ref, *, layer, tb, chunk, sub):
    n_b = hn_ref.shape[0]
    mix = o_ref.shape[-1]
    hk = HG_HEAD_DIM
    n_head = mix // hk
    n_sub = chunk // sub
    grp = SUBLANES
    n_grp = sub // grp

    @pl.when(pl.program_id(0) == 0)
    def _():
        st_ref[...] = jnp.zeros(st_ref.shape, F32)

    lbl = lbl_ref[...]
    p = jnp.exp(lbl - jnp.max(lbl, axis=0, keepdims=True))
    p = p / jnp.sum(p, axis=0, keepdims=True)
    lb = jnp.zeros((1, mix), F32)
    for m in range(1, layer + 1):
        lb = lb + p[m:m + 1, :]
    log_lb = jnp.log(jnp.maximum(lb, LB_FLOOR))
    log_1m_lb = jnp.log1p(-lb)

    row = lax.broadcasted_iota(jnp.int32, (chunk, chunk), 0)
    col = lax.broadcasted_iota(jnp.int32, (chunk, chunk), 1)
    same_sub = (row // sub) == (col // sub)
    later_sub = (col // sub) > (row // sub)
    causal = row <= col
    trow = lax.broadcasted_iota(jnp.int32, (chunk, mix), 0)

    def stage1(c, slot):
        rows = pl.ds(pl.multiple_of(c * chunk, chunk), chunk)
        proj = jnp.dot(jnp.concatenate([hn_ref[bi, rows, :] for bi in range(n_b)], axis=0),
                       wp_ref[...], preferred_element_type=F32)
        per_b = []
        span = None
        for bi in range(n_b):
            brows = slice(bi * chunk, (bi + 1) * chunk)
            q_raw = proj[brows, 0:mix]
            f_pre = proj[brows, mix:2 * mix]
            vg_ref[bi, rows, :] = proj[brows, 2 * mix:4 * mix]
            q = q_raw * jax.nn.sigmoid(q_raw)
            t2 = log_1m_lb - (jnp.maximum(-f_pre, 0.0) + _log1p_exp_neg_abs(f_pre))
            log_f = jnp.maximum(log_lb, t2) + _log1p_exp_neg_abs(log_lb - t2)
            k = (1.0 - lb) * jax.nn.sigmoid(-f_pre)
            bcum = log_f
            d = 1
            while d < chunk:
                bcum = bcum + jnp.where(trow >= d, pltpu.roll(bcum, d, 0), 0.0)
                d *= 2
            qkb_ref[slot, bi, 0] = q
            qkb_ref[slot, bi, 1] = k
            qkb_ref[slot, bi, 2] = bcum
            b3 = bcum.reshape(n_sub, sub, mix)
            q3 = q.reshape(n_sub, sub, mix)
            k3 = k.reshape(n_sub, sub, mix)
            ref3 = jnp.concatenate([jnp.zeros((1, 1, mix), F32), b3[:n_sub - 1, sub - 1:sub, :]], axis=0)
            drop = jnp.max(ref3 - b3[:, sub - 1:sub, :])
            span = drop if span is None else jnp.maximum(span, drop)
            per_b.append((q3, k3, b3, ref3))

        def factored():
            for bi, (q3, k3, b3, ref3) in enumerate(per_b):
                q_s = (q3 * jnp.exp(b3 - ref3)).reshape(chunk, mix).astype(BF16)
                k_s = (k3 * jnp.exp(ref3 - b3)).reshape(chunk, mix).astype(BF16)
                for hd in range(n_head):
                    hl = slice(hd * hk, (hd + 1) * hk)
                    diag_ref[slot, bi, hd] = lax.dot_general(
                        k_s[:, hl], q_s[:, hl], (((1,), (1,)), ((), ())), preferred_element_type=F32)

        def pairwise():
            for bi, (q3, k3, b3, ref3) in enumerate(per_b):
                b3s = b3 * LOG2_E
                tiles = tuple([] for _ in range(n_grp))
                for tl in range(sub):
                    b_t = b3s[:, tl:tl + 1, :]
                    q_t = q3[:, tl:tl + 1, :]
                    for gi in range(tl // grp + 1):
                        srows = slice(gi * grp, (gi + 1) * grp)
                        dec = jnp.exp2(jnp.minimum(b_t - b3s[:, srows, :], 0.0))
                        tile = (dec * (q_t * k3[:, srows, :])).reshape(n_sub * grp, mix).astype(BF16)
                        tiles[gi].append(tile)
                sums = []
                for gi in range(n_grp):
                    lhs = jnp.concatenate(
                        [jnp.concatenate([t[:, hd * hk:(hd + 1) * hk] for t in tiles[gi]], axis=1)
                         for hd in range(n_head)], axis=0)
                    sums.append(jnp.dot(lhs, sel_ref[gi * grp * hk:, :], preferred_element_type=F32))
                for hd in range(n_head):
                    r0 = hd * n_sub * grp
                    diag_ref[slot, bi, hd] = jnp.concatenate(
                        [sums[gi][r0 + j * grp:r0 + (j + 1) * grp, :]
                         for j in range(n_sub) for gi in range(n_grp)], axis=0)

        factored()
        pl.when(span > HG_SAFE_SPAN)(pairwise)

    def stage2(c, slot):
        rows = pl.ds(pl.multiple_of(c * chunk, chunk), chunk)
        for bi in range(n_b):
            q = qkb_ref[slot, bi, 0]
            k = qkb_ref[slot, bi, 1]
            bcum = qkb_ref[slot, bi, 2]
            b_last = bcum[chunk - 1:chunk, :]
            vb = vg_ref[bi, rows, 0:mix].astype(BF16)
            q_in = (q * jnp.exp(bcum)).astype(BF16)
            k_st = (k * jnp.exp(b_last - bcum)).astype(BF16)
            decay_last = jnp.exp(b_last)
            g_raw = vg_ref[bi, rows, mix:2 * mix]
            for hd in range(n_head):
                hl = slice(hd * hk, (hd + 1) * hk)
                at = jnp.where(same_sub & causal, diag_ref[slot, bi, hd], 0.0)
                k_rows, q_cols = [], []
                for j in range(n_sub - 1):
                    r_j = bcum[(j + 1) * sub - 1:(j + 1) * sub, hl]
                    srows = slice(j * sub, (j + 1) * sub)
                    trows = slice((j + 1) * sub, chunk)
                    k_hat = (k[srows, hl] * jnp.exp(r_j - bcum[srows, hl])).astype(BF16)
                    q_j = (q[trows, hl] * jnp.exp(bcum[trows, hl] - r_j)).astype(BF16)
                    q_cols.append(jnp.concatenate([jnp.zeros(((j + 1) * sub, hk), BF16), q_j], axis=0))
                    k_rows.append(jnp.concatenate(
                        [k_hat if jj == j else jnp.zeros((sub, hk), BF16) for jj in range(n_sub - 1)],
                        axis=1))
                if k_rows:
                    k_rows.append(jnp.zeros((sub, (n_sub - 1) * hk), BF16))
                    a_off = lax.dot_general(jnp.concatenate(k_rows, axis=0),
                                            jnp.concatenate(q_cols, axis=1),
                                            (((1,), (1,)), ((), ())), preferred_element_type=F32)
                    at = jnp.where(later_sub, a_off, at)
                st = st_ref[bi * n_head + hd]
                o = lax.dot_general(at.astype(BF16), vb[:, hl], (((0,), (0,)), ((), ())),
                                    preferred_element_type=F32)
                o = o + lax.dot_general(q_in[:, hl], st.astype(BF16), (((1,), (1,)), ((), ())),
                                        preferred_element_type=F32)
                st_ref[bi * n_head + hd] = decay_last[:, hl] * st + lax.dot_general(
                    vb[:, hl], k_st[:, hl], (((0,), (0,)), ((), ())), preferred_element_type=F32)
                o = _rms_scale(o, ng_ref[...])
                g_h = g_raw[:, hl]
                o_ref[bi, rows, hl] = (o * (g_h * jax.nn.sigmoid(g_h))).astype(o_ref.dtype)

    n_chunk = tb // chunk
    stage1(0, 0)

    def body(c, _):
        stage2(c - 1, lax.rem(c - 1, 2))
        stage1(c, lax.rem(c, 2))
        return 0

    lax.fori_loop(1, n_chunk, body, 0)
    stage2(n_chunk - 1, (n_chunk - 1) % 2)


def _hgrn2(hn, w_bf, col0, lb_logits, norm_gain, sel, layer):
    b, s, d = hn.shape
    mix = lb_logits.shape[1]
    wcb = col0 // (4 * mix)
    tb = min(HG_TB, s)
    chunk, sub = HG_CHUNK, HG_SUB
    assert sub % (2 * SUBLANES) == 0 and chunk % sub == 0 and tb % chunk == 0
    n_head = mix // HG_HEAD_DIM
    depth = lb_logits.shape[0]
    return pl.pallas_call(
        functools.partial(_hgrn2_kernel, layer=layer, tb=tb, chunk=chunk, sub=sub),
        grid=(s // tb,),
        in_specs=[
            pl.BlockSpec((b, tb, d), lambda j: (0, j, 0)),
            pl.BlockSpec((None, d, 4 * mix), lambda j: (layer, 0, wcb)),
            pl.BlockSpec((depth, mix), lambda j: (0, 0)),
            pl.BlockSpec((1, HG_HEAD_DIM), lambda j: (0, 0)),
            pl.BlockSpec((sub * HG_HEAD_DIM, chunk), lambda j: (0, 0)),
        ],
        out_specs=pl.BlockSpec((b, tb, mix), lambda j: (0, j, 0)),
        out_shape=jax.ShapeDtypeStruct((b, s, mix), BF16),
        scratch_shapes=[
            pltpu.VMEM((b * n_head, HG_HEAD_DIM, HG_HEAD_DIM), F32),
            pltpu.VMEM((b, tb, 2 * mix), F32),
            pltpu.VMEM((2, b, 3, chunk, mix), F32),
            pltpu.VMEM((2, b, n_head, chunk, chunk), F32),
        ],
        compiler_params=_params("arbitrary"),
        name="hgrn2",
    )(hn, w_bf, lb_logits, norm_gain, sel)


def _merge_kernel(ya_ref, yb_ref, yc_ref, hn_ref, x_ref, wga_ref, wgb_ref, wgc_ref,
                  wbr_ref, wout_ref, gain_ref, o_ref):
    hn = hn_ref[...]
    merged = None
    for idx, (y_ref, wg_ref) in enumerate(((ya_ref, wga_ref), (yb_ref, wgb_ref), (yc_ref, wgc_ref))):
        gate = jnp.dot(hn, wg_ref[...], preferred_element_type=F32)
        part = jnp.dot(y_ref[...].astype(BF16), wbr_ref[idx], preferred_element_type=F32)
        part = jax.nn.sigmoid(gate) * part
        merged = part if merged is None else merged + part
    out = jnp.dot(merged.astype(BF16), wout_ref[...], preferred_element_type=F32)
    o_ref[...] = x_ref[...] + _rms_scale(out, gain_ref[...])


def _merge(ya, yb, yc, hn2, x2, w_bf, layer, gate_col0, wbr_bf, wout_bf, gain):
    t, d = x2.shape
    mix = ya.shape[1]
    tm = min(MERGE_TM, t)
    gcb = gate_col0 // d
    ybs = lambda: pl.BlockSpec((tm, mix), lambda i: (i, 0))
    row = lambda: pl.BlockSpec((tm, d), lambda i: (i, 0))
    wgate = lambda off: pl.BlockSpec((None, d, d), lambda i: (layer, 0, gcb + off))
    return pl.pallas_call(
        _merge_kernel,
        grid=(t // tm,),
        in_specs=[
            ybs(), ybs(), ybs(), row(), row(),
            wgate(0), wgate(1), wgate(2),
            pl.BlockSpec((None,) + wbr_bf.shape[1:], lambda i: (layer, 0, 0, 0)),
            pl.BlockSpec((None, d, d), lambda i: (layer, 0, 0)),
            pl.BlockSpec((1, d), lambda i: (0, 0)),
        ],
        out_specs=row(),
        out_shape=jax.ShapeDtypeStruct((t, d), F32),
        compiler_params=_params("parallel"),
        name="merge_out",
    )(ya, yb, yc, hn2, x2, w_bf, w_bf, w_bf, wbr_bf, wout_bf, gain)


def _mlp_kernel(x_ref, g_in_ref, wup_ref, wdn_ref, g_out_ref, o_ref):
    x = x_ref[...]
    h = _rms_scale(x, g_in_ref[...]).astype(BF16)
    u = jnp.maximum(jnp.dot(h, wup_ref[...], preferred_element_type=F32), 0.0)
    m = jnp.dot((u * u).astype(BF16), wdn_ref[...], preferred_element_type=F32)
    o_ref[...] = x + _rms_scale(m, g_out_ref[...])


def _mlp(x2, g_in, wup_bf, wdn_bf, layer, g_out):
    t, d = x2.shape
    dff = wup_bf.shape[-1]
    tm = min(MLP_TM, t)
    resident = dict(pipeline_mode=pl.Buffered(1))
    return pl.pallas_call(
        _mlp_kernel,
        grid=(t // tm,),
        in_specs=[
            pl.BlockSpec((tm, d), lambda i: (i, 0)),
            pl.BlockSpec((1, d), lambda i: (0, 0)),
            pl.BlockSpec((None, d, dff), lambda i: (layer, 0, 0), **resident),
            pl.BlockSpec((None, dff, d), lambda i: (layer, 0, 0), **resident),
            pl.BlockSpec((1, d), lambda i: (0, 0)),
        ],
        out_specs=pl.BlockSpec((tm, d), lambda i: (i, 0)),
        out_shape=jax.ShapeDtypeStruct((t, d), F32),
        compiler_params=_params("parallel"),
        name="mlp",
    )(x2, g_in, wup_bf, wdn_bf, g_out)


def _block_diag(w):
    n, c, _ = w.shape
    eye = jnp.eye(n, dtype=w.dtype)
    return (eye[:, None, :, None] * w[:, :, None, :]).reshape(n * c, n * c)


def kernel(x, ln_gains, w_in, conv_w, conv_b, rg_w_a, rg_b_a, rg_w_x, rg_b_x, rg_lambda,
           lb_logits, hgrn_norm, w_branch, w_out, w_up, w_down):
    b, s, d = x.shape
    depth = w_in.shape[0]
    mix = conv_w.shape[-1]
    t = b * s
    hg_lo, gate_lo = 4 * mix, 8 * mix

    blk = min(SB_BLK, s)
    tri = (lax.broadcasted_iota(jnp.int32, (blk, blk), 0)
           > lax.broadcasted_iota(jnp.int32, (blk, blk), 1)).astype(BF16)
    sel_r = lax.broadcasted_iota(jnp.int32, (HG_SUB * HG_HEAD_DIM, HG_CHUNK), 0) // HG_HEAD_DIM
    sel_c = lax.broadcasted_iota(jnp.int32, (HG_SUB * HG_HEAD_DIM, HG_CHUNK), 1) % HG_SUB
    sel = (sel_r == sel_c).astype(BF16)

    w_in_bf, w_br_bf, w_out_bf = w_in.astype(BF16), w_branch.astype(BF16), w_out.astype(BF16)
    w_up_bf, w_dn_bf = w_up.astype(BF16), w_down.astype(BF16)

    x2 = x.reshape(t, d)
    for l in range(depth):
        g = ln_gains[l]
        y_a, hn, qkv = _norm_qkv_rglru(x2.reshape(b, s, d), g[0:1], w_in_bf, l, conv_w[l], conv_b[l:l + 1],
                                       _block_diag(rg_w_a[l]).astype(BF16), rg_b_a[l:l + 1],
                                       _block_diag(rg_w_x[l]).astype(BF16), rg_b_x[l:l + 1],
                                       rg_lambda[l:l + 1])
        y_b = _stick_breaking(qkv, tri, mix)
        y_c = _hgrn2(hn, w_in_bf, hg_lo, lb_logits, hgrn_norm[l:l + 1], sel, l)
        x2 = _merge(y_a.reshape(t, mix), y_b.reshape(t, mix), y_c.reshape(t, mix), hn.reshape(t, d), x2,
                    w_in_bf, l, gate_lo, w_br_bf, w_out_bf, g[1:2])
        x2 = _mlp(x2, g[2:3], w_up_bf, w_dn_bf, l, g[3:4])
    return x2.reshape(b, s, d)
```

```python
import functools

import jax
import jax.numpy as jnp
from jax import lax
from jax.experimental import pallas as pl
from jax.experimental.pallas import tpu as pltpu

F32 = jnp.float32
BF16 = jnp.bfloat16

EPS = 1e-6
RG_C = 8.0
CONV_WIDTH = 4
SB_HEAD_DIM = 64
HG_HEAD_DIM = 128
LB_FLOOR = 1e-30
LOG2_E = 1.4426950408889634
HG_SAFE_SPAN = 60.0
SB_ZERO_EXP = 105.0

SUBLANES = 8
VMEM_LIMIT = 56 * 1024 * 1024

RG_TB = 512
SB_BLK = 256
SB_HEADS_PER_STEP = 8
HG_TB = 512
HG_CHUNK = 64
HG_SUB = 16
MERGE_TM = 512
MLP_TM = 512


def _params(*sem):
    return pltpu.CompilerParams(dimension_semantics=sem, vmem_limit_bytes=VMEM_LIMIT)


def _softplus(x):
    return jnp.maximum(x, 0.0) + jnp.log1p(jnp.exp(-jnp.abs(x)))


def _log1p_exp_neg_abs(x):
    return jnp.log(1.0 + jnp.exp2(jnp.abs(x) * (-LOG2_E)))


def _rms_scale(x, gain):
    inv = lax.rsqrt(jnp.mean(x * x, axis=-1, keepdims=True) + EPS)
    return x * inv * gain


def _rglru_kernel(x_ref, g_ref, wp_ref, cw_ref, cb_ref, wa_ref, ba_ref, wx_ref, bx_ref, lam_ref,
                  o_ref, hn_ref, qkv_ref, tail_ref, a_ref, u_ref, h_ref, *, tb, q_scale):
    j = pl.program_id(0)
    n_b = x_ref.shape[0]
    w = cw_ref.shape[-1]

    @pl.when(j == 0)
    def _():
        tail_ref[...] = jnp.zeros(tail_ref.shape, F32)
        h_ref[...] = jnp.zeros(h_ref.shape, F32)

    sp_lam = _softplus(-lam_ref[...])
    row = lax.broadcasted_iota(jnp.int32, (tb, w), 0) + j * tb
    for bi in range(n_b):
        hn = _rms_scale(x_ref[bi], g_ref[...]).astype(BF16)
        hn_ref[bi] = hn
        x = jnp.dot(hn, wp_ref[:, :w], preferred_element_type=F32)
        qkv = jnp.dot(hn, wp_ref[:, w:], preferred_element_type=F32)
        qkv_ref[bi, :, :w] = (qkv[:, :w] * q_scale).astype(BF16)
        qkv_ref[bi, :, w:] = qkv[:, w:].astype(BF16)
        xp = jnp.concatenate([tail_ref[bi], x], axis=0)
        acc = cw_ref[0:1, :] * xp
        for k in range(1, CONV_WIDTH):
            acc = pltpu.roll(acc, 1, 0) + cw_ref[k:k + 1, :] * xp
        xc = acc[SUBLANES:, :] + cb_ref[...]
        tail_ref[bi] = x[tb - SUBLANES:tb, :]

        xcb = xc.astype(BF16)
        r = jax.nn.sigmoid(jnp.dot(xcb, wa_ref[...], preferred_element_type=F32) + ba_ref[...])
        gate_i = jax.nn.sigmoid(jnp.dot(xcb, wx_ref[...], preferred_element_type=F32) + bx_ref[...])
        a = jnp.exp(-RG_C * r * sp_lam)
        mult = jnp.sqrt(jnp.maximum(1.0 - a * a, 0.0))
        mult = jnp.where(row == 0, 1.0, mult)
        a_ref[bi] = a
        u_ref[bi] = mult * (gate_i * xc)

        h = h_ref[bi]
        for t in range(tb):
            h = a_ref[bi, t:t + 1, :] * h + u_ref[bi, t:t + 1, :]
            o_ref[bi, t:t + 1, :] = h
        h_ref[bi] = h


def _norm_qkv_rglru(x3, gain, w_bf, layer, cw, cb, wa_bd, ba, wx_bd, bx, lam):
    b, s, d = x3.shape
    mix = cw.shape[-1]
    tb = min(RG_TB, s)
    vec = lambda: pl.BlockSpec((1, mix), lambda j: (0, 0))
    blk3 = lambda width: pl.BlockSpec((b, tb, width), lambda j: (0, j, 0))
    return pl.pallas_call(
        functools.partial(_rglru_kernel, tb=tb, q_scale=SB_HEAD_DIM ** -0.5),
        grid=(s // tb,),
        in_specs=[
            blk3(d),
            pl.BlockSpec((1, d), lambda j: (0, 0)),
            pl.BlockSpec((None, d, 4 * mix), lambda j: (layer, 0, 0)),
            pl.BlockSpec((CONV_WIDTH, mix), lambda j: (0, 0)),
            vec(),
            pl.BlockSpec((mix, mix), lambda j: (0, 0)),
            vec(),
            pl.BlockSpec((mix, mix), lambda j: (0, 0)),
            vec(),
            vec(),
        ],
        out_specs=[blk3(mix), blk3(d), blk3(3 * mix)],
        out_shape=[jax.ShapeDtypeStruct((b, s, mix), F32),
                   jax.ShapeDtypeStruct((b, s, d), BF16),
                   jax.ShapeDtypeStruct((b, s, 3 * mix), BF16)],
        scratch_shapes=[
            pltpu.VMEM((b, SUBLANES, mix), F32),
            pltpu.VMEM((b, tb, mix), F32),
            pltpu.VMEM((b, tb, mix), F32),
            pltpu.VMEM((b, 1, mix), F32),
        ],
        compiler_params=_params("arbitrary"),
        name="norm_qkv_rglru",
    )(x3, gain, w_bf, cw, cb, wa_bd, ba, wx_bd, bx, lam)


def _sb_kernel(q_ref, k_ref, v_ref, tri_ref, o_ref, acc_ref, car_ref, *, blk, dh, n_h):
    qi = pl.program_id(2)
    row = lax.broadcasted_iota(jnp.int32, (blk, blk), 0)
    col = lax.broadcasted_iota(jnp.int32, (blk, blk), 1)
    causal = col < row
    tri = tri_ref[...]
    heads = [slice(h * dh, (h + 1) * dh) for h in range(n_h)]
    qs = [q_ref[0, :, lanes] for lanes in heads]

    acc_ref[...] = jnp.zeros(acc_ref.shape, F32)
    car_ref[...] = jnp.zeros(car_ref.shape, F32)

    def block(kj, diag, live=None, nr=blk):
        rows = pl.ds(pl.multiple_of(kj * blk, blk), blk)
        carry_min = None
        sps, log_betas = [], []
        for h, lanes in enumerate(heads):
            kb = k_ref[0, rows, lanes]
            z = lax.dot_general(qs[h][:nr], kb, (((1,), (1,)), ((), ())), preferred_element_type=F32)
            sp = jnp.maximum(z, 0.0) + _log1p_exp_neg_abs(z)
            log_betas.append(z - sp)
            if diag:
                sp = jnp.where(causal, sp, 0.0)
            sps.append(sp)
        inner_all = jnp.dot(jnp.concatenate([sp.astype(BF16) for sp in sps], axis=0), tri,
                            preferred_element_type=F32)
        for h, lanes in enumerate(heads):
            vb = v_ref[0, rows, lanes]
            sp, log_beta = sps[h], log_betas[h]
            inner = inner_all[h * nr:(h + 1) * nr, :]
            carry = car_ref[h, :nr]
            wgt = jnp.exp(log_beta - inner - carry)
            if diag:
                wgt = jnp.where(causal, wgt, 0.0)
            pv = jnp.dot(wgt.astype(BF16), vb, preferred_element_type=F32)
            if live is not None:
                pv = jnp.where(live, pv, 0.0)
            acc_ref[h, :nr] += pv
            carry = carry + jnp.sum(sp, axis=1, keepdims=True)
            car_ref[h, :nr] = carry
            carry_min = carry if carry_min is None else jnp.minimum(carry_min, carry)
        return jnp.min(carry_min)

    def cond(c):
        return jnp.logical_and(c[0] >= 0, c[1] < SB_ZERO_EXP)

    def sweep_rest(kj0, first_min, nr):
        lax.while_loop(cond, lambda c: (c[0] - 1, block(c[0], False, nr=nr)), (kj0, first_min))

    block(qi, True)
    first_min = block(jnp.maximum(qi - 1, 0), False, live=qi >= 1)
    half = blk // 2
    lower = car_ref[0, half:]
    for h in range(1, n_h):
        lower = jnp.minimum(lower, car_ref[h, half:])
    lax.cond(jnp.min(lower) >= SB_ZERO_EXP,
             lambda: sweep_rest(qi - 2, first_min, half),
             lambda: sweep_rest(qi - 2, first_min, blk))
    for h, lanes in enumerate(heads):
        o_ref[0, :, lanes] = acc_ref[h].astype(o_ref.dtype)


def _stick_breaking(qkv, tri, mix):
    b, s, _ = qkv.shape
    blk = min(SB_BLK, s)
    dh = SB_HEAD_DIM
    width = SB_HEADS_PER_STEP * dh
    n_grp = mix // width
    return pl.pallas_call(
        functools.partial(_sb_kernel, blk=blk, dh=dh, n_h=SB_HEADS_PER_STEP),
        grid=(b, n_grp, s // blk),
        in_specs=[
            pl.BlockSpec((1, blk, width), lambda i, g, t: (i, t, g)),
            pl.BlockSpec((1, s, width), lambda i, g, t: (i, 0, n_grp + g)),
            pl.BlockSpec((1, s, width), lambda i, g, t: (i, 0, 2 * n_grp + g)),
            pl.BlockSpec((blk, blk), lambda i, g, t: (0, 0)),
        ],
        out_specs=pl.BlockSpec((1, blk, width), lambda i, g, t: (i, t, g)),
        out_shape=jax.ShapeDtypeStruct((b, s, mix), BF16),
        scratch_shapes=[
            pltpu.VMEM((SB_HEADS_PER_STEP, blk, dh), F32),
            pltpu.VMEM((SB_HEADS_PER_STEP, blk, 1), F32),
        ],
        compiler_params=_params("parallel", "parallel", "arbitrary"),
        name="stick_breaking",
    )(qkv, qkv, qkv, tri)


def _hgrn2_kernel(hn_ref, wp_ref, lbl_ref, ng_ref, sel_ref, o_ref,
                  st_ref, vg_ref, qkb_ref, diag_ref, *, layer, tb, chunk, sub):
    n_b = hn_ref.shape[0]
    mix = o_ref.shape[-1]
    hk = HG_HEAD_DIM
    n_head = mix // hk
    n_sub = chunk // sub
    grp = SUBLANES
    n_grp = sub // grp

    @pl.when(pl.program_id(0) == 0)
    def _():
        st_ref[...] = jnp.zeros(st_ref.shape, F32)

    lbl = lbl_ref[...]
    p = jnp.exp(lbl - jnp.max(lbl, axis=0, keepdims=True))
    p = p / jnp.sum(p, axis=0, keepdims=True)
    lb = jnp.zeros((1, mix), F32)
    for m in range(1, layer + 1):
        lb = lb + p[m:m + 1, :]
    log_lb = jnp.log(jnp.maximum(lb, LB_FLOOR))
    log_1m_lb = jnp.log1p(-lb)

    row = lax.broadcasted_iota(jnp.int32, (chunk, chunk), 0)
    col = lax.broadcasted_iota(jnp.int32, (chunk, chunk), 1)
    same_sub = (row // sub) == (col // sub)
    later_sub = (col // sub) > (row // sub)
    causal = row <= col
    trow = lax.broadcasted_iota(jnp.int32, (chunk, mix), 0)

    def stage1(c, slot):
        rows = pl.ds(pl.multiple_of(c * chunk, chunk), chunk)
        proj = jnp.dot(jnp.concatenate([hn_ref[bi, rows, :] for bi in range(n_b)], axis=0),
                       wp_ref[...], preferred_element_type=F32)
        per_b = []
        span = None
        for bi in range(n_b):
            brows = slice(bi * chunk, (bi + 1) * chunk)
            q_raw = proj[brows, 0:mix]
            f_pre = proj[brows, mix:2 * mix]
            vg_ref[bi, rows, :] = proj[brows, 2 * mix:4 * mix]
            q = q_raw * jax.nn.sigmoid(q_raw)
            t2 = log_1m_lb - (jnp.maximum(-f_pre, 0.0) + _log1p_exp_neg_abs(f_pre))
            log_f = jnp.maximum(log_lb, t2) + _log1p_exp_neg_abs(log_lb - t2)
            k = (1.0 - lb) * jax.nn.sigmoid(-f_pre)
            bcum = log_f
            d = 1
            while d < chunk:
                bcum = bcum + jnp.where(trow >= d, pltpu.roll(bcum, d, 0), 0.0)
                d *= 2
            qkb_ref[slot, bi, 0] = q
            qkb_ref[slot, bi, 1] = k
            qkb_ref[slot, bi, 2] = bcum
            b3 = bcum.reshape(n_sub, sub, mix)
            q3 = q.reshape(n_sub, sub, mix)
            k3 = k.reshape(n_sub, sub, mix)
            ref3 = jnp.concatenate([jnp.zeros((1, 1, mix), F32), b3[:n_sub - 1, sub - 1:sub, :]], axis=0)
            drop = jnp.max(ref3 - b3[:, sub - 1:sub, :])
            span = drop if span is None else jnp.maximum(span, drop)
            per_b.append((q3, k3, b3, ref3))

        def factored():
            for bi, (q3, k3, b3, ref3) in enumerate(per_b):
                q_s = (q3 * jnp.exp(b3 - ref3)).reshape(chunk, mix).astype(BF16)
                k_s = (k3 * jnp.exp(ref3 - b3)).reshape(chunk, mix).astype(BF16)
                for hd in range(n_head):
                    hl = slice(hd * hk, (hd + 1) * hk)
                    diag_ref[slot, bi, hd] = lax.dot_general(
                        k_s[:, hl], q_s[:, hl], (((1,), (1,)), ((), ())), preferred_element_type=F32)

        def pairwise():
            for bi, (q3, k3, b3, ref3) in enumerate(per_b):
                b3s = b3 * LOG2_E
                tiles = tuple([] for _ in range(n_grp))
                for tl in range(sub):
                    b_t = b3s[:, tl:tl + 1, :]
                    q_t = q3[:, tl:tl + 1, :]
                    for gi in range(tl // grp + 1):
                        srows = slice(gi * grp, (gi + 1) * grp)
                        dec = jnp.exp2(jnp.minimum(b_t - b3s[:, srows, :], 0.0))
                        tile = (dec * (q_t * k3[:, srows, :])).reshape(n_sub * grp, mix).astype(BF16)
                        tiles[gi].append(tile)
                sums = []
                for gi in range(n_grp):
                    lhs = jnp.concatenate(
                        [jnp.concatenate([t[:, hd * hk:(hd + 1) * hk] for t in tiles[gi]], axis=1)
                         for hd in range(n_head)], axis=0)
                    sums.append(jnp.dot(lhs, sel_ref[gi * grp * hk:, :], preferred_element_type=F32))
                for hd in range(n_head):
                    r0 = hd * n_sub * grp
                    diag_ref[slot, bi, hd] = jnp.concatenate(
                        [sums[gi][r0 + j * grp:r0 + (j + 1) * grp, :]
                         for j in range(n_sub) for gi in range(n_grp)], axis=0)

        factored()
        pl.when(span > HG_SAFE_SPAN)(pairwise)

    def stage2(c, slot):
        rows = pl.ds(pl.multiple_of(c * chunk, chunk), chunk)
        for bi in range(n_b):
            q = qkb_ref[slot, bi, 0]
            k = qkb_ref[slot, bi, 1]
            bcum = qkb_ref[slot, bi, 2]
            b_last = bcum[chunk - 1:chunk, :]
            vb = vg_ref[bi, rows, 0:mix].astype(BF16)
            q_in = (q * jnp.exp(bcum)).astype(BF16)
            k_st = (k * jnp.exp(b_last - bcum)).astype(BF16)
            decay_last = jnp.exp(b_last)
            g_raw = vg_ref[bi, rows, mix:2 * mix]
            for hd in range(n_head):
                hl = slice(hd * hk, (hd + 1) * hk)
                at = jnp.where(same_sub & causal, diag_ref[slot, bi, hd], 0.0)
                k_rows, q_cols = [], []
                for j in range(n_sub - 1):
                    r_j = bcum[(j + 1) * sub - 1:(j + 1) * sub, hl]
                    srows = slice(j * sub, (j + 1) * sub)
                    trows = slice((j + 1) * sub, chunk)
                    k_hat = (k[srows, hl] * jnp.exp(r_j - bcum[srows, hl])).astype(BF16)
                    q_j = (q[trows, hl] * jnp.exp(bcum[trows, hl] - r_j)).astype(BF16)
                    q_cols.append(jnp.concatenate([jnp.zeros(((j + 1) * sub, hk), BF16), q_j], axis=0))
                    k_rows.append(jnp.concatenate(
                        [k_hat if jj == j else jnp.zeros((sub, hk), BF16) for jj in range(n_sub - 1)],
                        axis=1))
                if k_rows:
                    k_rows.append(jnp.zeros((sub, (n_sub - 1) * hk), BF16))
                    a_off = lax.dot_general(jnp.concatenate(k_rows, axis=0),
                                            jnp.concatenate(q_cols, axis=1),
                                            (((1,), (1,)), ((), ())), preferred_element_type=F32)
                    at = jnp.where(later_sub, a_off, at)
                st = st_ref[bi * n_head + hd]
                o = lax.dot_general(at.astype(BF16), vb[:, hl], (((0,), (0,)), ((), ())),
                                    preferred_element_type=F32)
                o = o + lax.dot_general(q_in[:, hl], st.astype(BF16), (((1,), (1,)), ((), ())),
                                        preferred_element_type=F32)
                st_ref[bi * n_head + hd] = decay_last[:, hl] * st + lax.dot_general(
                    vb[:, hl], k_st[:, hl], (((0,), (0,)), ((), ())), preferred_element_type=F32)
                o = _rms_scale(o, ng_ref[...])
                g_h = g_raw[:, hl]
                o_ref[bi, rows, hl] = (o * (g_h * jax.nn.sigmoid(g_h))).astype(o_ref.dtype)

    n_chunk = tb // chunk
    stage1(0, 0)

    def body(c, _):
        stage2(c - 1, lax.rem(c - 1, 2))
        stage1(c, lax.rem(c, 2))
        return 0

    lax.fori_loop(1, n_chunk, body, 0)
    stage2(n_chunk - 1, (n_chunk - 1) % 2)


def _hgrn2(hn, w_bf, col0, lb_logits, norm_gain, sel, layer):
    b, s, d = hn.shape
    mix = lb_logits.shape[1]
    wcb = col0 // (4 * mix)
    tb = min(HG_TB, s)
    chunk, sub = HG_CHUNK, HG_SUB
    assert sub % (2 * SUBLANES) == 0 and chunk % sub == 0 and tb % chunk == 0
    n_head = mix // HG_HEAD_DIM
    depth = lb_logits.shape[0]
    return pl.pallas_call(
        functools.partial(_hgrn2_kernel, layer=layer, tb=tb, chunk=chunk, sub=sub),
        grid=(s // tb,),
        in_specs=[
            pl.BlockSpec((b, tb, d), lambda j: (0, j, 0)),
            pl.BlockSpec((None, d, 4 * mix), lambda j: (layer, 0, wcb)),
            pl.BlockSpec((depth, mix), lambda j: (0, 0)),
            pl.BlockSpec((1, HG_HEAD_DIM), lambda j: (0, 0)),
            pl.BlockSpec((sub * HG_HEAD_DIM, chunk), lambda j: (0, 0)),
        ],
        out_specs=pl.BlockSpec((b, tb, mix), lambda j: (0, j, 0)),
        out_shape=jax.ShapeDtypeStruct((b, s, mix), BF16),
        scratch_shapes=[
            pltpu.VMEM((b * n_head, HG_HEAD_DIM, HG_HEAD_DIM), F32),
            pltpu.VMEM((b, tb, 2 * mix), F32),
            pltpu.VMEM((2, b, 3, chunk, mix), F32),
            pltpu.VMEM((2, b, n_head, chunk, chunk), F32),
        ],
        compiler_params=_params("arbitrary"),
        name="hgrn2",
    )(hn, w_bf, lb_logits, norm_gain, sel)


def _merge_kernel(ya_ref, yb_ref, yc_ref, hn_ref, x_ref, wga_ref, wgb_ref, wgc_ref,
                  wbr_ref, wout_ref, gain_ref, o_ref):
    hn = hn_ref[...]
    merged = None
    for idx, (y_ref, wg_ref) in enumerate(((ya_ref, wga_ref), (yb_ref, wgb_ref), (yc_ref, wgc_ref))):
        gate = jnp.dot(hn, wg_ref[...], preferred_element_type=F32)
        part = jnp.dot(y_ref[...].astype(BF16), wbr_ref[idx], preferred_element_type=F32)
        part = jax.nn.sigmoid(gate) * part
        merged = part if merged is None else merged + part
    out = jnp.dot(merged.astype(BF16), wout_ref[...], preferred_element_type=F32)
    o_ref[...] = x_ref[...] + _rms_scale(out, gain_ref[...])


def _merge(ya, yb, yc, hn2, x2, w_bf, layer, gate_col0, wbr_bf, wout_bf, gain):
    t, d = x2.shape
    mix = ya.shape[1]
    tm = min(MERGE_TM, t)
    gcb = gate_col0 // d
    ybs = lambda: pl.BlockSpec((tm, mix), lambda i: (i, 0))
    row = lambda: pl.BlockSpec((tm, d), lambda i: (i, 0))
    wgate = lambda off: pl.BlockSpec((None, d, d), lambda i: (layer, 0, gcb + off))
    return pl.pallas_call(
        _merge_kernel,
        grid=(t // tm,),
        in_specs=[
            ybs(), ybs(), ybs(), row(), row(),
            wgate(0), wgate(1), wgate(2),
            pl.BlockSpec((None,) + wbr_bf.shape[1:], lambda i: (layer, 0, 0, 0)),
            pl.BlockSpec((None, d, d), lambda i: (layer, 0, 0)),
            pl.BlockSpec((1, d), lambda i: (0, 0)),
        ],
        out_specs=row(),
        out_shape=jax.ShapeDtypeStruct((t, d), F32),
        compiler_params=_params("parallel"),
        name="merge_out",
    )(ya, yb, yc, hn2, x2, w_bf, w_bf, w_bf, wbr_bf, wout_bf, gain)


def _mlp_kernel(x_ref, g_in_ref, wup_ref, wdn_ref, g_out_ref, o_ref):
    x = x_ref[...]
    h = _rms_scale(x, g_in_ref[...]).astype(BF16)
    u = jnp.maximum(jnp.dot(h, wup_ref[...], preferred_element_type=F32), 0.0)
    m = jnp.dot((u * u).astype(BF16), wdn_ref[...], preferred_element_type=F32)
    o_ref[...] = x + _rms_scale(m, g_out_ref[...])


def _mlp(x2, g_in, wup_bf, wdn_bf, layer, g_out):
    t, d = x2.shape
    dff = wup_bf.shape[-1]
    tm = min(MLP_TM, t)
    resident = dict(pipeline_mode=pl.Buffered(1))
    return pl.pallas_call(
        _mlp_kernel,
        grid=(t // tm,),
        in_specs=[
            pl.BlockSpec((tm, d), lambda i: (i, 0)),
            pl.BlockSpec((1, d), lambda i: (0, 0)),
            pl.BlockSpec((None, d, dff), lambda i: (layer, 0, 0), **resident),
            pl.BlockSpec((None, dff, d), lambda i: (layer, 0, 0), **resident),
            pl.BlockSpec((1, d), lambda i: (0, 0)),
        ],
        out_specs=pl.BlockSpec((tm, d), lambda i: (i, 0)),
        out_shape=jax.ShapeDtypeStruct((t, d), F32),
        compiler_params=_params("parallel"),
        name="mlp",
    )(x2, g_in, wup_bf, wdn_bf, g_out)


def _block_diag(w):
    n, c, _ = w.shape
    eye = jnp.eye(n, dtype=w.dtype)
    return (eye[:, None, :, None] * w[:, :, None, :]).reshape(n * c, n * c)


def kernel(x, ln_gains, w_in, conv_w, conv_b, rg_w_a, rg_b_a, rg_w_x, rg_b_x, rg_lambda,
           lb_logits, hgrn_norm, w_branch, w_out, w_up, w_down):
    b, s, d = x.shape
    depth = w_in.shape[0]
    mix = conv_w.shape[-1]
    t = b * s
    hg_lo, gate_lo = 4 * mix, 8 * mix

    blk = min(SB_BLK, s)
    tri = (lax.broadcasted_iota(jnp.int32, (blk, blk), 0)
           > lax.broadcasted_iota(jnp.int32, (blk, blk), 1)).astype(BF16)
    sel_r = lax.broadcasted_iota(jnp.int32, (HG_SUB * HG_HEAD_DIM, HG_CHUNK), 0) // HG_HEAD_DIM
    sel_c = lax.broadcasted_iota(jnp.int32, (HG_SUB * HG_HEAD_DIM, HG_CHUNK), 1) % HG_SUB
    sel = (sel_r == sel_c).astype(BF16)

    w_in_bf, w_br_bf, w_out_bf = w_in.astype(BF16), w_branch.astype(BF16), w_out.astype(BF16)
    w_up_bf, w_dn_bf = w_up.astype(BF16), w_down.astype(BF16)

    x2 = x.reshape(t, d)
    for l in range(depth):
        g = ln_gains[l]
        y_a, hn, qkv = _norm_qkv_rglru(x2.reshape(b, s, d), g[0:1], w_in_bf, l, conv_w[l], conv_b[l:l + 1],
                                       _block_diag(rg_w_a[l]).astype(BF16), rg_b_a[l:l + 1],
                                       _block_diag(rg_w_x[l]).astype(BF16), rg_b_x[l:l + 1],
                                       rg_lambda[l:l + 1])
        y_b = _stick_breaking(qkv, tri, mix)
        y_c = _hgrn2(hn, w_in_bf, hg_lo, lb_logits, hgrn_norm[l:l + 1], sel, l)
        x2 = _merge(y_a.reshape(t, mix), y_b.reshape(t, mix), y_c.reshape(t, mix), hn.reshape(t, d), x2,
                    w_in_bf, l, gate_lo, w_br_bf, w_out_bf, g[1:2])
        x2 = _mlp(x2, g[2:3], w_up_bf, w_dn_bf, l, g[3:4])
    return x2.reshape(b, s, d)
```

```python
import functools

import jax
import jax.numpy as jnp
from jax import lax
from jax.experimental import pallas as pl
from jax.experimental.pallas import tpu as pltpu

F32 = jnp.float32
BF16 = jnp.bfloat16

EPS = 1e-6
RG_C = 8.0
CONV_WIDTH = 4
SB_HEAD_DIM = 64
HG_HEAD_DIM = 128
LB_FLOOR = 1e-30
LOG2_E = 1.4426950408889634
HG_SAFE_SPAN = 60.0
SB_ZERO_EXP = 105.0

SUBLANES = 8
VMEM_LIMIT = 56 * 1024 * 1024

RG_TB = 512
SB_BLK = 256
SB_HEADS_PER_STEP = 8
HG_TB = 512
HG_CHUNK = 64
HG_SUB = 16
MERGE_TM = 512
MLP_TM = 512


def _params(*sem):
    return pltpu.CompilerParams(dimension_semantics=sem, vmem_limit_bytes=VMEM_LIMIT)


def _softplus(x):
    return jnp.maximum(x, 0.0) + jnp.log1p(jnp.exp(-jnp.abs(x)))


def _log1p_exp_neg_abs(x):
    return jnp.log(1.0 + jnp.exp2(jnp.abs(x) * (-LOG2_E)))


def _rms_scale(x, gain):
    inv = lax.rsqrt(jnp.mean(x * x, axis=-1, keepdims=True) + EPS)
    return x * inv * gain


def _rglru_kernel(x_ref, g_ref, wp_ref, cw_ref, cb_ref, wa_ref, ba_ref, wx_ref, bx_ref, lam_ref,
                  o_ref, hn_ref, qkv_ref, tail_ref, a_ref, u_ref, h_ref, *, tb, q_scale):
    j = pl.program_id(0)
    n_b = x_ref.shape[0]
    w = cw_ref.shape[-1]

    @pl.when(j == 0)
    def _():
        tail_ref[...] = jnp.zeros(tail_ref.shape, F32)
        h_ref[...] = jnp.zeros(h_ref.shape, F32)

    sp_lam = _softplus(-lam_ref[...])
    row = lax.broadcasted_iota(jnp.int32, (tb, w), 0) + j * tb
    for bi in range(n_b):
        hn = _rms_scale(x_ref[bi], g_ref[...]).astype(BF16)
        hn_ref[bi] = hn
        x = jnp.dot(hn, wp_ref[:, :w], preferred_element_type=F32)
        qkv = jnp.dot(hn, wp_ref[:, w:], preferred_element_type=F32)
        qkv_ref[bi, :, :w] = (qkv[:, :w] * q_scale).astype(BF16)
        qkv_ref[bi, :, w:] = qkv[:, w:].astype(BF16)
        xp = jnp.concatenate([tail_ref[bi], x], axis=0)
        acc = cw_ref[0:1, :] * xp
        for k in range(1, CONV_WIDTH):
            acc = pltpu.roll(acc, 1, 0) + cw_ref[k:k + 1, :] * xp
        xc = acc[SUBLANES:, :] + cb_ref[...]
        tail_ref[bi] = x[tb - SUBLANES:tb, :]

        xcb = xc.astype(BF16)
        r = jax.nn.sigmoid(jnp.dot(xcb, wa_ref[...], preferred_element_type=F32) + ba_ref[...])
        gate_i = jax.nn.sigmoid(jnp.dot(xcb, wx_ref[...], preferred_element_type=F32) + bx_ref[...])
        a = jnp.exp(-RG_C * r * sp_lam)
        mult = jnp.sqrt(jnp.maximum(1.0 - a * a, 0.0))
        mult = jnp.where(row == 0, 1.0, mult)
        a_ref[bi] = a
        u_ref[bi] = mult * (gate_i * xc)

        h = h_ref[bi]
        for t in range(tb):
            h = a_ref[bi, t:t + 1, :] * h + u_ref[bi, t:t + 1, :]
            o_ref[bi, t:t + 1, :] = h
        h_ref[bi] = h


def _norm_qkv_rglru(x3, gain, w_bf, layer, cw, cb, wa_bd, ba, wx_bd, bx, lam):
    b, s, d = x3.shape
    mix = cw.shape[-1]
    tb = min(RG_TB, s)
    vec = lambda: pl.BlockSpec((1, mix), lambda j: (0, 0))
    blk3 = lambda width: pl.BlockSpec((b, tb, width), lambda j: (0, j, 0))
    return pl.pallas_call(
        functools.partial(_rglru_kernel, tb=tb, q_scale=SB_HEAD_DIM ** -0.5),
        grid=(s // tb,),
        in_specs=[
            blk3(d),
            pl.BlockSpec((1, d), lambda j: (0, 0)),
            pl.BlockSpec((None, d, 4 * mix), lambda j: (layer, 0, 0)),
            pl.BlockSpec((CONV_WIDTH, mix), lambda j: (0, 0)),
            vec(),
            pl.BlockSpec((mix, mix), lambda j: (0, 0)),
            vec(),
            pl.BlockSpec((mix, mix), lambda j: (0, 0)),
            vec(),
            vec(),
        ],
        out_specs=[blk3(mix), blk3(d), blk3(3 * mix)],
        out_shape=[jax.ShapeDtypeStruct((b, s, mix), F32),
                   jax.ShapeDtypeStruct((b, s, d), BF16),
                   jax.ShapeDtypeStruct((b, s, 3 * mix), BF16)],
        scratch_shapes=[
            pltpu.VMEM((b, SUBLANES, mix), F32),
            pltpu.VMEM((b, tb, mix), F32),
            pltpu.VMEM((b, tb, mix), F32),
            pltpu.VMEM((b, 1, mix), F32),
        ],
        compiler_params=_params("arbitrary"),
        name="norm_qkv_rglru",
    )(x3, gain, w_bf, cw, cb, wa_bd, ba, wx_bd, bx, lam)


def _sb_kernel(q_ref, k_ref, v_ref, tri_ref, o_ref, acc_ref, car_ref, *, blk, dh, n_h):
    qi = pl.program_id(2)
    row = lax.broadcasted_iota(jnp.int32, (blk, blk), 0)
    col = lax.broadcasted_iota(jnp.int32, (blk, blk), 1)
    causal = col < row
    tri = tri_ref[...]
    heads = [slice(h * dh, (h + 1) * dh) for h in range(n_h)]
    qs = [q_ref[0, :, lanes] for lanes in heads]

    acc_ref[...] = jnp.zeros(acc_ref.shape, F32)
    car_ref[...] = jnp.zeros(car_ref.shape, F32)

    def block(kj, diag, live=None, nr=blk):
        rows = pl.ds(pl.multiple_of(kj * blk, blk), blk)
        carry_min = None
        sps, log_betas = [], []
        for h, lanes in enumerate(heads):
            kb = k_ref[0, rows, lanes]
            z = lax.dot_general(qs[h][:nr], kb, (((1,), (1,)), ((), ())), preferred_element_type=F32)
            sp = jnp.maximum(z, 0.0) + _log1p_exp_neg_abs(z)
            log_betas.append(z - sp)
            if diag:
                sp = jnp.where(causal, sp, 0.0)
            sps.append(sp)
        inner_all = jnp.dot(jnp.concatenate([sp.astype(BF16) for sp in sps], axis=0), tri,
                            preferred_element_type=F32)
        for h, lanes in enumerate(heads):
            vb = v_ref[0, rows, lanes]
            sp, log_beta = sps[h], log_betas[h]
            inner = inner_all[h * nr:(h + 1) * nr, :]
            carry = car_ref[h, :nr]
            wgt = jnp.exp(log_beta - inner - carry)
            if diag:
                wgt = jnp.where(causal, wgt, 0.0)
            pv = jnp.dot(wgt.astype(BF16), vb, preferred_element_type=F32)
            if live is not None:
                pv = jnp.where(live, pv, 0.0)
            acc_ref[h, :nr] += pv
            carry = carry + jnp.sum(sp, axis=1, keepdims=True)
            car_ref[h, :nr] = carry
            carry_min = carry if carry_min is None else jnp.minimum(carry_min, carry)
        return jnp.min(carry_min)

    def cond(c):
        return jnp.logical_and(c[0] >= 0, c[1] < SB_ZERO_EXP)

    def sweep_rest(kj0, first_min, nr):
        lax.while_loop(cond, lambda c: (c[0] - 1, block(c[0], False, nr=nr)), (kj0, first_min))

    block(qi, True)
    first_min = block(jnp.maximum(qi - 1, 0), False, live=qi >= 1)
    half = blk // 2
    lower = car_ref[0, half:]
    for h in range(1, n_h):
        lower = jnp.minimum(lower, car_ref[h, half:])
    lax.cond(jnp.min(lower) >= SB_ZERO_EXP,
             lambda: sweep_rest(qi - 2, first_min, half),
             lambda: sweep_rest(qi - 2, first_min, blk))
    for h, lanes in enumerate(heads):
        o_ref[0, :, lanes] = acc_ref[h].astype(o_ref.dtype)


def _stick_breaking(qkv, tri, mix):
    b, s, _ = qkv.shape
    blk = min(SB_BLK, s)
    dh = SB_HEAD_DIM
    width = SB_HEADS_PER_STEP * dh
    n_grp = mix // width
    return pl.pallas_call(
        functools.partial(_sb_kernel, blk=blk, dh=dh, n_h=SB_HEADS_PER_STEP),
        grid=(b, n_grp, s // blk),
        in_specs=[
            pl.BlockSpec((1, blk, width), lambda i, g, t: (i, t, g)),
            pl.BlockSpec((1, s, width), lambda i, g, t: (i, 0, n_grp + g)),
            pl.BlockSpec((1, s, width), lambda i, g, t: (i, 0, 2 * n_grp + g)),
            pl.BlockSpec((blk, blk), lambda i, g, t: (0, 0)),
        ],
        out_specs=pl.BlockSpec((1, blk, width), lambda i, g, t: (i, t, g)),
        out_shape=jax.ShapeDtypeStruct((b, s, mix), BF16),
        scratch_shapes=[
            pltpu.VMEM((SB_HEADS_PER_STEP, blk, dh), F32),
            pltpu.VMEM((SB_HEADS_PER_STEP, blk, 1), F32),
        ],
        compiler_params=_params("parallel", "parallel", "arbitrary"),
        name="stick_breaking",
    )(qkv, qkv, qkv, tri)


def _hgrn2_kernel(hn_ref, wp_ref, lbl_ref, ng_ref, sel_ref, o_ref,
                  st_ref, vg_ref, qkb_ref, diag_ref, *, layer, tb, chunk, sub):
    n_b = hn_ref.shape[0]
    mix = o_ref.shape[-1]
    hk = HG_HEAD_DIM
    n_head = mix // hk
    n_sub = chunk // sub
    grp = SUBLANES
    n_grp = sub // grp

    @pl.when(pl.program_id(0) == 0)
    def _():
        st_ref[...] = jnp.zeros(st_ref.shape, F32)

    lbl = lbl_ref[...]
    p = jnp.exp(lbl - jnp.max(lbl, axis=0, keepdims=True))
    p = p / jnp.sum(p, axis=0, keepdims=True)
    lb = jnp.zeros((1, mix), F32)
    for m in range(1, layer + 1):
        lb = lb + p[m:m + 1, :]
    log_lb = jnp.log(jnp.maximum(lb, LB_FLOOR))
    log_1m_lb = jnp.log1p(-lb)

    row = lax.broadcasted_iota(jnp.int32, (chunk, chunk), 0)
    col = lax.broadcasted_iota(jnp.int32, (chunk, chunk), 1)
    same_sub = (row // sub) == (col // sub)
    later_sub = (row // sub) > (col // sub)
    causal = col <= row
    trow = lax.broadcasted_iota(jnp.int32, (chunk, mix), 0)

    def stage1(c, slot):
        rows = pl.ds(pl.multiple_of(c * chunk, chunk), chunk)
        proj = jnp.dot(jnp.concatenate([hn_ref[bi, rows, :] for bi in range(n_b)], axis=0),
                       wp_ref[...], preferred_element_type=F32)
        per_b = []
        span = None
        for bi in range(n_b):
            brows = slice(bi * chunk, (bi + 1) * chunk)
            q_raw = proj[brows, 0:mix]
            f_pre = proj[brows, mix:2 * mix]
            vg_ref[bi, rows, :] = proj[brows, 2 * mix:4 * mix]
            q = q_raw * jax.nn.sigmoid(q_raw)
            t2 = log_1m_lb - (jnp.maximum(-f_pre, 0.0) + _log1p_exp_neg_abs(f_pre))
            log_f = jnp.maximum(log_lb, t2) + _log1p_exp_neg_abs(log_lb - t2)
            k = (1.0 - lb) * jax.nn.sigmoid(-f_pre)
            bcum = log_f
            d = 1
            while d < chunk:
                bcum = bcum + jnp.where(trow >= d, pltpu.roll(bcum, d, 0), 0.0)
                d *= 2
            qkb_ref[slot, bi, 0] = q
            qkb_ref[slot, bi, 1] = k
            qkb_ref[slot, bi, 2] = bcum
            b3 = bcum.reshape(n_sub, sub, mix)
            q3 = q.reshape(n_sub, sub, mix)
            k3 = k.reshape(n_sub, sub, mix)
            ref3 = jnp.concatenate([jnp.zeros((1, 1, mix), F32), b3[:n_sub - 1, sub - 1:sub, :]], axis=0)
            drop = jnp.max(ref3 - b3[:, sub - 1:sub, :])
            span = drop if span is None else jnp.maximum(span, drop)
            per_b.append((q3, k3, b3, ref3))

        def factored():
            for bi, (q3, k3, b3, ref3) in enumerate(per_b):
                q_s = (q3 * jnp.exp(b3 - ref3)).reshape(chunk, mix).astype(BF16)
                k_s = (k3 * jnp.exp(ref3 - b3)).reshape(chunk, mix).astype(BF16)
                for hd in range(n_head):
                    hl = slice(hd * hk, (hd + 1) * hk)
                    diag_ref[slot, bi, hd] = lax.dot_general(
                        q_s[:, hl], k_s[:, hl], (((1,), (1,)), ((), ())), preferred_element_type=F32)

        def pairwise():
            for bi, (q3, k3, b3, ref3) in enumerate(per_b):
                b3s = b3 * LOG2_E
                tiles = tuple([] for _ in range(n_grp))
                for tl in range(sub):
                    b_t = b3s[:, tl:tl + 1, :]
                    q_t = q3[:, tl:tl + 1, :]
                    for gi in range(tl // grp + 1):
                        srows = slice(gi * grp, (gi + 1) * grp)
                        dec = jnp.exp2(jnp.minimum(b_t - b3s[:, srows, :], 0.0))
                        tile = (dec * (q_t * k3[:, srows, :])).reshape(n_sub * grp, mix).astype(BF16)
                        tiles[gi].append(tile)
                sums = []
                for gi in range(n_grp):
                    lhs = jnp.concatenate(
                        [jnp.concatenate([t[:, hd * hk:(hd + 1) * hk] for t in tiles[gi]], axis=1)
                         for hd in range(n_head)], axis=0)
                    sums.append(jnp.dot(lhs, sel_ref[gi * grp * hk:, :], preferred_element_type=F32))
                for hd in range(n_head):
                    r0 = hd * n_sub * grp
                    by_key = jnp.concatenate(
                        [sums[gi][r0 + j * grp:r0 + (j + 1) * grp, :]
                         for j in range(n_sub) for gi in range(n_grp)], axis=0)
                    diag_ref[slot, bi, hd] = by_key.T

        factored()
        pl.when(span > HG_SAFE_SPAN)(pairwise)

    def stage2(c, slot):
        rows = pl.ds(pl.multiple_of(c * chunk, chunk), chunk)
        for bi in range(n_b):
            q = qkb_ref[slot, bi, 0]
            k = qkb_ref[slot, bi, 1]
            bcum = qkb_ref[slot, bi, 2]
            b_last = bcum[chunk - 1:chunk, :]
            vb = vg_ref[bi, rows, 0:mix].astype(BF16)
            q_in = (q * jnp.exp(bcum)).astype(BF16)
            k_st = (k * jnp.exp(b_last - bcum)).astype(BF16)
            decay_last = jnp.exp(b_last)
            g_raw = vg_ref[bi, rows, mix:2 * mix]
            for hd in range(n_head):
                hl = slice(hd * hk, (hd + 1) * hk)
                aw = jnp.where(same_sub & causal, diag_ref[slot, bi, hd], 0.0)
                k_rows, q_cols = [], []
                for j in range(n_sub - 1):
                    r_j = bcum[(j + 1) * sub - 1:(j + 1) * sub, hl]
                    srows = slice(j * sub, (j + 1) * sub)
                    trows = slice((j + 1) * sub, chunk)
                    k_hat = (k[srows, hl] * jnp.exp(r_j - bcum[srows, hl])).astype(BF16)
                    q_j = (q[trows, hl] * jnp.exp(bcum[trows, hl] - r_j)).astype(BF16)
                    q_cols.append(jnp.concatenate([jnp.zeros(((j + 1) * sub, hk), BF16), q_j], axis=0))
                    k_rows.append(jnp.concatenate(
                        [k_hat if jj == j else jnp.zeros((sub, hk), BF16) for jj in range(n_sub - 1)],
                        axis=1))
                if k_rows:
                    k_rows.append(jnp.zeros((sub, (n_sub - 1) * hk), BF16))
                    a_off = lax.dot_general(jnp.concatenate(q_cols, axis=1),
                                            jnp.concatenate(k_rows, axis=0),
                                            (((1,), (1,)), ((), ())), preferred_element_type=F32)
                    aw = jnp.where(later_sub, a_off, aw)
                st = st_ref[bi * n_head + hd]
                o = jnp.dot(aw.astype(BF16), vb[:, hl], preferred_element_type=F32)
                o = o + lax.dot_general(q_in[:, hl], st.astype(BF16), (((1,), (1,)), ((), ())),
                                        preferred_element_type=F32)
                st_ref[bi * n_head + hd] = decay_last[:, hl] * st + lax.dot_general(
                    vb[:, hl], k_st[:, hl], (((0,), (0,)), ((), ())), preferred_element_type=F32)
                o = _rms_scale(o, ng_ref[...])
                g_h = g_raw[:, hl]
                o_ref[bi, rows, hl] = (o * (g_h * jax.nn.sigmoid(g_h))).astype(o_ref.dtype)

    n_chunk = tb // chunk
    stage1(0, 0)

    def body(c, _):
        stage2(c - 1, lax.rem(c - 1, 2))
        stage1(c, lax.rem(c, 2))
        return 0

    lax.fori_loop(1, n_chunk, body, 0)
    stage2(n_chunk - 1, (n_chunk - 1) % 2)


def _hgrn2(hn, w_bf, col0, lb_logits, norm_gain, sel, layer):
    b, s, d = hn.shape
    mix = lb_logits.shape[1]
    wcb = col0 // (4 * mix)
    tb = min(HG_TB, s)
    chunk, sub = HG_CHUNK, HG_SUB
    assert sub % (2 * SUBLANES) == 0 and chunk % sub == 0 and tb % chunk == 0
    n_head = mix // HG_HEAD_DIM
    depth = lb_logits.shape[0]
    return pl.pallas_call(
        functools.partial(_hgrn2_kernel, layer=layer, tb=tb, chunk=chunk, sub=sub),
        grid=(s // tb,),
        in_specs=[
            pl.BlockSpec((b, tb, d), lambda j: (0, j, 0)),
            pl.BlockSpec((None, d, 4 * mix), lambda j: (layer, 0, wcb)),
            pl.BlockSpec((depth, mix), lambda j: (0, 0)),
            pl.BlockSpec((1, HG_HEAD_DIM), lambda j: (0, 0)),
            pl.BlockSpec((sub * HG_HEAD_DIM, chunk), lambda j: (0, 0)),
        ],
        out_specs=pl.BlockSpec((b, tb, mix), lambda j: (0, j, 0)),
        out_shape=jax.ShapeDtypeStruct((b, s, mix), BF16),
        scratch_shapes=[
            pltpu.VMEM((b * n_head, HG_HEAD_DIM, HG_HEAD_DIM), F32),
            pltpu.VMEM((b, tb, 2 * mix), F32),
            pltpu.VMEM((2, b, 3, chunk, mix), F32),
            pltpu.VMEM((2, b, n_head, chunk, chunk), F32),
        ],
        compiler_params=_params("arbitrary"),
        name="hgrn2",
    )(hn, w_bf, lb_logits, norm_gain, sel)


def _merge_kernel(ya_ref, yb_ref, yc_ref, hn_ref, x_ref, wga_ref, wgb_ref, wgc_ref,
                  wbr_ref, wout_ref, gain_ref, o_ref):
    hn = hn_ref[...]
    merged = None
    for idx, (y_ref, wg_ref) in enumerate(((ya_ref, wga_ref), (yb_ref, wgb_ref), (yc_ref, wgc_ref))):
        gate = jnp.dot(hn, wg_ref[...], preferred_element_type=F32)
        part = jnp.dot(y_ref[...].astype(BF16), wbr_ref[idx], preferred_element_type=F32)
        part = jax.nn.sigmoid(gate) * part
        merged = part if merged is None else merged + part
    out = jnp.dot(merged.astype(BF16), wout_ref[...], preferred_element_type=F32)
    o_ref[...] = x_ref[...] + _rms_scale(out, gain_ref[...])


def _merge(ya, yb, yc, hn2, x2, w_bf, layer, gate_col0, wbr_bf, wout_bf, gain):
    t, d = x2.shape
    mix = ya.shape[1]
    tm = min(MERGE_TM, t)
    gcb = gate_col0 // d
    ybs = lambda: pl.BlockSpec((tm, mix), lambda i: (i, 0))
    row = lambda: pl.BlockSpec((tm, d), lambda i: (i, 0))
    wgate = lambda off: pl.BlockSpec((None, d, d), lambda i: (layer, 0, gcb + off))
    return pl.pallas_call(
        _merge_kernel,
        grid=(t // tm,),
        in_specs=[
            ybs(), ybs(), ybs(), row(), row(),
            wgate(0), wgate(1), wgate(2),
            pl.BlockSpec((None,) + wbr_bf.shape[1:], lambda i: (layer, 0, 0, 0)),
            pl.BlockSpec((None, d, d), lambda i: (layer, 0, 0)),
            pl.BlockSpec((1, d), lambda i: (0, 0)),
        ],
        out_specs=row(),
        out_shape=jax.ShapeDtypeStruct((t, d), F32),
        compiler_params=_params("parallel"),
        name="merge_out",
    )(ya, yb, yc, hn2, x2, w_bf, w_bf, w_bf, wbr_bf, wout_bf, gain)


def _mlp_kernel(x_ref, g_in_ref, wup_ref, wdn_ref, g_out_ref, o_ref):
    x = x_ref[...]
    h = _rms_scale(x, g_in_ref[...]).astype(BF16)
    u = jnp.maximum(jnp.dot(h, wup_ref[...], preferred_element_type=F32), 0.0)
    m = jnp.dot((u * u).astype(BF16), wdn_ref[...], preferred_element_type=F32)
    o_ref[...] = x + _rms_scale(m, g_out_ref[...])


def _mlp(x2, g_in, wup_bf, wdn_bf, layer, g_out):
    t, d = x2.shape
    dff = wup_bf.shape[-1]
    tm = min(MLP_TM, t)
    resident = dict(pipeline_mode=pl.Buffered(1))
    return pl.pallas_call(
        _mlp_kernel,
        grid=(t // tm,),
        in_specs=[
            pl.BlockSpec((tm, d), lambda i: (i, 0)),
            pl.BlockSpec((1, d), lambda i: (0, 0)),
            pl.BlockSpec((None, d, dff), lambda i: (layer, 0, 0), **resident),
            pl.BlockSpec((None, dff, d), lambda i: (layer, 0, 0), **resident),
            pl.BlockSpec((1, d), lambda i: (0, 0)),
        ],
        out_specs=pl.BlockSpec((tm, d), lambda i: (i, 0)),
        out_shape=jax.ShapeDtypeStruct((t, d), F32),
        compiler_params=_params("parallel"),
        name="mlp",
    )(x2, g_in, wup_bf, wdn_bf, g_out)


def _block_diag(w):
    n, c, _ = w.shape
    eye = jnp.eye(n, dtype=w.dtype)
    return (eye[:, None, :, None] * w[:, :, None, :]).reshape(n * c, n * c)


def kernel(x, ln_gains, w_in, conv_w, conv_b, rg_w_a, rg_b_a, rg_w_x, rg_b_x, rg_lambda,
           lb_logits, hgrn_norm, w_branch, w_out, w_up, w_down):
    b, s, d = x.shape
    depth = w_in.shape[0]
    mix = conv_w.shape[-1]
    t = b * s
    hg_lo, gate_lo = 4 * mix, 8 * mix

    blk = min(SB_BLK, s)
    tri = (lax.broadcasted_iota(jnp.int32, (blk, blk), 0)
           > lax.broadcasted_iota(jnp.int32, (blk, blk), 1)).astype(BF16)
    sel_r = lax.broadcasted_iota(jnp.int32, (HG_SUB * HG_HEAD_DIM, HG_CHUNK), 0) // HG_HEAD_DIM
    sel_c = lax.broadcasted_iota(jnp.int32, (HG_SUB * HG_HEAD_DIM, HG_CHUNK), 1) % HG_SUB
    sel = (sel_r == sel_c).astype(BF16)

    w_in_bf, w_br_bf, w_out_bf = w_in.astype(BF16), w_branch.astype(BF16), w_out.astype(BF16)
    w_up_bf, w_dn_bf = w_up.astype(BF16), w_down.astype(BF16)

    x2 = x.reshape(t, d)
    for l in range(depth):
        g = ln_gains[l]
        y_a, hn, qkv = _norm_qkv_rglru(x2.reshape(b, s, d), g[0:1], w_in_bf, l, conv_w[l], conv_b[l:l + 1],
                                       _block_diag(rg_w_a[l]).astype(BF16), rg_b_a[l:l + 1],
                                       _block_diag(rg_w_x[l]).astype(BF16), rg_b_x[l:l + 1],
                                       rg_lambda[l:l + 1])
        y_b = _stick_breaking(qkv, tri, mix)
        y_c = _hgrn2(hn, w_in_bf, hg_lo, lb_logits, hgrn_norm[l:l + 1], sel, l)
        x2 = _merge(y_a.reshape(t, mix), y_b.reshape(t, mix), y_c.reshape(t, mix), hn.reshape(t, d), x2,
                    w_in_bf, l, gate_lo, w_br_bf, w_out_bf, g[1:2])
        x2 = _mlp(x2, g[2:3], w_up_bf, w_dn_bf, l, g[3:4])
    return x2.reshape(b, s, d)
```

```python
import functools

import jax
import jax.numpy as jnp
from jax import lax
from jax.experimental import pallas as pl
from jax.experimental.pallas import tpu as pltpu

F32 = jnp.float32
BF16 = jnp.bfloat16

EPS = 1e-6
RG_C = 8.0
CONV_WIDTH = 4
SB_HEAD_DIM = 64
HG_HEAD_DIM = 128
LB_FLOOR = 1e-30
LOG2_E = 1.4426950408889634
HG_SAFE_SPAN = 60.0
SB_ZERO_EXP = 105.0

SUBLANES = 8
VMEM_LIMIT = 56 * 1024 * 1024

RG_TB = 512
QKV_PIECE = 256
SB_BLK = 256
SB_HEADS_PER_STEP = 8
HG_TB = 512
HG_CHUNK = 64
HG_SUB = 16
MERGE_TM = 512
MLP_TM = 512


def _params(*sem):
    return pltpu.CompilerParams(dimension_semantics=sem, vmem_limit_bytes=VMEM_LIMIT)


def _softplus(x):
    return jnp.maximum(x, 0.0) + jnp.log1p(jnp.exp(-jnp.abs(x)))


def _log1p_exp_neg_abs(x):
    return jnp.log(1.0 + jnp.exp2(jnp.abs(x) * (-LOG2_E)))


def _zero_after(v):
    bits = lax.bitcast_convert_type(v, jnp.uint32)
    bits = lax.shift_right_logical(lax.shift_right_logical(bits, jnp.uint32(16)), jnp.uint32(16))
    return lax.bitcast_convert_type(bits, F32)


def _rms_scale(x, gain):
    inv = lax.rsqrt(jnp.mean(x * x, axis=-1, keepdims=True) + EPS)
    return x * inv * gain


def _rglru_kernel(x_ref, g_ref, wp_ref, cw_ref, cb_ref, wa_ref, ba_ref, wx_ref, bx_ref, lam_ref,
                  o_ref, hn_ref, qkv_ref, tail_ref, a_ref, u_ref, h_ref, *, tb, q_scale):
    j = pl.program_id(0)
    n_b = x_ref.shape[0]
    w = cw_ref.shape[-1]

    @pl.when(j == 0)
    def _():
        tail_ref[...] = jnp.zeros(tail_ref.shape, F32)
        h_ref[...] = jnp.zeros(h_ref.shape, F32)

    sp_lam = _softplus(-lam_ref[...])
    row = lax.broadcasted_iota(jnp.int32, (tb, w), 0) + j * tb
    for bi in range(n_b):
        hn = _rms_scale(x_ref[bi], g_ref[...]).astype(BF16)
        hn_ref[bi] = hn
        x = jnp.dot(hn, wp_ref[:, :w], preferred_element_type=F32)
        xp = jnp.concatenate([tail_ref[bi], x], axis=0)
        acc = cw_ref[0:1, :] * xp
        for k in range(1, CONV_WIDTH):
            acc = pltpu.roll(acc, 1, 0) + cw_ref[k:k + 1, :] * xp
        xc = acc[SUBLANES:, :] + cb_ref[...]
        tail_ref[bi] = x[tb - SUBLANES:tb, :]

        xcb = xc.astype(BF16)
        r = jax.nn.sigmoid(jnp.dot(xcb, wa_ref[...], preferred_element_type=F32) + ba_ref[...])
        gate_i = jax.nn.sigmoid(jnp.dot(xcb, wx_ref[...], preferred_element_type=F32) + bx_ref[...])
        a = jnp.exp(-RG_C * r * sp_lam)
        mult = jnp.sqrt(jnp.maximum(1.0 - a * a, 0.0))
        mult = jnp.where(row == 0, 1.0, mult)
        a_ref[bi] = a
        u_ref[bi] = mult * (gate_i * xc)

        h = h_ref[bi]
        n_piece = 3 * w // QKV_PIECE
        seg = max(tb // (n_piece + 1), 1)
        marks = []
        for t in range(tb):
            h = a_ref[bi, t:t + 1, :] * h + u_ref[bi, t:t + 1, :]
            o_ref[bi, t:t + 1, :] = h
            if t % seg == 0 and len(marks) < n_piece:
                marks.append(h)
        h_ref[bi] = h

        for n in range(n_piece):
            c0 = n * QKV_PIECE
            zero = jnp.concatenate([_zero_after(marks[n])] * (x_ref.shape[-1] // w), axis=1)
            head = (hn[:2 * SUBLANES].astype(F32) + zero).astype(BF16)
            lhs = jnp.concatenate([head, hn[2 * SUBLANES:]], axis=0)
            piece = jnp.dot(lhs, wp_ref[:, w + c0:w + c0 + QKV_PIECE], preferred_element_type=F32)
            if c0 < w:
                piece = piece * q_scale
            qkv_ref[bi, :, c0:c0 + QKV_PIECE] = piece.astype(BF16)


def _norm_qkv_rglru(x3, gain, w_bf, layer, cw, cb, wa_bd, ba, wx_bd, bx, lam):
    b, s, d = x3.shape
    mix = cw.shape[-1]
    tb = min(RG_TB, s)
    vec = lambda: pl.BlockSpec((1, mix), lambda j: (0, 0))
    blk3 = lambda width: pl.BlockSpec((b, tb, width), lambda j: (0, j, 0))
    return pl.pallas_call(
        functools.partial(_rglru_kernel, tb=tb, q_scale=SB_HEAD_DIM ** -0.5),
        grid=(s // tb,),
        in_specs=[
            blk3(d),
            pl.BlockSpec((1, d), lambda j: (0, 0)),
            pl.BlockSpec((None, d, 4 * mix), lambda j: (layer, 0, 0)),
            pl.BlockSpec((CONV_WIDTH, mix), lambda j: (0, 0)),
            vec(),
            pl.BlockSpec((mix, mix), lambda j: (0, 0)),
            vec(),
            pl.BlockSpec((mix, mix), lambda j: (0, 0)),
            vec(),
            vec(),
        ],
        out_specs=[blk3(mix), blk3(d), blk3(3 * mix)],
        out_shape=[jax.ShapeDtypeStruct((b, s, mix), F32),
                   jax.ShapeDtypeStruct((b, s, d), BF16),
                   jax.ShapeDtypeStruct((b, s, 3 * mix), BF16)],
        scratch_shapes=[
            pltpu.VMEM((b, SUBLANES, mix), F32),
            pltpu.VMEM((b, tb, mix), F32),
            pltpu.VMEM((b, tb, mix), F32),
            pltpu.VMEM((b, 1, mix), F32),
        ],
        compiler_params=_params("arbitrary"),
        name="norm_qkv_rglru",
    )(x3, gain, w_bf, cw, cb, wa_bd, ba, wx_bd, bx, lam)


def _sb_kernel(q_ref, k_ref, v_ref, tri_ref, o_ref, acc_ref, car_ref, *, blk, dh, n_h):
    qi = pl.program_id(2)
    row = lax.broadcasted_iota(jnp.int32, (blk, blk), 0)
    col = lax.broadcasted_iota(jnp.int32, (blk, blk), 1)
    causal = col < row
    tri = tri_ref[...]
    heads = [slice(h * dh, (h + 1) * dh) for h in range(n_h)]
    qs = [q_ref[0, :, lanes] for lanes in heads]

    acc_ref[...] = jnp.zeros(acc_ref.shape, F32)
    car_ref[...] = jnp.zeros(car_ref.shape, F32)

    def block(kj, diag, live=None, nr=blk):
        rows = pl.ds(pl.multiple_of(kj * blk, blk), blk)
        carry_min = None
        sps, log_betas = [], []
        for h, lanes in enumerate(heads):
            kb = k_ref[0, rows, lanes]
            z = lax.dot_general(qs[h][:nr], kb, (((1,), (1,)), ((), ())), preferred_element_type=F32)
            sp = jnp.maximum(z, 0.0) + _log1p_exp_neg_abs(z)
            log_betas.append(z - sp)
            if diag:
                sp = jnp.where(causal, sp, 0.0)
            sps.append(sp)
        inner_all = jnp.dot(jnp.concatenate([sp.astype(BF16) for sp in sps], axis=0), tri,
                            preferred_element_type=F32)
        for h, lanes in enumerate(heads):
            vb = v_ref[0, rows, lanes]
            sp, log_beta = sps[h], log_betas[h]
            inner = inner_all[h * nr:(h + 1) * nr, :]
            carry = car_ref[h, :nr]
            wgt = jnp.exp(log_beta - inner - carry)
            if diag:
                wgt = jnp.where(causal, wgt, 0.0)
            pv = jnp.dot(wgt.astype(BF16), vb, preferred_element_type=F32)
            if live is not None:
                pv = jnp.where(live, pv, 0.0)
            acc_ref[h, :nr] += pv
            carry = carry + jnp.sum(sp, axis=1, keepdims=True)
            car_ref[h, :nr] = carry
            carry_min = carry if carry_min is None else jnp.minimum(carry_min, carry)
        return jnp.min(carry_min)

    def cond(c):
        return jnp.logical_and(c[0] >= 0, c[1] < SB_ZERO_EXP)

    def sweep_rest(kj0, first_min, nr):
        lax.while_loop(cond, lambda c: (c[0] - 1, block(c[0], False, nr=nr)), (kj0, first_min))

    block(qi, True)
    first_min = block(jnp.maximum(qi - 1, 0), False, live=qi >= 1)
    half = blk // 2
    lower = car_ref[0, half:]
    for h in range(1, n_h):
        lower = jnp.minimum(lower, car_ref[h, half:])
    lax.cond(jnp.min(lower) >= SB_ZERO_EXP,
             lambda: sweep_rest(qi - 2, first_min, half),
             lambda: sweep_rest(qi - 2, first_min, blk))
    for h, lanes in enumerate(heads):
        o_ref[0, :, lanes] = acc_ref[h].astype(o_ref.dtype)


def _stick_breaking(qkv, tri, mix):
    b, s, _ = qkv.shape
    blk = min(SB_BLK, s)
    dh = SB_HEAD_DIM
    width = SB_HEADS_PER_STEP * dh
    n_grp = mix // width
    return pl.pallas_call(
        functools.partial(_sb_kernel, blk=blk, dh=dh, n_h=SB_HEADS_PER_STEP),
        grid=(b, n_grp, s // blk),
        in_specs=[
            pl.BlockSpec((1, blk, width), lambda i, g, t: (i, t, g)),
            pl.BlockSpec((1, s, width), lambda i, g, t: (i, 0, n_grp + g)),
            pl.BlockSpec((1, s, width), lambda i, g, t: (i, 0, 2 * n_grp + g)),
            pl.BlockSpec((blk, blk), lambda i, g, t: (0, 0)),
        ],
        out_specs=pl.BlockSpec((1, blk, width), lambda i, g, t: (i, t, g)),
        out_shape=jax.ShapeDtypeStruct((b, s, mix), BF16),
        scratch_shapes=[
            pltpu.VMEM((SB_HEADS_PER_STEP, blk, dh), F32),
            pltpu.VMEM((SB_HEADS_PER_STEP, blk, 1), F32),
        ],
        compiler_params=_params("parallel", "parallel", "arbitrary"),
        name="stick_breaking",
    )(qkv, qkv, qkv, tri)


def _hgrn2_kernel(hn_ref, wp_ref, lbl_ref, ng_ref, sel_ref, o_ref,
                  st_ref, vg_ref, qkb_ref, diag_ref, *, layer, tb, chunk, sub):
    n_b = hn_ref.shape[0]
    mix = o_ref.shape[-1]
    hk = HG_HEAD_DIM
    n_head = mix // hk
    n_sub = chunk // sub
    grp = SUBLANES
    n_grp = sub // grp

    @pl.when(pl.program_id(0) == 0)
    def _():
        st_ref[...] = jnp.zeros(st_ref.shape, F32)

    lbl = lbl_ref[...]
    p = jnp.exp(lbl - jnp.max(lbl, axis=0, keepdims=True))
    p = p / jnp.sum(p, axis=0, keepdims=True)
    lb = jnp.zeros((1, mix), F32)
    for m in range(1, layer + 1):
        lb = lb + p[m:m + 1, :]
    log_lb = jnp.log(jnp.maximum(lb, LB_FLOOR))
    log_1m_lb = jnp.log1p(-lb)

    row = lax.broadcasted_iota(jnp.int32, (chunk, chunk), 0)
    col = lax.broadcasted_iota(jnp.int32, (chunk, chunk), 1)
    same_sub = (row // sub) == (col // sub)
    later_sub = (row // sub) > (col // sub)
    causal = col <= row
    trow = lax.broadcasted_iota(jnp.int32, (chunk, mix), 0)

    def stage1(c, slot):
        rows = pl.ds(pl.multiple_of(c * chunk, chunk), chunk)
        proj = jnp.dot(jnp.concatenate([hn_ref[bi, rows, :] for bi in range(n_b)], axis=0),
                       wp_ref[...], preferred_element_type=F32)
        per_b = []
        span = None
        for bi in range(n_b):
            brows = slice(bi * chunk, (bi + 1) * chunk)
            q_raw = proj[brows, 0:mix]
            f_pre = proj[brows, mix:2 * mix]
            vg_ref[bi, rows, :] = proj[brows, 2 * mix:4 * mix]
            q = q_raw * jax.nn.sigmoid(q_raw)
            t2 = log_1m_lb - (jnp.maximum(-f_pre, 0.0) + _log1p_exp_neg_abs(f_pre))
            log_f = jnp.maximum(log_lb, t2) + _log1p_exp_neg_abs(log_lb - t2)
            k = (1.0 - lb) * jax.nn.sigmoid(-f_pre)
            bcum = log_f
            d = 1
            while d < chunk:
                bcum = bcum + jnp.where(trow >= d, pltpu.roll(bcum, d, 0), 0.0)
                d *= 2
            qkb_ref[slot, bi, 0] = q
            qkb_ref[slot, bi, 1] = k
            qkb_ref[slot, bi, 2] = bcum
            b3 = bcum.reshape(n_sub, sub, mix)
            q3 = q.reshape(n_sub, sub, mix)
            k3 = k.reshape(n_sub, sub, mix)
            ref3 = jnp.concatenate([jnp.zeros((1, 1, mix), F32), b3[:n_sub - 1, sub - 1:sub, :]], axis=0)
            drop = jnp.max(ref3 - b3[:, sub - 1:sub, :])
            span = drop if span is None else jnp.maximum(span, drop)
            per_b.append((q3, k3, b3, ref3))

        def factored():
            for bi, (q3, k3, b3, ref3) in enumerate(per_b):
                q_s = (q3 * jnp.exp(b3 - ref3)).reshape(chunk, mix).astype(BF16)
                k_s = (k3 * jnp.exp(ref3 - b3)).reshape(chunk, mix).astype(BF16)
                for hd in range(n_head):
                    hl = slice(hd * hk, (hd + 1) * hk)
                    diag_ref[slot, bi, hd] = lax.dot_general(
                        q_s[:, hl], k_s[:, hl], (((1,), (1,)), ((), ())), preferred_element_type=F32)

        def pairwise():
            for bi, (q3, k3, b3, ref3) in enumerate(per_b):
                b3s = b3 * LOG2_E
                tiles = tuple([] for _ in range(n_grp))
                for tl in range(sub):
                    b_t = b3s[:, tl:tl + 1, :]
                    q_t = q3[:, tl:tl + 1, :]
                    for gi in range(tl // grp + 1):
                        srows = slice(gi * grp, (gi + 1) * grp)
                        dec = jnp.exp2(jnp.minimum(b_t - b3s[:, srows, :], 0.0))
                        tile = (dec * (q_t * k3[:, srows, :])).reshape(n_sub * grp, mix).astype(BF16)
                        tiles[gi].append(tile)
                sums = []
                for gi in range(n_grp):
                    lhs = jnp.concatenate(
                        [jnp.concatenate([t[:, hd * hk:(hd + 1) * hk] for t in tiles[gi]], axis=1)
                         for hd in range(n_head)], axis=0)
                    sums.append(jnp.dot(lhs, sel_ref[gi * grp * hk:, :], preferred_element_type=F32))
                for hd in range(n_head):
                    r0 = hd * n_sub * grp
                    by_key = jnp.concatenate(
                        [sums[gi][r0 + j * grp:r0 + (j + 1) * grp, :]
                         for j in range(n_sub) for gi in range(n_grp)], axis=0)
                    diag_ref[slot, bi, hd] = by_key.T

        factored()
        pl.when(span > HG_SAFE_SPAN)(pairwise)

    def stage2(c, slot):
        rows = pl.ds(pl.multiple_of(c * chunk, chunk), chunk)
        for bi in range(n_b):
            q = qkb_ref[slot, bi, 0]
            k = qkb_ref[slot, bi, 1]
            bcum = qkb_ref[slot, bi, 2]
            b_last = bcum[chunk - 1:chunk, :]
            vb = vg_ref[bi, rows, 0:mix].astype(BF16)
            q_in = (q * jnp.exp(bcum)).astype(BF16)
            k_st = (k * jnp.exp(b_last - bcum)).astype(BF16)
            decay_last = jnp.exp(b_last)
            g_raw = vg_ref[bi, rows, mix:2 * mix]
            for hd in range(n_head):
                hl = slice(hd * hk, (hd + 1) * hk)
                aw = jnp.where(same_sub & causal, diag_ref[slot, bi, hd], 0.0)
                k_rows, q_cols = [], []
                for j in range(n_sub - 1):
                    r_j = bcum[(j + 1) * sub - 1:(j + 1) * sub, hl]
                    srows = slice(j * sub, (j + 1) * sub)
                    trows = slice((j + 1) * sub, chunk)
                    k_hat = (k[srows, hl] * jnp.exp(r_j - bcum[srows, hl])).astype(BF16)
                    q_j = (q[trows, hl] * jnp.exp(bcum[trows, hl] - r_j)).astype(BF16)
                    q_cols.append(jnp.concatenate([jnp.zeros(((j + 1) * sub, hk), BF16), q_j], axis=0))
                    k_rows.append(jnp.concatenate(
                        [k_hat if jj == j else jnp.zeros((sub, hk), BF16) for jj in range(n_sub - 1)],
                        axis=1))
                if k_rows:
                    k_rows.append(jnp.zeros((sub, (n_sub - 1) * hk), BF16))
                    a_off = lax.dot_general(jnp.concatenate(q_cols, axis=1),
                                            jnp.concatenate(k_rows, axis=0),
                                            (((1,), (1,)), ((), ())), preferred_element_type=F32)
                    aw = jnp.where(later_sub, a_off, aw)
                st = st_ref[bi * n_head + hd]
                o = jnp.dot(aw.astype(BF16), vb[:, hl], preferred_element_type=F32)
                o = o + lax.dot_general(q_in[:, hl], st.astype(BF16), (((1,), (1,)), ((), ())),
                                        preferred_element_type=F32)
                st_ref[bi * n_head + hd] = decay_last[:, hl] * st + lax.dot_general(
                    vb[:, hl], k_st[:, hl], (((0,), (0,)), ((), ())), preferred_element_type=F32)
                o = _rms_scale(o, ng_ref[...])
                g_h = g_raw[:, hl]
                o_ref[bi, rows, hl] = (o * (g_h * jax.nn.sigmoid(g_h))).astype(o_ref.dtype)

    n_chunk = tb // chunk
    stage1(0, 0)

    def body(c, _):
        stage2(c - 1, lax.rem(c - 1, 2))
        stage1(c, lax.rem(c, 2))
        return 0

    lax.fori_loop(1, n_chunk, body, 0)
    stage2(n_chunk - 1, (n_chunk - 1) % 2)


def _hgrn2(hn, w_bf, col0, lb_logits, norm_gain, sel, layer):
    b, s, d = hn.shape
    mix = lb_logits.shape[1]
    wcb = col0 // (4 * mix)
    tb = min(HG_TB, s)
    chunk, sub = HG_CHUNK, HG_SUB
    assert sub % (2 * SUBLANES) == 0 and chunk % sub == 0 and tb % chunk == 0
    n_head = mix // HG_HEAD_DIM
    depth = lb_logits.shape[0]
    return pl.pallas_call(
        functools.partial(_hgrn2_kernel, layer=layer, tb=tb, chunk=chunk, sub=sub),
        grid=(s // tb,),
        in_specs=[
            pl.BlockSpec((b, tb, d), lambda j: (0, j, 0)),
            pl.BlockSpec((None, d, 4 * mix), lambda j: (layer, 0, wcb)),
            pl.BlockSpec((depth, mix), lambda j: (0, 0)),
            pl.BlockSpec((1, HG_HEAD_DIM), lambda j: (0, 0)),
            pl.BlockSpec((sub * HG_HEAD_DIM, chunk), lambda j: (0, 0)),
        ],
        out_specs=pl.BlockSpec((b, tb, mix), lambda j: (0, j, 0)),
        out_shape=jax.ShapeDtypeStruct((b, s, mix), BF16),
        scratch_shapes=[
            pltpu.VMEM((b * n_head, HG_HEAD_DIM, HG_HEAD_DIM), F32),
            pltpu.VMEM((b, tb, 2 * mix), F32),
            pltpu.VMEM((2, b, 3, chunk, mix), F32),
            pltpu.VMEM((2, b, n_head, chunk, chunk), F32),
        ],
        compiler_params=_params("arbitrary"),
        name="hgrn2",
    )(hn, w_bf, lb_logits, norm_gain, sel)


def _merge_kernel(ya_ref, yb_ref, yc_ref, hn_ref, x_ref, wga_ref, wgb_ref, wgc_ref,
                  wbr_ref, wout_ref, gain_ref, o_ref):
    hn = hn_ref[...]
    merged = None
    for idx, (y_ref, wg_ref) in enumerate(((ya_ref, wga_ref), (yb_ref, wgb_ref), (yc_ref, wgc_ref))):
        gate = jnp.dot(hn, wg_ref[...], preferred_element_type=F32)
        part = jnp.dot(y_ref[...].astype(BF16), wbr_ref[idx], preferred_element_type=F32)
        part = jax.nn.sigmoid(gate) * part
        merged = part if merged is None else merged + part
    out = jnp.dot(merged.astype(BF16), wout_ref[...], preferred_element_type=F32)
    o_ref[...] = x_ref[...] + _rms_scale(out, gain_ref[...])


def _merge(ya, yb, yc, hn2, x2, w_bf, layer, gate_col0, wbr_bf, wout_bf, gain):
    t, d = x2.shape
    mix = ya.shape[1]
    tm = min(MERGE_TM, t)
    gcb = gate_col0 // d
    ybs = lambda: pl.BlockSpec((tm, mix), lambda i: (i, 0))
    row = lambda: pl.BlockSpec((tm, d), lambda i: (i, 0))
    wgate = lambda off: pl.BlockSpec((None, d, d), lambda i: (layer, 0, gcb + off))
    return pl.pallas_call(
        _merge_kernel,
        grid=(t // tm,),
        in_specs=[
            ybs(), ybs(), ybs(), row(), row(),
            wgate(0), wgate(1), wgate(2),
            pl.BlockSpec((None,) + wbr_bf.shape[1:], lambda i: (layer, 0, 0, 0)),
            pl.BlockSpec((None, d, d), lambda i: (layer, 0, 0)),
            pl.BlockSpec((1, d), lambda i: (0, 0)),
        ],
        out_specs=row(),
        out_shape=jax.ShapeDtypeStruct((t, d), F32),
        compiler_params=_params("parallel"),
        name="merge_out",
    )(ya, yb, yc, hn2, x2, w_bf, w_bf, w_bf, wbr_bf, wout_bf, gain)


def _mlp_kernel(x_ref, g_in_ref, wup_ref, wdn_ref, g_out_ref, o_ref):
    x = x_ref[...]
    h = _rms_scale(x, g_in_ref[...]).astype(BF16)
    u = jnp.maximum(jnp.dot(h, wup_ref[...], preferred_element_type=F32), 0.0)
    m = jnp.dot((u * u).astype(BF16), wdn_ref[...], preferred_element_type=F32)
    o_ref[...] = x + _rms_scale(m, g_out_ref[...])


def _mlp(x2, g_in, wup_bf, wdn_bf, layer, g_out):
    t, d = x2.shape
    dff = wup_bf.shape[-1]
    tm = min(MLP_TM, t)
    resident = dict(pipeline_mode=pl.Buffered(1))
    return pl.pallas_call(
        _mlp_kernel,
        grid=(t // tm,),
        in_specs=[
            pl.BlockSpec((tm, d), lambda i: (i, 0)),
            pl.BlockSpec((1, d), lambda i: (0, 0)),
            pl.BlockSpec((None, d, dff), lambda i: (layer, 0, 0), **resident),
            pl.BlockSpec((None, dff, d), lambda i: (layer, 0, 0), **resident),
            pl.BlockSpec((1, d), lambda i: (0, 0)),
        ],
        out_specs=pl.BlockSpec((tm, d), lambda i: (i, 0)),
        out_shape=jax.ShapeDtypeStruct((t, d), F32),
        compiler_params=_params("parallel"),
        name="mlp",
    )(x2, g_in, wup_bf, wdn_bf, g_out)


def _block_diag(w):
    n, c, _ = w.shape
    eye = jnp.eye(n, dtype=w.dtype)
    return (eye[:, None, :, None] * w[:, :, None, :]).reshape(n * c, n * c)


def kernel(x, ln_gains, w_in, conv_w, conv_b, rg_w_a, rg_b_a, rg_w_x, rg_b_x, rg_lambda,
           lb_logits, hgrn_norm, w_branch, w_out, w_up, w_down):
    b, s, d = x.shape
    depth = w_in.shape[0]
    mix = conv_w.shape[-1]
    t = b * s
    hg_lo, gate_lo = 4 * mix, 8 * mix

    blk = min(SB_BLK, s)
    tri = (lax.broadcasted_iota(jnp.int32, (blk, blk), 0)
           > lax.broadcasted_iota(jnp.int32, (blk, blk), 1)).astype(BF16)
    sel_r = lax.broadcasted_iota(jnp.int32, (HG_SUB * HG_HEAD_DIM, HG_CHUNK), 0) // HG_HEAD_DIM
    sel_c = lax.broadcasted_iota(jnp.int32, (HG_SUB * HG_HEAD_DIM, HG_CHUNK), 1) % HG_SUB
    sel = (sel_r == sel_c).astype(BF16)

    w_in_bf, w_br_bf, w_out_bf = w_in.astype(BF16), w_branch.astype(BF16), w_out.astype(BF16)
    w_up_bf, w_dn_bf = w_up.astype(BF16), w_down.astype(BF16)

    x2 = x.reshape(t, d)
    for l in range(depth):
        g = ln_gains[l]
        y_a, hn, qkv = _norm_qkv_rglru(x2.reshape(b, s, d), g[0:1], w_in_bf, l, conv_w[l], conv_b[l:l + 1],
                                       _block_diag(rg_w_a[l]).astype(BF16), rg_b_a[l:l + 1],
                                       _block_diag(rg_w_x[l]).astype(BF16), rg_b_x[l:l + 1],
                                       rg_lambda[l:l + 1])
        y_b = _stick_breaking(qkv, tri, mix)
        y_c = _hgrn2(hn, w_in_bf, hg_lo, lb_logits, hgrn_norm[l:l + 1], sel, l)
        x2 = _merge(y_a.reshape(t, mix), y_b.reshape(t, mix), y_c.reshape(t, mix), hn.reshape(t, d), x2,
                    w_in_bf, l, gate_lo, w_br_bf, w_out_bf, g[1:2])
        x2 = _mlp(x2, g[2:3], w_up_bf, w_dn_bf, l, g[3:4])
    return x2.reshape(b, s, d)
```

```python
import functools

import jax
import jax.numpy as jnp
from jax import lax
from jax.experimental import pallas as pl
from jax.experimental.pallas import tpu as pltpu

F32 = jnp.float32
BF16 = jnp.bfloat16

EPS = 1e-6
RG_C = 8.0
CONV_WIDTH = 4
SB_HEAD_DIM = 64
HG_HEAD_DIM = 128
LB_FLOOR = 1e-30
LOG2_E = 1.4426950408889634
HG_SAFE_SPAN = 60.0
SB_ZERO_EXP = 105.0

SUBLANES = 8
VMEM_LIMIT = 56 * 1024 * 1024

RG_TB = 512
QKV_PIECE = 256
SB_BLK = 256
SB_HEADS_PER_STEP = 8
HG_TB = 512
HG_CHUNK = 64
HG_SUB = 16
MERGE_TM = 512
MLP_TM = 512


def _params(*sem):
    return pltpu.CompilerParams(dimension_semantics=sem, vmem_limit_bytes=VMEM_LIMIT)


def _softplus(x):
    return jnp.maximum(x, 0.0) + jnp.log1p(jnp.exp(-jnp.abs(x)))


def _log1p_exp_neg_abs(x):
    return jnp.log(1.0 + jnp.exp2(jnp.abs(x) * (-LOG2_E)))


def _rms_scale(x, gain):
    inv = lax.rsqrt(jnp.mean(x * x, axis=-1, keepdims=True) + EPS)
    return x * inv * gain


def _rglru_kernel(x_ref, g_ref, wp_ref, cw_ref, cb_ref, wa_ref, ba_ref, wx_ref, bx_ref, lam_ref,
                  o_ref, hn_ref, qkv_ref, tail_ref, a_ref, u_ref, h_ref, *, tb, q_scale):
    j = pl.program_id(0)
    n_b = x_ref.shape[0]
    w = cw_ref.shape[-1]

    @pl.when(j == 0)
    def _():
        tail_ref[...] = jnp.zeros(tail_ref.shape, F32)
        h_ref[...] = jnp.zeros(h_ref.shape, F32)

    sp_lam = _softplus(-lam_ref[...])
    row = lax.broadcasted_iota(jnp.int32, (tb, w), 0) + j * tb
    for bi in range(n_b):
        hn = _rms_scale(x_ref[bi], g_ref[...]).astype(BF16)
        hn_ref[bi] = hn
        x = jnp.dot(hn, wp_ref[:, :w], preferred_element_type=F32)

        def qkv_piece(n, hn=hn, bi=bi):
            c0 = n * QKV_PIECE
            piece = jnp.dot(hn, wp_ref[:, w + c0:w + c0 + QKV_PIECE], preferred_element_type=F32)
            if c0 < w:
                piece = piece * q_scale
            qkv_ref[bi, :, c0:c0 + QKV_PIECE] = piece.astype(BF16)

        qkv_piece(0)
        xp = jnp.concatenate([tail_ref[bi], x], axis=0)
        acc = cw_ref[0:1, :] * xp
        for k in range(1, CONV_WIDTH):
            acc = pltpu.roll(acc, 1, 0) + cw_ref[k:k + 1, :] * xp
        xc = acc[SUBLANES:, :] + cb_ref[...]
        tail_ref[bi] = x[tb - SUBLANES:tb, :]

        xcb = xc.astype(BF16)
        r_pre = jnp.dot(xcb, wa_ref[...], preferred_element_type=F32)
        i_pre = jnp.dot(xcb, wx_ref[...], preferred_element_type=F32)
        qkv_piece(1)
        qkv_piece(2)
        r = jax.nn.sigmoid(r_pre + ba_ref[...])
        gate_i = jax.nn.sigmoid(i_pre + bx_ref[...])
        a = jnp.exp(-RG_C * r * sp_lam)
        mult = jnp.sqrt(jnp.maximum(1.0 - a * a, 0.0))
        mult = jnp.where(row == 0, 1.0, mult)
        a_ref[bi] = a
        u_ref[bi] = mult * (gate_i * xc)

        h = h_ref[bi]
        for t in range(tb):
            h = a_ref[bi, t:t + 1, :] * h + u_ref[bi, t:t + 1, :]
            o_ref[bi, t:t + 1, :] = h
        h_ref[bi] = h
        for n in range(3, 3 * w // QKV_PIECE):
            qkv_piece(n)


def _norm_qkv_rglru(x3, gain, w_bf, layer, cw, cb, wa_bd, ba, wx_bd, bx, lam):
    b, s, d = x3.shape
    mix = cw.shape[-1]
    tb = min(RG_TB, s)
    vec = lambda: pl.BlockSpec((1, mix), lambda j: (0, 0))
    blk3 = lambda width: pl.BlockSpec((b, tb, width), lambda j: (0, j, 0))
    return pl.pallas_call(
        functools.partial(_rglru_kernel, tb=tb, q_scale=SB_HEAD_DIM ** -0.5),
        grid=(s // tb,),
        in_specs=[
            blk3(d),
            pl.BlockSpec((1, d), lambda j: (0, 0)),
            pl.BlockSpec((None, d, 4 * mix), lambda j: (layer, 0, 0)),
            pl.BlockSpec((CONV_WIDTH, mix), lambda j: (0, 0)),
            vec(),
            pl.BlockSpec((mix, mix), lambda j: (0, 0)),
            vec(),
            pl.BlockSpec((mix, mix), lambda j: (0, 0)),
            vec(),
            vec(),
        ],
        out_specs=[blk3(mix), blk3(d), blk3(3 * mix)],
        out_shape=[jax.ShapeDtypeStruct((b, s, mix), F32),
                   jax.ShapeDtypeStruct((b, s, d), BF16),
                   jax.ShapeDtypeStruct((b, s, 3 * mix), BF16)],
        scratch_shapes=[
            pltpu.VMEM((b, SUBLANES, mix), F32),
            pltpu.VMEM((b, tb, mix), F32),
            pltpu.VMEM((b, tb, mix), F32),
            pltpu.VMEM((b, 1, mix), F32),
        ],
        compiler_params=_params("arbitrary"),
        name="norm_qkv_rglru",
    )(x3, gain, w_bf, cw, cb, wa_bd, ba, wx_bd, bx, lam)


def _sb_kernel(q_ref, k_ref, v_ref, tri_ref, o_ref, acc_ref, car_ref, *, blk, dh, n_h):
    qi = pl.program_id(2)
    row = lax.broadcasted_iota(jnp.int32, (blk, blk), 0)
    col = lax.broadcasted_iota(jnp.int32, (blk, blk), 1)
    causal = col < row
    tri = tri_ref[...]
    heads = [slice(h * dh, (h + 1) * dh) for h in range(n_h)]
    qs = [q_ref[0, :, lanes] for lanes in heads]

    acc_ref[...] = jnp.zeros(acc_ref.shape, F32)
    car_ref[...] = jnp.zeros(car_ref.shape, F32)

    def block(kj, diag, live=None, nr=blk):
        rows = pl.ds(pl.multiple_of(kj * blk, blk), blk)
        carry_min = None
        sps, log_betas = [], []
        for h, lanes in enumerate(heads):
            kb = k_ref[0, rows, lanes]
            z = lax.dot_general(qs[h][:nr], kb, (((1,), (1,)), ((), ())), preferred_element_type=F32)
            sp = jnp.maximum(z, 0.0) + _log1p_exp_neg_abs(z)
            log_betas.append(z - sp)
            if diag:
                sp = jnp.where(causal, sp, 0.0)
            sps.append(sp)
        inner_all = jnp.dot(jnp.concatenate([sp.astype(BF16) for sp in sps], axis=0), tri,
                            preferred_element_type=F32)
        for h, lanes in enumerate(heads):
            vb = v_ref[0, rows, lanes]
            sp, log_beta = sps[h], log_betas[h]
            inner = inner_all[h * nr:(h + 1) * nr, :]
            carry = car_ref[h, :nr]
            wgt = jnp.exp(log_beta - inner - carry)
            if diag:
                wgt = jnp.where(causal, wgt, 0.0)
            pv = jnp.dot(wgt.astype(BF16), vb, preferred_element_type=F32)
            if live is not None:
                pv = jnp.where(live, pv, 0.0)
            acc_ref[h, :nr] += pv
            carry = carry + jnp.sum(sp, axis=1, keepdims=True)
            car_ref[h, :nr] = carry
            carry_min = carry if carry_min is None else jnp.minimum(carry_min, carry)
        return jnp.min(carry_min)

    def cond(c):
        return jnp.logical_and(c[0] >= 0, c[1] < SB_ZERO_EXP)

    def sweep_rest(kj0, first_min, nr):
        lax.while_loop(cond, lambda c: (c[0] - 1, block(c[0], False, nr=nr)), (kj0, first_min))

    block(qi, True)
    first_min = block(jnp.maximum(qi - 1, 0), False, live=qi >= 1)
    half = blk // 2
    lower = car_ref[0, half:]
    for h in range(1, n_h):
        lower = jnp.minimum(lower, car_ref[h, half:])
    lax.cond(jnp.min(lower) >= SB_ZERO_EXP,
             lambda: sweep_rest(qi - 2, first_min, half),
             lambda: sweep_rest(qi - 2, first_min, blk))
    for h, lanes in enumerate(heads):
        o_ref[0, :, lanes] = acc_ref[h].astype(o_ref.dtype)


def _stick_breaking(qkv, tri, mix):
    b, s, _ = qkv.shape
    blk = min(SB_BLK, s)
    dh = SB_HEAD_DIM
    width = SB_HEADS_PER_STEP * dh
    n_grp = mix // width
    return pl.pallas_call(
        functools.partial(_sb_kernel, blk=blk, dh=dh, n_h=SB_HEADS_PER_STEP),
        grid=(b, n_grp, s // blk),
        in_specs=[
            pl.BlockSpec((1, blk, width), lambda i, g, t: (i, t, g)),
            pl.BlockSpec((1, s, width), lambda i, g, t: (i, 0, n_grp + g)),
            pl.BlockSpec((1, s, width), lambda i, g, t: (i, 0, 2 * n_grp + g)),
            pl.BlockSpec((blk, blk), lambda i, g, t: (0, 0)),
        ],
        out_specs=pl.BlockSpec((1, blk, width), lambda i, g, t: (i, t, g)),
        out_shape=jax.ShapeDtypeStruct((b, s, mix), BF16),
        scratch_shapes=[
            pltpu.VMEM((SB_HEADS_PER_STEP, blk, dh), F32),
            pltpu.VMEM((SB_HEADS_PER_STEP, blk, 1), F32),
        ],
        compiler_params=_params("parallel", "parallel", "arbitrary"),
        name="stick_breaking",
    )(qkv, qkv, qkv, tri)


def _hgrn2_kernel(hn_ref, wp_ref, lbl_ref, ng_ref, sel_ref, o_ref,
                  st_ref, vg_ref, qkb_ref, diag_ref, *, layer, tb, chunk, sub):
    n_b = hn_ref.shape[0]
    mix = o_ref.shape[-1]
    hk = HG_HEAD_DIM
    n_head = mix // hk
    n_sub = chunk // sub
    grp = SUBLANES
    n_grp = sub // grp

    @pl.when(pl.program_id(0) == 0)
    def _():
        st_ref[...] = jnp.zeros(st_ref.shape, F32)

    lbl = lbl_ref[...]
    p = jnp.exp(lbl - jnp.max(lbl, axis=0, keepdims=True))
    p = p / jnp.sum(p, axis=0, keepdims=True)
    lb = jnp.zeros((1, mix), F32)
    for m in range(1, layer + 1):
        lb = lb + p[m:m + 1, :]
    log_lb = jnp.log(jnp.maximum(lb, LB_FLOOR))
    log_1m_lb = jnp.log1p(-lb)

    row = lax.broadcasted_iota(jnp.int32, (chunk, chunk), 0)
    col = lax.broadcasted_iota(jnp.int32, (chunk, chunk), 1)
    same_sub = (row // sub) == (col // sub)
    later_sub = (row // sub) > (col // sub)
    causal = col <= row
    trow = lax.broadcasted_iota(jnp.int32, (chunk, mix), 0)

    def stage1(c, slot):
        rows = pl.ds(pl.multiple_of(c * chunk, chunk), chunk)
        proj = jnp.dot(jnp.concatenate([hn_ref[bi, rows, :] for bi in range(n_b)], axis=0),
                       wp_ref[...], preferred_element_type=F32)
        per_b = []
        span = None
        for bi in range(n_b):
            brows = slice(bi * chunk, (bi + 1) * chunk)
            q_raw = proj[brows, 0:mix]
            f_pre = proj[brows, mix:2 * mix]
            vg_ref[bi, rows, :] = proj[brows, 2 * mix:4 * mix]
            q = q_raw * jax.nn.sigmoid(q_raw)
            t2 = log_1m_lb - (jnp.maximum(-f_pre, 0.0) + _log1p_exp_neg_abs(f_pre))
            log_f = jnp.maximum(log_lb, t2) + _log1p_exp_neg_abs(log_lb - t2)
            k = (1.0 - lb) * jax.nn.sigmoid(-f_pre)
            bcum = log_f
            d = 1
            while d < chunk:
                bcum = bcum + jnp.where(trow >= d, pltpu.roll(bcum, d, 0), 0.0)
                d *= 2
            qkb_ref[slot, bi, 0] = q
            qkb_ref[slot, bi, 1] = k
            qkb_ref[slot, bi, 2] = bcum
            b3 = bcum.reshape(n_sub, sub, mix)
            q3 = q.reshape(n_sub, sub, mix)
            k3 = k.reshape(n_sub, sub, mix)
            ref3 = jnp.concatenate([jnp.zeros((1, 1, mix), F32), b3[:n_sub - 1, sub - 1:sub, :]], axis=0)
            drop = jnp.max(ref3 - b3[:, sub - 1:sub, :])
            span = drop if span is None else jnp.maximum(span, drop)
            per_b.append((q3, k3, b3, ref3))

        def factored():
            for bi, (q3, k3, b3, ref3) in enumerate(per_b):
                q_s = (q3 * jnp.exp(b3 - ref3)).reshape(chunk, mix).astype(BF16)
                k_s = (k3 * jnp.exp(ref3 - b3)).reshape(chunk, mix).astype(BF16)
                for hd in range(n_head):
                    hl = slice(hd * hk, (hd + 1) * hk)
                    diag_ref[slot, bi, hd] = lax.dot_general(
                        q_s[:, hl], k_s[:, hl], (((1,), (1,)), ((), ())), preferred_element_type=F32)

        def pairwise():
            for bi, (q3, k3, b3, ref3) in enumerate(per_b):
                b3s = b3 * LOG2_E
                tiles = tuple([] for _ in range(n_grp))
                for tl in range(sub):
                    b_t = b3s[:, tl:tl + 1, :]
                    q_t = q3[:, tl:tl + 1, :]
                    for gi in range(tl // grp + 1):
                        srows = slice(gi * grp, (gi + 1) * grp)
                        dec = jnp.exp2(jnp.minimum(b_t - b3s[:, srows, :], 0.0))
                        tile = (dec * (q_t * k3[:, srows, :])).reshape(n_sub * grp, mix).astype(BF16)
                        tiles[gi].append(tile)
                sums = []
                for gi in range(n_grp):
                    lhs = jnp.concatenate(
                        [jnp.concatenate([t[:, hd * hk:(hd + 1) * hk] for t in tiles[gi]], axis=1)
                         for hd in range(n_head)], axis=0)
                    sums.append(jnp.dot(lhs, sel_ref[gi * grp * hk:, :], preferred_element_type=F32))
                for hd in range(n_head):
                    r0 = hd * n_sub * grp
                    by_key = jnp.concatenate(
                        [sums[gi][r0 + j * grp:r0 + (j + 1) * grp, :]
                         for j in range(n_sub) for gi in range(n_grp)], axis=0)
                    diag_ref[slot, bi, hd] = by_key.T

        factored()
        pl.when(span > HG_SAFE_SPAN)(pairwise)

    def stage2(c, slot):
        rows = pl.ds(pl.multiple_of(c * chunk, chunk), chunk)
        for bi in range(n_b):
            q = qkb_ref[slot, bi, 0]
            k = qkb_ref[slot, bi, 1]
            bcum = qkb_ref[slot, bi, 2]
            b_last = bcum[chunk - 1:chunk, :]
            vb = vg_ref[bi, rows, 0:mix].astype(BF16)
            q_in = (q * jnp.exp(bcum)).astype(BF16)
            k_st = (k * jnp.exp(b_last - bcum)).astype(BF16)
            decay_last = jnp.exp(b_last)
            g_raw = vg_ref[bi, rows, mix:2 * mix]
            for hd in range(n_head):
                hl = slice(hd * hk, (hd + 1) * hk)
                aw = jnp.where(same_sub & causal, diag_ref[slot, bi, hd], 0.0)
                k_rows, q_cols = [], []
                for j in range(n_sub - 1):
                    r_j = bcum[(j + 1) * sub - 1:(j + 1) * sub, hl]
                    srows = slice(j * sub, (j + 1) * sub)
                    trows = slice((j + 1) * sub, chunk)
                    k_hat = (k[srows, hl] * jnp.exp(r_j - bcum[srows, hl])).astype(BF16)
                    q_j = (q[trows, hl] * jnp.exp(bcum[trows, hl] - r_j)).astype(BF16)
                    q_cols.append(jnp.concatenate([jnp.zeros(((j + 1) * sub, hk), BF16), q_j], axis=0))
                    k_rows.append(jnp.concatenate(
                        [k_hat if jj == j else jnp.zeros((sub, hk), BF16) for jj in range(n_sub - 1)],
                        axis=1))
                if k_rows:
                    k_rows.append(jnp.zeros((sub, (n_sub - 1) * hk), BF16))
                    a_off = lax.dot_general(jnp.concatenate(q_cols, axis=1),
                                            jnp.concatenate(k_rows, axis=0),
                                            (((1,), (1,)), ((), ())), preferred_element_type=F32)
                    aw = jnp.where(later_sub, a_off, aw)
                st = st_ref[bi * n_head + hd]
                o = jnp.dot(aw.astype(BF16), vb[:, hl], preferred_element_type=F32)
                o = o + lax.dot_general(q_in[:, hl], st.astype(BF16), (((1,), (1,)), ((), ())),
                                        preferred_element_type=F32)
                st_ref[bi * n_head + hd] = decay_last[:, hl] * st + lax.dot_general(
                    vb[:, hl], k_st[:, hl], (((0,), (0,)), ((), ())), preferred_element_type=F32)
                o = _rms_scale(o, ng_ref[...])
                g_h = g_raw[:, hl]
                o_ref[bi, rows, hl] = (o * (g_h * jax.nn.sigmoid(g_h))).astype(o_ref.dtype)

    n_chunk = tb // chunk
    stage1(0, 0)

    def body(c, _):
        stage2(c - 1, lax.rem(c - 1, 2))
        stage1(c, lax.rem(c, 2))
        return 0

    lax.fori_loop(1, n_chunk, body, 0)
    stage2(n_chunk - 1, (n_chunk - 1) % 2)


def _hgrn2(hn, w_bf, col0, lb_logits, norm_gain, sel, layer):
    b, s, d = hn.shape
    mix = lb_logits.shape[1]
    wcb = col0 // (4 * mix)
    tb = min(HG_TB, s)
    chunk, sub = HG_CHUNK, HG_SUB
    assert sub % (2 * SUBLANES) == 0 and chunk % sub == 0 and tb % chunk == 0
    n_head = mix // HG_HEAD_DIM
    depth = lb_logits.shape[0]
    return pl.pallas_call(
        functools.partial(_hgrn2_kernel, layer=layer, tb=tb, chunk=chunk, sub=sub),
        grid=(s // tb,),
        in_specs=[
            pl.BlockSpec((b, tb, d), lambda j: (0, j, 0)),
            pl.BlockSpec((None, d, 4 * mix), lambda j: (layer, 0, wcb)),
            pl.BlockSpec((depth, mix), lambda j: (0, 0)),
            pl.BlockSpec((1, HG_HEAD_DIM), lambda j: (0, 0)),
            pl.BlockSpec((sub * HG_HEAD_DIM, chunk), lambda j: (0, 0)),
        ],
        out_specs=pl.BlockSpec((b, tb, mix), lambda j: (0, j, 0)),
        out_shape=jax.ShapeDtypeStruct((b, s, mix), BF16),
        scratch_shapes=[
            pltpu.VMEM((b * n_head, HG_HEAD_DIM, HG_HEAD_DIM), F32),
            pltpu.VMEM((b, tb, 2 * mix), F32),
            pltpu.VMEM((2, b, 3, chunk, mix), F32),
            pltpu.VMEM((2, b, n_head, chunk, chunk), F32),
        ],
        compiler_params=_params("arbitrary"),
        name="hgrn2",
    )(hn, w_bf, lb_logits, norm_gain, sel)


def _merge_kernel(ya_ref, yb_ref, yc_ref, hn_ref, x_ref, wga_ref, wgb_ref, wgc_ref,
                  wbr_ref, wout_ref, gain_ref, o_ref):
    hn = hn_ref[...]
    merged = None
    for idx, (y_ref, wg_ref) in enumerate(((ya_ref, wga_ref), (yb_ref, wgb_ref), (yc_ref, wgc_ref))):
        gate = jnp.dot(hn, wg_ref[...], preferred_element_type=F32)
        part = jnp.dot(y_ref[...].astype(BF16), wbr_ref[idx], preferred_element_type=F32)
        part = jax.nn.sigmoid(gate) * part
        merged = part if merged is None else merged + part
    out = jnp.dot(merged.astype(BF16), wout_ref[...], preferred_element_type=F32)
    o_ref[...] = x_ref[...] + _rms_scale(out, gain_ref[...])


def _merge(ya, yb, yc, hn2, x2, w_bf, layer, gate_col0, wbr_bf, wout_bf, gain):
    t, d = x2.shape
    mix = ya.shape[1]
    tm = min(MERGE_TM, t)
    gcb = gate_col0 // d
    ybs = lambda: pl.BlockSpec((tm, mix), lambda i: (i, 0))
    row = lambda: pl.BlockSpec((tm, d), lambda i: (i, 0))
    wgate = lambda off: pl.BlockSpec((None, d, d), lambda i: (layer, 0, gcb + off))
    return pl.pallas_call(
        _merge_kernel,
        grid=(t // tm,),
        in_specs=[
            ybs(), ybs(), ybs(), row(), row(),
            wgate(0), wgate(1), wgate(2),
            pl.BlockSpec((None,) + wbr_bf.shape[1:], lambda i: (layer, 0, 0, 0)),
            pl.BlockSpec((None, d, d), lambda i: (layer, 0, 0)),
            pl.BlockSpec((1, d), lambda i: (0, 0)),
        ],
        out_specs=row(),
        out_shape=jax.ShapeDtypeStruct((t, d), F32),
        compiler_params=_params("parallel"),
        name="merge_out",
    )(ya, yb, yc, hn2, x2, w_bf, w_bf, w_bf, wbr_bf, wout_bf, gain)


def _mlp_kernel(x_ref, g_in_ref, wup_ref, wdn_ref, g_out_ref, o_ref):
    x = x_ref[...]
    h = _rms_scale(x, g_in_ref[...]).astype(BF16)
    u = jnp.maximum(jnp.dot(h, wup_ref[...], preferred_element_type=F32), 0.0)
    m = jnp.dot((u * u).astype(BF16), wdn_ref[...], preferred_element_type=F32)
    o_ref[...] = x + _rms_scale(m, g_out_ref[...])


def _mlp(x2, g_in, wup_bf, wdn_bf, layer, g_out):
    t, d = x2.shape
    dff = wup_bf.shape[-1]
    tm = min(MLP_TM, t)
    resident = dict(pipeline_mode=pl.Buffered(1))
    return pl.pallas_call(
        _mlp_kernel,
        grid=(t // tm,),
        in_specs=[
            pl.BlockSpec((tm, d), lambda i: (i, 0)),
            pl.BlockSpec((1, d), lambda i: (0, 0)),
            pl.BlockSpec((None, d, dff), lambda i: (layer, 0, 0), **resident),
            pl.BlockSpec((None, dff, d), lambda i: (layer, 0, 0), **resident),
            pl.BlockSpec((1, d), lambda i: (0, 0)),
        ],
        out_specs=pl.BlockSpec((tm, d), lambda i: (i, 0)),
        out_shape=jax.ShapeDtypeStruct((t, d), F32),
        compiler_params=_params("parallel"),
        name="mlp",
    )(x2, g_in, wup_bf, wdn_bf, g_out)


def _block_diag(w):
    n, c, _ = w.shape
    eye = jnp.eye(n, dtype=w.dtype)
    return (eye[:, None, :, None] * w[:, :, None, :]).reshape(n * c, n * c)


def kernel(x, ln_gains, w_in, conv_w, conv_b, rg_w_a, rg_b_a, rg_w_x, rg_b_x, rg_lambda,
           lb_logits, hgrn_norm, w_branch, w_out, w_up, w_down):
    b, s, d = x.shape
    depth = w_in.shape[0]
    mix = conv_w.shape[-1]
    t = b * s
    hg_lo, gate_lo = 4 * mix, 8 * mix

    blk = min(SB_BLK, s)
    tri = (lax.broadcasted_iota(jnp.int32, (blk, blk), 0)
           > lax.broadcasted_iota(jnp.int32, (blk, blk), 1)).astype(BF16)
    sel_r = lax.broadcasted_iota(jnp.int32, (HG_SUB * HG_HEAD_DIM, HG_CHUNK), 0) // HG_HEAD_DIM
    sel_c = lax.broadcasted_iota(jnp.int32, (HG_SUB * HG_HEAD_DIM, HG_CHUNK), 1) % HG_SUB
    sel = (sel_r == sel_c).astype(BF16)

    w_in_bf, w_br_bf, w_out_bf = w_in.astype(BF16), w_branch.astype(BF16), w_out.astype(BF16)
    w_up_bf, w_dn_bf = w_up.astype(BF16), w_down.astype(BF16)

    x2 = x.reshape(t, d)
    for l in range(depth):
        g = ln_gains[l]
        y_a, hn, qkv = _norm_qkv_rglru(x2.reshape(b, s, d), g[0:1], w_in_bf, l, conv_w[l], conv_b[l:l + 1],
                                       _block_diag(rg_w_a[l]).astype(BF16), rg_b_a[l:l + 1],
                                       _block_diag(rg_w_x[l]).astype(BF16), rg_b_x[l:l + 1],
                                       rg_lambda[l:l + 1])
        y_b = _stick_breaking(qkv, tri, mix)
        y_c = _hgrn2(hn, w_in_bf, hg_lo, lb_logits, hgrn_norm[l:l + 1], sel, l)
        x2 = _merge(y_a.reshape(t, mix), y_b.reshape(t, mix), y_c.reshape(t, mix), hn.reshape(t, d), x2,
                    w_in_bf, l, gate_lo, w_br_bf, w_out_bf, g[1:2])
        x2 = _mlp(x2, g[2:3], w_up_bf, w_dn_bf, l, g[3:4])
    return x2.reshape(b, s, d)
```
